```python
import jax, jax.numpy as jnp
from jax import lax
import numpy as np

D_MODEL = 1024
BATCH = 8
SEQ = 2048
DEPTH = 2

D_MIX = 2 * D_MODEL
CONV_W = D_MIX // 4
CONV_K = 31
RET_W = D_MIX // 4
RET_HEADS = 4
RET_HD = RET_W // RET_HEADS
RET_CHUNK = 128
MLA_W = D_MIX // 2
MLA_HEADS = 8
MLA_V_HD = MLA_W // MLA_HEADS
MLA_NOPE = 128
MLA_ROPE = 64
MLA_Q_RANK = 384
MLA_KV_RANK = 256
ATTN_BLOCK = 128
ROPE_BASE = 10000.0
EPS = 1e-6

OFF_RET = 2 * CONV_W
OFF_QLAT = OFF_RET + 3 * RET_W
OFF_KVLAT = OFF_QLAT + MLA_Q_RANK
OFF_KROPE = OFF_KVLAT + MLA_KV_RANK
OFF_GATE = OFF_KROPE + MLA_ROPE
N_IN = OFF_GATE + D_MIX
SPLIT_POINTS = (OFF_RET, OFF_QLAT, OFF_KVLAT, OFF_KROPE, OFF_GATE)

kernel_name = "hybrid_conv_retention_mla_encoder"


def rmsnorm(x, g):
    xf = x.astype(jnp.float32)
    y = xf * lax.rsqrt(jnp.mean(xf * xf, axis=-1, keepdims=True) + EPS)
    return (y * g.astype(jnp.float32)).astype(x.dtype)


def rope_tables(seq, dim, dtype):
    inv = 1.0 / (ROPE_BASE ** (jnp.arange(0, dim, 2, dtype=jnp.float32) / dim))
    ang = jnp.arange(seq, dtype=jnp.float32)[:, None] * inv[None, :]
    ang = jnp.concatenate([ang, ang], axis=-1)
    return jnp.cos(ang).astype(dtype), jnp.sin(ang).astype(dtype)


def apply_rope(x, cos, sin):
    x1, x2 = jnp.split(x, 2, axis=-1)
    return x * cos + jnp.concatenate([-x2, x1], axis=-1) * sin


def conv_module(u, dw_w, dw_b, ln_g, ln_b):
    a, b = jnp.split(u, 2, axis=-1)
    h = a * jax.nn.sigmoid(b)
    h = lax.conv_general_dilated(
        h, dw_w[:, None, :].astype(h.dtype), window_strides=(1,),
        padding=[(CONV_K // 2, CONV_K // 2)],
        dimension_numbers=('NWC', 'WIO', 'NWC'),
        feature_group_count=CONV_W) + dw_b.astype(h.dtype)
    hf = h.astype(jnp.float32)
    mu = jnp.mean(hf, axis=-1, keepdims=True)
    var = jnp.mean(jnp.square(hf - mu), axis=-1, keepdims=True)
    hn = (hf - mu) * lax.rsqrt(var + EPS) * ln_g.astype(jnp.float32) + ln_b.astype(jnp.float32)
    return jax.nn.silu(hn).astype(u.dtype)


def retention_scan(q, k, v, log_g, strict):
    B, H, S, D = q.shape
    C = RET_CHUNK
    N = S // C
    qc = q.reshape(B, H, N, C, D)
    kc = k.reshape(B, H, N, C, D)
    vc = v.reshape(B, H, N, C, D)
    idx = jnp.arange(C, dtype=jnp.float32)
    diff = idx[:, None] - idx[None, :]
    mask = (diff > 0) if strict else (diff >= 0)
    decay_in = jnp.where(mask[None], jnp.exp(log_g[:, None, None] * jnp.maximum(diff, 0.0)[None]), 0.0)
    s = jnp.einsum('bhnid,bhnjd->bhnij', qc, kc) * decay_in[None, :, None]
    inner = jnp.einsum('bhnij,bhnjd->bhnid', s, vc)
    k_dec = jnp.exp(log_g[:, None] * (C - 1 - idx)[None, :])
    kv = jnp.einsum('bhnjd,bhnje->nbhde', kc * k_dec[None, :, None, :, None], vc)
    chunk_dec = jnp.exp(log_g * C)[None, :, None, None]

    def step(state, kv_n):
        return chunk_dec * state + kv_n, state

    _, prev = lax.scan(step, jnp.zeros((B, H, D, D), q.dtype), kv)
    q_dec = jnp.exp(log_g[:, None] * (idx + 1.0)[None, :])
    cross = jnp.einsum('bhnid,nbhde->bhnie', qc * q_dec[None, :, None, :, None], prev)
    return (inner + cross).reshape(B, H, S, D)


def retention_branch(u_qkv, decay_logit, cos, sin):
    B, S, _ = u_qkv.shape
    q, k, v = jnp.split(u_qkv, 3, axis=-1)
    q = q.reshape(B, S, RET_HEADS, RET_HD)
    k = k.reshape(B, S, RET_HEADS, RET_HD)
    v = v.reshape(B, S, RET_HEADS, RET_HD)
    q = apply_rope(q, cos[:, None, :], sin[:, None, :])
    k = apply_rope(k, cos[:, None, :], sin[:, None, :]) * (RET_HD ** -0.5)
    q, k, v = [t.transpose(0, 2, 1, 3).astype(jnp.float32) for t in (q, k, v)]
    log_g = jax.nn.log_sigmoid(decay_logit.astype(jnp.float32))
    flip = lambda t: jnp.flip(t, axis=2)
    o = retention_scan(q, k, v, log_g[0], False) + flip(
        retention_scan(flip(q), flip(k), flip(v), log_g[1], True))
    mu = jnp.mean(o, axis=-1, keepdims=True)
    var = jnp.mean(jnp.square(o - mu), axis=-1, keepdims=True)
    o = (o - mu) * lax.rsqrt(var + EPS)
    return o.transpose(0, 2, 1, 3).reshape(B, S, RET_W).astype(u_qkv.dtype)


def mla_branch(q_lat, kv_lat, k_rope_raw, qa_g, w_uq, kva_g, w_ukv, cos, sin):
    B, S, _ = q_lat.shape
    q = (rmsnorm(q_lat, qa_g) @ w_uq).reshape(B, S, MLA_HEADS, MLA_NOPE + MLA_ROPE)
    q_nope = q[..., :MLA_NOPE]
    q_rope = apply_rope(q[..., MLA_NOPE:], cos[:, None, :], sin[:, None, :])
    kv = (rmsnorm(kv_lat, kva_g) @ w_ukv).reshape(B, S, MLA_HEADS, MLA_NOPE + MLA_V_HD)
    k_nope = kv[..., :MLA_NOPE]
    v = kv[..., MLA_NOPE:]
    k_rope = apply_rope(k_rope_raw, cos, sin)
    scale = (MLA_NOPE + MLA_ROPE) ** -0.5
    nb = S // ATTN_BLOCK

    def to_blocks(t):
        return t.reshape(B, nb, ATTN_BLOCK, *t.shape[2:]).swapaxes(0, 1)

    def attend(blk):
        qn, qr = blk
        s = (jnp.einsum('bqhd,bkhd->bhqk', qn, k_nope)
             + jnp.einsum('bqhr,bkr->bhqk', qr, k_rope))
        p = jax.nn.softmax(s.astype(jnp.float32) * scale, axis=-1).astype(v.dtype)
        return jnp.einsum('bhqk,bkhd->bqhd', p, v)

    o = lax.map(attend, (to_blocks(q_nope), to_blocks(q_rope)))
    return o.swapaxes(0, 1).reshape(B, S, MLA_W)


def setup_inputs(seed: int = 0) -> dict:
    key = jax.random.key(seed)
    ks = jax.random.split(key, 16)
    f32 = jnp.float32
    nrm = lambda k, shape, s: jax.random.normal(k, shape, f32) * s
    base_logit = jnp.log(2.0 ** (5.0 + jnp.arange(RET_HEADS, dtype=f32)) - 1.0)
    return {
        "x": nrm(ks[0], (BATCH, SEQ, D_MODEL), 1.0),
        "norm_g": 1.0 + nrm(ks[1], (DEPTH, D_MODEL), 0.02),
        "w_in": nrm(ks[2], (DEPTH, D_MODEL, N_IN), D_MODEL ** -0.5),
        "conv_dw_w": nrm(ks[3], (DEPTH, CONV_K, CONV_W), CONV_K ** -0.5),
        "conv_dw_b": nrm(ks[4], (DEPTH, CONV_W), 0.02),
        "conv_ln_g": 1.0 + nrm(ks[5], (DEPTH, CONV_W), 0.02),
        "conv_ln_b": nrm(ks[6], (DEPTH, CONV_W), 0.02),
        "ret_decay_logit": base_logit[None, None, :] + nrm(ks[7], (DEPTH, 2, RET_HEADS), 0.1),
        "mla_qa_g": 1.0 + nrm(ks[8], (DEPTH, MLA_Q_RANK), 0.02),
        "mla_w_uq": nrm(ks[9], (DEPTH, MLA_Q_RANK, MLA_HEADS * (MLA_NOPE + MLA_ROPE)), MLA_Q_RANK ** -0.5),
        "mla_kva_g": 1.0 + nrm(ks[10], (DEPTH, MLA_KV_RANK), 0.02),
        "mla_w_ukv": nrm(ks[11], (DEPTH, MLA_KV_RANK, MLA_HEADS * (MLA_NOPE + MLA_V_HD)), MLA_KV_RANK ** -0.5),
        "w_out": nrm(ks[12], (DEPTH, D_MIX, D_MODEL), D_MIX ** -0.5),
        "final_g": 1.0 + nrm(ks[13], (D_MODEL,), 0.02),
    }


def reference(x, norm_g, w_in, conv_dw_w, conv_dw_b, conv_ln_g, conv_ln_b, ret_decay_logit,
              mla_qa_g, mla_w_uq, mla_kva_g, mla_w_ukv, w_out, final_g):
    S = x.shape[1]
    cos_r, sin_r = rope_tables(S, RET_HD, x.dtype)
    cos_m, sin_m = rope_tables(S, MLA_ROPE, x.dtype)
    for l in range(DEPTH):
        h = rmsnorm(x, norm_g[l])
        u = h @ w_in[l]
        u_conv, u_ret, u_q, u_kv, u_kr, u_gate = jnp.split(u, SPLIT_POINTS, axis=-1)
        y_conv = conv_module(u_conv, conv_dw_w[l], conv_dw_b[l], conv_ln_g[l], conv_ln_b[l])
        y_ret = retention_branch(u_ret, ret_decay_logit[l], cos_r, sin_r)
        y_mla = mla_branch(u_q, u_kv, u_kr, mla_qa_g[l], mla_w_uq[l], mla_kva_g[l], mla_w_ukv[l], cos_m, sin_m)
        y = jnp.concatenate([y_conv, y_ret, y_mla], axis=-1) * jax.nn.silu(u_gate)
        x = x + y @ w_out[l]
    return rmsnorm(x, final_g)
```

```python
import functools
import math

import jax
import jax.numpy as jnp
from jax import lax
from jax.experimental import pallas as pl
from jax.experimental.pallas import tpu as pltpu

D_MODEL = 1024
D_MIX = 2 * D_MODEL
CONV_W = 512
CONV_K = 31
RET_W = 512
RET_HEADS = 4
RET_HD = 128
RET_CHUNK = 128
MLA_W = 1024
MLA_HEADS = 8
MLA_V_HD = 128
MLA_NOPE = 128
MLA_ROPE = 64
MLA_Q_RANK = 384
MLA_KV_RANK = 256
ROPE_BASE = 10000.0
EPS = 1e-6

OFF_RET = 2 * CONV_W
OFF_QLAT = OFF_RET + 3 * RET_W
OFF_KVLAT = OFF_QLAT + MLA_Q_RANK
OFF_KROPE = OFF_KVLAT + MLA_KV_RANK
OFF_GATE = OFF_KROPE + MLA_ROPE
N_IN = OFF_GATE + D_MIX

LANES = 128
MLA_QK_PAD = 2 * LANES
CONV_HALO = 16
VMEM_LIMIT = 56 * 1024 * 1024

P_KROPE = OFF_KROPE
P_GATE = P_KROPE + LANES
P_N_IN = P_GATE + D_MIX

BF16 = jnp.bfloat16
F32 = jnp.float32


def _rope_tables(seq, dim):
    inv = 1.0 / (ROPE_BASE ** (jnp.arange(0, dim, 2, dtype=F32) / dim))
    ang = jnp.arange(seq, dtype=F32)[:, None] * inv[None, :]
    ang = jnp.concatenate([ang, ang], axis=-1)
    return jnp.cos(ang), jnp.sin(ang)


def _rmsnorm(x, g):
    return x * lax.rsqrt(jnp.mean(x * x, axis=-1, keepdims=True) + EPS) * g


def _inproj_kernel(x_ref, g_ref, win_ref, wuq_ref, wukv_ref, qag_ref, kvag_ref,
                   cosr_ref, sinr_ref, cosm_ref, sinma_ref, sinmb_ref,
                   hglu_ref, rqkv_ref, mq_ref, mk_ref, mv_ref, gate_ref):
    h = _rmsnorm(x_ref[...], g_ref[...]).astype(BF16)

    def proj(lo, hi):
        return jnp.dot(h, win_ref[:, lo:hi], preferred_element_type=F32)

    a = proj(0, CONV_W)
    b = proj(CONV_W, 2 * CONV_W)
    hglu_ref[...] = (a * jax.nn.sigmoid(b)).astype(BF16)

    cosr = cosr_ref[...]
    sinr = sinr_ref[...]
    k_scale = RET_HD ** -0.5
    for part in range(3):
        u = proj(OFF_RET + part * RET_W, OFF_RET + (part + 1) * RET_W)
        for hd in range(RET_HEADS):
            blk = u[:, hd * RET_HD:(hd + 1) * RET_HD]
            if part < 2:
                blk = blk * cosr + pltpu.roll(blk, RET_HD // 2, 1) * sinr
            if part == 1:
                blk = blk * k_scale
            col = part * RET_W + hd * RET_HD
            rqkv_ref[:, col:col + RET_HD] = blk.astype(BF16)

    cosm = cosm_ref[...]
    sinma = sinma_ref[...]
    sinmb = sinmb_ref[...]

    def rope64(blk):
        return (blk * cosm + pltpu.roll(blk, LANES - MLA_ROPE // 2, 1) * sinma
                + pltpu.roll(blk, MLA_ROPE // 2, 1) * sinmb)

    q_lat = proj(OFF_QLAT, OFF_KVLAT)
    qn = _rmsnorm(q_lat, qag_ref[...]).astype(BF16)
    q_scale = (MLA_NOPE + MLA_ROPE) ** -0.5 * math.log2(math.e)
    for hd in range(MLA_HEADS):
        qh = jnp.dot(qn, wuq_ref[:, hd * MLA_QK_PAD:(hd + 1) * MLA_QK_PAD], preferred_element_type=F32)
        mq_ref[:, hd * MLA_QK_PAD:hd * MLA_QK_PAD + LANES] = (qh[:, :LANES] * q_scale).astype(BF16)
        mq_ref[:, hd * MLA_QK_PAD + LANES:(hd + 1) * MLA_QK_PAD] = (rope64(qh[:, LANES:]) * q_scale).astype(BF16)

    kv_lat = proj(OFF_KVLAT, OFF_KROPE)
    kvn = _rmsnorm(kv_lat, kvag_ref[...]).astype(BF16)
    k_rope = rope64(proj(P_KROPE, P_GATE)).astype(BF16)
    for hd in range(MLA_HEADS):
        kh = jnp.dot(kvn, wukv_ref[:, hd * MLA_NOPE:(hd + 1) * MLA_NOPE], preferred_element_type=F32)
        mk_ref[:, hd * MLA_QK_PAD:hd * MLA_QK_PAD + LANES] = kh.astype(BF16)
        mk_ref[:, hd * MLA_QK_PAD + LANES:(hd + 1) * MLA_QK_PAD] = k_rope
    v_off = MLA_HEADS * MLA_NOPE
    mv_ref[...] = jnp.dot(kvn, wukv_ref[:, v_off:v_off + MLA_W], preferred_element_type=F32).astype(BF16)

    for c in range(D_MIX // CONV_W):
        gt = proj(P_GATE + c * CONV_W, P_GATE + (c + 1) * CONV_W)
        gate_ref[:, c * CONV_W:(c + 1) * CONV_W] = (gt * jax.nn.sigmoid(gt)).astype(BF16)


def _inproj(x2, g, win, wuq, wukv, qag, kvag, tabs, seq, tm):
    tokens = x2.shape[0]
    nseq = seq // tm
    cosr, sinr, cosm, sinma, sinmb = tabs
    row = lambda i: (i, 0)
    const = lambda i: (0, 0)
    pos = lambda i: (i % nseq, 0)
    return pl.pallas_call(
        _inproj_kernel,
        grid=(tokens // tm,),
        in_specs=[
            pl.BlockSpec((tm, D_MODEL), row),
            pl.BlockSpec((1, D_MODEL), const),
            pl.BlockSpec((D_MODEL, P_N_IN), const),
            pl.BlockSpec((MLA_Q_RANK, MLA_HEADS * MLA_QK_PAD), const),
            pl.BlockSpec((MLA_KV_RANK, 2 * MLA_W), const),
            pl.BlockSpec((1, MLA_Q_RANK), const),
            pl.BlockSpec((1, MLA_KV_RANK), const),
            pl.BlockSpec((tm, LANES), pos),
            pl.BlockSpec((tm, LANES), pos),
            pl.BlockSpec((tm, LANES), pos),
            pl.BlockSpec((tm, LANES), pos),
            pl.BlockSpec((tm, LANES), pos),
        ],
        out_specs=[
            pl.BlockSpec((tm, CONV_W), row),
            pl.BlockSpec((tm, 3 * RET_W), row),
            pl.BlockSpec((tm, MLA_HEADS * MLA_QK_PAD), row),
            pl.BlockSpec((tm, MLA_HEADS * MLA_QK_PAD), row),
            pl.BlockSpec((tm, MLA_W), row),
            pl.BlockSpec((tm, D_MIX), row),
        ],
        out_shape=[
            jax.ShapeDtypeStruct((tokens, CONV_W), BF16),
            jax.ShapeDtypeStruct((tokens, 3 * RET_W), BF16),
            jax.ShapeDtypeStruct((tokens, MLA_HEADS * MLA_QK_PAD), BF16),
            jax.ShapeDtypeStruct((tokens, MLA_HEADS * MLA_QK_PAD), BF16),
            jax.ShapeDtypeStruct((tokens, MLA_W), BF16),
            jax.ShapeDtypeStruct((tokens, D_MIX), BF16),
        ],
        compiler_params=pltpu.CompilerParams(
            dimension_semantics=("arbitrary",), vmem_limit_bytes=VMEM_LIMIT),
        name="inproj",
    )(x2, g, win, wuq, wukv, qag, kvag, cosr, sinr, cosm, sinma, sinmb)


def _conv_kernel(h_ref, w_ref, b_ref, lg_ref, lb_ref, o_ref, pad_ref, *, seq, rows):
    zeros = jnp.zeros((CONV_HALO, CONV_W), F32)
    pad_ref[0:CONV_HALO, :] = zeros
    pad_ref[CONV_HALO + seq:, :] = zeros
    pad_ref[CONV_HALO:CONV_HALO + seq, :] = h_ref[...].astype(F32)
    first_tap = CONV_HALO - CONV_K // 2

    def tile(t, carry):
        r0 = pl.multiple_of(t * rows, rows)
        accs = []
        for c in range(CONV_W // LANES):
            lanes = slice(c * LANES, (c + 1) * LANES)
            win = pad_ref[pl.ds(r0, rows + 2 * CONV_HALO), lanes]
            acc = jnp.broadcast_to(b_ref[:, lanes], (rows, LANES))
            for shift in range(8):
                taps = [k for k in range(CONV_K) if (first_tap + k) % 8 == shift]
                span = (first_tap + taps[-1]) - shift + rows
                shifted = win[shift:shift + span, :]
                for k in taps:
                    off = first_tap + k - shift
                    acc = acc + shifted[off:off + rows, :] * w_ref[k:k + 1, lanes]
            accs.append(acc)
        acc = jnp.concatenate(accs, axis=1)
        mu = jnp.mean(acc, axis=-1, keepdims=True)
        cen = acc - mu
        var = jnp.mean(cen * cen, axis=-1, keepdims=True)
        hn = cen * lax.rsqrt(var + EPS) * lg_ref[...] + lb_ref[...]
        o_ref[pl.ds(r0, rows), :] = (hn * jax.nn.sigmoid(hn)).astype(o_ref.dtype)
        return carry

    lax.fori_loop(0, seq // rows, tile, 0)


def _conv(hglu, w, b, lg, lb, batch, seq):
    rows = 64
    const = lambda i: (0, 0)
    return pl.pallas_call(
        functools.partial(_conv_kernel, seq=seq, rows=rows),
        grid=(batch,),
        in_specs=[
            pl.BlockSpec((seq, CONV_W), lambda i: (i, 0)),
            pl.BlockSpec((CONV_K, CONV_W), const),
            pl.BlockSpec((1, CONV_W), const),
            pl.BlockSpec((1, CONV_W), const),
            pl.BlockSpec((1, CONV_W), const),
        ],
        out_specs=pl.BlockSpec((seq, CONV_W), lambda i: (i, 0)),
        out_shape=jax.ShapeDtypeStruct(hglu.shape, BF16),
        scratch_shapes=[pltpu.VMEM((seq + 2 * CONV_HALO, CONV_W), F32)],
        compiler_params=pltpu.CompilerParams(dimension_semantics=("arbitrary",)),
        name="conv",
    )(hglu, w, b, lg, lb)


def _log_sigmoid(x):
    return jnp.minimum(x, 0.0) - jnp.log1p(jnp.exp(-jnp.abs(x)))


def _ret_kernel(dl_ref, q_ref, k_ref, v_ref, o_ref, acc_ref, *, seq):
    C = RET_CHUNK
    n_chunks = seq // C
    lg_f = jnp.broadcast_to(_log_sigmoid(dl_ref[0, 0])[0:1, :], (C, LANES))
    lg_b = jnp.broadcast_to(_log_sigmoid(dl_ref[1, 0])[0:1, :], (C, LANES))
    ri = lax.broadcasted_iota(jnp.int32, (C, C), 0)
    ci = lax.broadcasted_iota(jnp.int32, (C, C), 1)
    diff = (ri - ci).astype(F32)
    decay = jnp.where(diff >= 0.0, jnp.exp(lg_f * jnp.maximum(diff, 0.0)), jnp.exp(lg_b * jnp.maximum(-diff, 0.0)))
    idx = lax.broadcasted_iota(jnp.int32, (C, LANES), 0).astype(F32)
    qdec_f = jnp.exp(lg_f * (idx + 1.0))
    kdec_f = jnp.exp(lg_f * (C - 1.0 - idx))
    qdec_b = jnp.exp(lg_b * (C - idx))
    kdec_b = jnp.exp(lg_b * idx)
    cdec_f = jnp.exp(lg_f * float(C))
    cdec_b = jnp.exp(lg_b * float(C))
    trans_b = (((1,), (1,)), ((), ()))
    trans_a = (((0,), (0,)), ((), ()))

    def chunk(n):
        sl = pl.ds(n * C, C)
        return q_ref[sl, :], k_ref[sl, :], v_ref[sl, :]

    state = jnp.zeros((RET_HD, RET_HD), F32)
    for n in range(n_chunks):
        qn, kn, vn = chunk(n)
        s = lax.dot_general(qn, kn, trans_b, preferred_element_type=F32)
        out = jnp.dot((s * decay).astype(BF16), vn, preferred_element_type=F32)
        if n > 0:
            qd = (qn.astype(F32) * qdec_f).astype(BF16)
            out = out + jnp.dot(qd, state.astype(BF16), preferred_element_type=F32)
        acc_ref[pl.ds(n * C, C), :] = out
        if n < n_chunks - 1:
            kd = (kn.astype(F32) * kdec_f).astype(BF16)
            kv = lax.dot_general(kd, vn, trans_a, preferred_element_type=F32)
            state = kv if n == 0 else cdec_f * state + kv

    for n in reversed(range(n_chunks)):
        qn, kn, vn = chunk(n)
        out = acc_ref[pl.ds(n * C, C), :]
        if n < n_chunks - 1:
            qd = (qn.astype(F32) * qdec_b).astype(BF16)
            out = out + jnp.dot(qd, state.astype(BF16), preferred_element_type=F32)
        mu = jnp.mean(out, axis=-1, keepdims=True)
        cen = out - mu
        var = jnp.mean(cen * cen, axis=-1, keepdims=True)
        o_ref[pl.ds(n * C, C), :] = (cen * lax.rsqrt(var + EPS)).astype(o_ref.dtype)
        if n > 0:
            kd = (kn.astype(F32) * kdec_b).astype(BF16)
            kv = lax.dot_general(kd, vn, trans_a, preferred_element_type=F32)
            state = kv if n == n_chunks - 1 else cdec_b * state + kv


def _retention(rqkv, decay_tile, batch, seq):
    blk = lambda part: pl.BlockSpec((seq, RET_HD), lambda b, h: (b, part * RET_HEADS + h))
    return pl.pallas_call(
        functools.partial(_ret_kernel, seq=seq),
        grid=(batch, RET_HEADS),
        in_specs=[
            pl.BlockSpec((2, 1, 8, LANES), lambda b, h: (0, h, 0, 0)),
            blk(0), blk(1), blk(2),
        ],
        out_specs=pl.BlockSpec((seq, RET_HD), lambda b, h: (b, h)),
        out_shape=jax.ShapeDtypeStruct((batch * seq, RET_W), BF16),
        scratch_shapes=[pltpu.VMEM((seq, RET_HD), F32)],
        compiler_params=pltpu.CompilerParams(dimension_semantics=("arbitrary", "arbitrary")),
        name="retention",
    )(decay_tile, rqkv, rqkv, rqkv)


def _mla_kernel(q_ref, k_ref, v_ref, o_ref):
    s = lax.dot_general(q_ref[...], k_ref[...], (((1,), (1,)), ((), ())), preferred_element_type=F32)
    m = jnp.max(s, axis=-1, keepdims=True)
    p = jnp.exp2(s - m)
    l = jnp.sum(p, axis=-1, keepdims=True)
    o = jnp.dot(p.astype(BF16), v_ref[...], preferred_element_type=F32)
    o_ref[...] = (o / l).astype(o_ref.dtype)


def _mla(mq, mk, mv, batch, seq, tq):
    nq = seq // tq
    return pl.pallas_call(
        _mla_kernel,
        grid=(batch, MLA_HEADS, nq),
        in_specs=[
            pl.BlockSpec((tq, MLA_QK_PAD), lambda b, h, i: (b * nq + i, h)),
            pl.BlockSpec((seq, MLA_QK_PAD), lambda b, h, i: (b, h)),
            pl.BlockSpec((seq, MLA_V_HD), lambda b, h, i: (b, h)),
        ],
        out_specs=pl.BlockSpec((tq, MLA_V_HD), lambda b, h, i: (b * nq + i, h)),
        out_shape=jax.ShapeDtypeStruct((batch * seq, MLA_W), BF16),
        compiler_params=pltpu.CompilerParams(
            dimension_semantics=("arbitrary", "arbitrary", "arbitrary"), vmem_limit_bytes=VMEM_LIMIT),
        name="mla_attention",
    )(mq, mk, mv)


def _outproj_kernel(x_ref, yc_ref, yr_ref, ym_ref, gate_ref, w_ref, fg_ref, o_ref, *, final):
    def gated(y_ref, lo, hi):
        return (y_ref[...].astype(F32) * gate_ref[:, lo:hi].astype(F32)).astype(BF16)

    acc = x_ref[...]
    acc = acc + jnp.dot(gated(yc_ref, 0, CONV_W), w_ref[0:CONV_W, :], preferred_element_type=F32)
    acc = acc + jnp.dot(gated(yr_ref, CONV_W, CONV_W + RET_W), w_ref[CONV_W:CONV_W + RET_W, :],
                        preferred_element_type=F32)
    acc = acc + jnp.dot(gated(ym_ref, CONV_W + RET_W, D_MIX), w_ref[CONV_W + RET_W:, :],
                        preferred_element_type=F32)
    if final:
        acc = _rmsnorm(acc, fg_ref[...])
    o_ref[...] = acc


def _outproj(x2, yc, yr, ym, gate, w, fg, tm, final):
    tokens = x2.shape[0]
    row = lambda i: (i, 0)
    const = lambda i: (0, 0)
    return pl.pallas_call(
        functools.partial(_outproj_kernel, final=final),
        grid=(tokens // tm,),
        in_specs=[
            pl.BlockSpec((tm, D_MODEL), row),
            pl.BlockSpec((tm, CONV_W), row),
            pl.BlockSpec((tm, RET_W), row),
            pl.BlockSpec((tm, MLA_W), row),
            pl.BlockSpec((tm, D_MIX), row),
            pl.BlockSpec((D_MIX, D_MODEL), const),
            pl.BlockSpec((1, D_MODEL), const),
        ],
        out_specs=pl.BlockSpec((tm, D_MODEL), row),
        out_shape=jax.ShapeDtypeStruct(x2.shape, F32),
        compiler_params=pltpu.CompilerParams(
            dimension_semantics=("arbitrary",), vmem_limit_bytes=VMEM_LIMIT),
        name="outproj",
    )(x2, yc, yr, ym, gate, w, fg)


def _pad_cols(w, width):
    return jnp.pad(w, ((0, 0), (0, width - w.shape[1])))


def _prep_layer_weights(w_in, w_uq, w_ukv):
    win = jnp.concatenate(
        [w_in[:, :OFF_KROPE], _pad_cols(w_in[:, OFF_KROPE:OFF_GATE], LANES), w_in[:, OFF_GATE:]], axis=1)
    uq = w_uq.reshape(MLA_Q_RANK, MLA_HEADS, MLA_NOPE + MLA_ROPE)
    uq = jnp.pad(uq, ((0, 0), (0, 0), (0, MLA_QK_PAD - MLA_NOPE - MLA_ROPE)))
    uq = uq.reshape(MLA_Q_RANK, MLA_HEADS * MLA_QK_PAD)
    ukv = w_ukv.reshape(MLA_KV_RANK, MLA_HEADS, MLA_NOPE + MLA_V_HD)
    ukv = jnp.concatenate(
        [ukv[:, :, :MLA_NOPE].reshape(MLA_KV_RANK, MLA_HEADS * MLA_NOPE),
         ukv[:, :, MLA_NOPE:].reshape(MLA_KV_RANK, MLA_W)], axis=1)
    return win.astype(BF16), uq.astype(BF16), ukv.astype(BF16)


def _rope_inputs(seq):
    cos_r, sin_r = _rope_tables(seq, RET_HD)
    half = RET_HD // 2
    sin_r = jnp.concatenate([-sin_r[:, :half], sin_r[:, half:]], axis=1)
    cos_m, sin_m = _rope_tables(seq, MLA_ROPE)
    half = MLA_ROPE // 2
    zeros = jnp.zeros((seq, half), F32)
    cosm = _pad_cols(cos_m, LANES)
    sinma = _pad_cols(jnp.concatenate([-sin_m[:, :half], zeros], axis=1), LANES)
    sinmb = _pad_cols(jnp.concatenate([zeros, sin_m[:, half:]], axis=1), LANES)
    return cos_r, sin_r, cosm, sinma, sinmb


def kernel(x, norm_g, w_in, conv_dw_w, conv_dw_b, conv_ln_g, conv_ln_b, ret_decay_logit,
           mla_qa_g, mla_w_uq, mla_kva_g, mla_w_ukv, w_out, final_g):
    batch, seq, d_model = x.shape
    depth = norm_g.shape[0]
    assert d_model == D_MODEL and seq % RET_CHUNK == 0
    tm = 256
    tq = 256
    tabs = _rope_inputs(seq)
    x2 = x.reshape(batch * seq, d_model)
    for l in range(depth):
        win, wuq, wukv = _prep_layer_weights(w_in[l], mla_w_uq[l], mla_w_ukv[l])
        hglu, rqkv, mq, mk, mv, gate = _inproj(
            x2, norm_g[l][None], win, wuq, wukv, mla_qa_g[l][None], mla_kva_g[l][None], tabs, seq, tm)
        yc = _conv(hglu, conv_dw_w[l], conv_dw_b[l][None], conv_ln_g[l][None], conv_ln_b[l][None], batch, seq)
        decay_tile = jnp.broadcast_to(ret_decay_logit[l][:, :, None, None], (2, RET_HEADS, 8, LANES))
        yr = _retention(rqkv, decay_tile, batch, seq)
        ym = _mla(mq, mk, mv, batch, seq, tq)
        x2 = _outproj(x2, yc, yr, ym, gate, w_out[l].astype(BF16), final_g[None], tm, final=(l == depth - 1))
    return x2.reshape(batch, seq, d_model)
```

```python
import functools
import math

import jax
import jax.numpy as jnp
from jax import lax
from jax.experimental import pallas as pl
from jax.experimental.pallas import tpu as pltpu

D_MODEL = 1024
D_MIX = 2 * D_MODEL
CONV_W = 512
CONV_K = 31
RET_W = 512
RET_HEADS = 4
RET_HD = 128
RET_CHUNK = 128
MLA_W = 1024
MLA_HEADS = 8
MLA_V_HD = 128
MLA_NOPE = 128
MLA_ROPE = 64
MLA_Q_RANK = 384
MLA_KV_RANK = 256
ROPE_BASE = 10000.0
EPS = 1e-6

OFF_RET = 2 * CONV_W
OFF_QLAT = OFF_RET + 3 * RET_W
OFF_KVLAT = OFF_QLAT + MLA_Q_RANK
OFF_KROPE = OFF_KVLAT + MLA_KV_RANK
OFF_GATE = OFF_KROPE + MLA_ROPE
N_IN = OFF_GATE + D_MIX

LANES = 128
SUBLANES = 8
MLA_QK_PAD = 2 * LANES
CONV_HALO = 16
VMEM_LIMIT = 56 * 1024 * 1024

P_KROPE = OFF_KROPE
P_GATE = P_KROPE + LANES
P_N_IN = P_GATE + D_MIX

BF16 = jnp.bfloat16
F32 = jnp.float32


def _rope_tables(seq, dim):
    inv = 1.0 / (ROPE_BASE ** (jnp.arange(0, dim, 2, dtype=F32) / dim))
    ang = jnp.arange(seq, dtype=F32)[:, None] * inv[None, :]
    ang = jnp.concatenate([ang, ang], axis=-1)
    return jnp.cos(ang), jnp.sin(ang)


def _rmsnorm(x, g):
    return x * lax.rsqrt(jnp.mean(x * x, axis=-1, keepdims=True) + EPS) * g


def _inproj_kernel(x_ref, g_ref, win_ref, wuq_ref, wukv_ref, qag_ref, kvag_ref,
                   cosr_ref, sinr_ref, cosm_ref, sinma_ref, sinmb_ref,
                   hglu_ref, rqkv_ref, mq_ref, mk_ref, mv_ref, gate_ref, *, sub):
    for r in range(x_ref.shape[0] // sub):
        _inproj_rows(slice(r * sub, (r + 1) * sub), x_ref, g_ref, win_ref, wuq_ref, wukv_ref, qag_ref, kvag_ref,
                     cosr_ref, sinr_ref, cosm_ref, sinma_ref, sinmb_ref,
                     hglu_ref, rqkv_ref, mq_ref, mk_ref, mv_ref, gate_ref)


def _inproj_rows(rows, x_ref, g_ref, win_ref, wuq_ref, wukv_ref, qag_ref, kvag_ref,
                 cosr_ref, sinr_ref, cosm_ref, sinma_ref, sinmb_ref,
                 hglu_ref, rqkv_ref, mq_ref, mk_ref, mv_ref, gate_ref):
    h = _rmsnorm(x_ref[rows, :], g_ref[...]).astype(BF16)

    def proj(lo, hi):
        return jnp.dot(h, win_ref[:, lo:hi], preferred_element_type=F32)

    a = proj(0, CONV_W)
    b = proj(CONV_W, 2 * CONV_W)
    hglu_ref[rows, :] = (a * jax.nn.sigmoid(b)).astype(BF16)

    cosr = cosr_ref[rows, :]
    sinr = sinr_ref[rows, :]
    k_scale = RET_HD ** -0.5
    for part in range(3):
        u = proj(OFF_RET + part * RET_W, OFF_RET + (part + 1) * RET_W)
        for hd in range(RET_HEADS):
            blk = u[:, hd * RET_HD:(hd + 1) * RET_HD]
            if part < 2:
                blk = blk * cosr + pltpu.roll(blk, RET_HD // 2, 1) * sinr
            if part == 1:
                blk = blk * k_scale
            col = part * RET_W + hd * RET_HD
            rqkv_ref[rows, col:col + RET_HD] = blk.astype(BF16)

    cosm = cosm_ref[rows, :]
    sinma = sinma_ref[rows, :]
    sinmb = sinmb_ref[rows, :]

    def rope64(blk):
        return (blk * cosm + pltpu.roll(blk, LANES - MLA_ROPE // 2, 1) * sinma
                + pltpu.roll(blk, MLA_ROPE // 2, 1) * sinmb)

    q_lat = proj(OFF_QLAT, OFF_KVLAT)
    qn = _rmsnorm(q_lat, qag_ref[...]).astype(BF16)
    q_scale = (MLA_NOPE + MLA_ROPE) ** -0.5 * math.log2(math.e)
    for hd in range(MLA_HEADS):
        qh = jnp.dot(qn, wuq_ref[:, hd * MLA_QK_PAD:(hd + 1) * MLA_QK_PAD], preferred_element_type=F32)
        mq_ref[rows, hd * MLA_QK_PAD:hd * MLA_QK_PAD + LANES] = (qh[:, :LANES] * q_scale).astype(BF16)
        mq_ref[rows, hd * MLA_QK_PAD + LANES:(hd + 1) * MLA_QK_PAD] = (rope64(qh[:, LANES:]) * q_scale).astype(BF16)

    kv_lat = proj(OFF_KVLAT, OFF_KROPE)
    kvn = _rmsnorm(kv_lat, kvag_ref[...]).astype(BF16)
    k_rope = rope64(proj(P_KROPE, P_GATE)).astype(BF16)
    v_off = MLA_HEADS * MLA_NOPE
    k_nope = jnp.dot(kvn, wukv_ref[:, :v_off], preferred_element_type=F32).astype(BF16)
    for hd in range(MLA_HEADS):
        mk_ref[rows, hd * MLA_QK_PAD:hd * MLA_QK_PAD + LANES] = k_nope[:, hd * MLA_NOPE:(hd + 1) * MLA_NOPE]
        mk_ref[rows, hd * MLA_QK_PAD + LANES:(hd + 1) * MLA_QK_PAD] = k_rope
    mv_ref[rows, :] = jnp.dot(kvn, wukv_ref[:, v_off:v_off + MLA_W], preferred_element_type=F32).astype(BF16)

    for c in range(D_MIX // CONV_W):
        gt = proj(P_GATE + c * CONV_W, P_GATE + (c + 1) * CONV_W)
        gate_ref[rows, c * CONV_W:(c + 1) * CONV_W] = (gt * jax.nn.sigmoid(gt)).astype(BF16)


def _resident(shape):
    return pl.BlockSpec(shape, lambda *_: (0,) * len(shape), pipeline_mode=pl.Buffered(1))


def _inproj(x2, g, win, wuq, wukv, qag, kvag, tabs, seq, tm, sub):
    tokens = x2.shape[0]
    nseq = seq // tm
    cosr, sinr, cosm, sinma, sinmb = tabs
    row = lambda i: (i, 0)
    pos = lambda i: (i % nseq, 0)
    return pl.pallas_call(
        functools.partial(_inproj_kernel, sub=sub),
        grid=(tokens // tm,),
        in_specs=[
            pl.BlockSpec((tm, D_MODEL), row),
            _resident((1, D_MODEL)),
            _resident((D_MODEL, P_N_IN)),
            _resident((MLA_Q_RANK, MLA_HEADS * MLA_QK_PAD)),
            _resident((MLA_KV_RANK, 2 * MLA_W)),
            _resident((1, MLA_Q_RANK)),
            _resident((1, MLA_KV_RANK)),
            pl.BlockSpec((tm, LANES), pos),
            pl.BlockSpec((tm, LANES), pos),
            pl.BlockSpec((tm, LANES), pos),
            pl.BlockSpec((tm, LANES), pos),
            pl.BlockSpec((tm, LANES), pos),
        ],
        out_specs=[
            pl.BlockSpec((tm, CONV_W), row),
            pl.BlockSpec((tm, 3 * RET_W), row),
            pl.BlockSpec((tm, MLA_HEADS * MLA_QK_PAD), row),
            pl.BlockSpec((tm, MLA_HEADS * MLA_QK_PAD), row),
            pl.BlockSpec((tm, MLA_W), row),
            pl.BlockSpec((tm, D_MIX), row),
        ],
        out_shape=[
            jax.ShapeDtypeStruct((tokens, CONV_W), BF16),
            jax.ShapeDtypeStruct((tokens, 3 * RET_W), BF16),
            jax.ShapeDtypeStruct((tokens, MLA_HEADS * MLA_QK_PAD), BF16),
            jax.ShapeDtypeStruct((tokens, MLA_HEADS * MLA_QK_PAD), BF16),
            jax.ShapeDtypeStruct((tokens, MLA_W), BF16),
            jax.ShapeDtypeStruct((tokens, D_MIX), BF16),
        ],
        compiler_params=pltpu.CompilerParams(
            dimension_semantics=("arbitrary",), vmem_limit_bytes=VMEM_LIMIT),
        name="inproj",
    )(x2, g, win, wuq, wukv, qag, kvag, cosr, sinr, cosm, sinma, sinmb)


def _conv_kernel(h_ref, w_ref, b_ref, lg_ref, lb_ref, o_ref, pad_ref, shift_ref, sh_ref, *, seq, rows):
    span = rows + 2 * CONV_HALO
    zeros = jnp.zeros((CONV_HALO, CONV_W), BF16)
    pad_ref[0:CONV_HALO, :] = zeros
    pad_ref[CONV_HALO + seq:, :] = zeros
    pad_ref[CONV_HALO:CONV_HALO + seq, :] = h_ref[...]
    ri = lax.broadcasted_iota(jnp.int32, (span, span), 0)
    ci = lax.broadcasted_iota(jnp.int32, (span, span), 1)
    for r in range(SUBLANES):
        shift_ref[r * span:(r + 1) * span, :] = jnp.where(ci == ri + r, 1.0, 0.0).astype(BF16)
    first_tap = CONV_HALO - CONV_K // 2

    n_tiles = seq // rows

    def shift_copies(t, dst_ref):
        r0 = pl.multiple_of(t * rows, rows)
        dst_ref[...] = jnp.dot(shift_ref[...], pad_ref[pl.ds(r0, span), :], preferred_element_type=F32)

    def taps(t, sh_ref):
        r0 = pl.multiple_of(t * rows, rows)
        accs = []
        for c in range(CONV_W // LANES):
            lanes = slice(c * LANES, (c + 1) * LANES)
            bias = jnp.broadcast_to(b_ref[:, lanes], (SUBLANES, LANES))
            acc = [bias] * (rows // SUBLANES)
            for k in range(CONV_K):
                r = (first_tap + k) % SUBLANES
                base = r * span + first_tap + k - r
                wk = jnp.broadcast_to(w_ref[k:k + 1, lanes], (SUBLANES, LANES))
                acc = [a + sh_ref[base + SUBLANES * j:base + SUBLANES * (j + 1), lanes] * wk
                       for j, a in enumerate(acc)]
            accs.append(jnp.concatenate(acc, axis=0))
        acc = jnp.concatenate(accs, axis=1)
        mu = jnp.mean(acc, axis=-1, keepdims=True)
        cen = acc - mu
        var = jnp.mean(cen * cen, axis=-1, keepdims=True)
        hn = cen * lax.rsqrt(var + EPS) * lg_ref[...] + lb_ref[...]
        o_ref[pl.ds(r0, rows), :] = (hn * jax.nn.sigmoid(hn)).astype(o_ref.dtype)

    sh = [sh_ref.at[i] for i in range(4)]
    shift_copies(0, sh[0])
    shift_copies(1, sh[1])

    def four_tiles(i, carry):
        t = 4 * i
        shift_copies(t + 2, sh[2])
        shift_copies(t + 3, sh[3])
        taps(t, sh[0])
        taps(t + 1, sh[1])
        shift_copies(jnp.minimum(t + 4, n_tiles - 2), sh[0])
        shift_copies(jnp.minimum(t + 5, n_tiles - 1), sh[1])
        taps(t + 2, sh[2])
        taps(t + 3, sh[3])
        return carry

    lax.fori_loop(0, n_tiles // 4, four_tiles, 0)


def _conv(hglu, w, b, lg, lb, batch, seq):
    rows = 64
    span = rows + 2 * CONV_HALO
    const = lambda i: (0, 0)
    return pl.pallas_call(
        functools.partial(_conv_kernel, seq=seq, rows=rows),
        grid=(batch,),
        in_specs=[
            pl.BlockSpec((seq, CONV_W), lambda i: (i, 0)),
            pl.BlockSpec((CONV_K, CONV_W), const),
            pl.BlockSpec((1, CONV_W), const),
            pl.BlockSpec((1, CONV_W), const),
            pl.BlockSpec((1, CONV_W), const),
        ],
        out_specs=pl.BlockSpec((seq, CONV_W), lambda i: (i, 0)),
        out_shape=jax.ShapeDtypeStruct(hglu.shape, BF16),
        scratch_shapes=[
            pltpu.VMEM((seq + 2 * CONV_HALO, CONV_W), BF16),
            pltpu.VMEM((SUBLANES * span, span), BF16),
            pltpu.VMEM((4, SUBLANES * span, CONV_W), F32),
        ],
        compiler_params=pltpu.CompilerParams(dimension_semantics=("arbitrary",)),
        name="conv",
    )(hglu, w, b, lg, lb)


def _log_sigmoid(x):
    return jnp.minimum(x, 0.0) - jnp.log1p(jnp.exp(-jnp.abs(x)))


def _ret_kernel(dl_ref, q_ref, k_ref, v_ref, o_ref, decay_ref, sd_ref, kv_ref, st_ref, *, seq, chunk):
    C = chunk
    n_chunks = seq // C
    D = RET_HD
    lg_f = jnp.broadcast_to(_log_sigmoid(dl_ref[0, 0])[0:1, :], (C, LANES))
    lg_b = jnp.broadcast_to(_log_sigmoid(dl_ref[1, 0])[0:1, :], (C, LANES))
    ri = lax.broadcasted_iota(jnp.int32, (C, LANES), 0)
    idx = ri.astype(F32)
    for c in range(C // LANES):
        diff = (ri - (lax.broadcasted_iota(jnp.int32, (C, LANES), 1) + c * LANES)).astype(F32)
        decay_ref[:, c * LANES:(c + 1) * LANES] = jnp.where(
            diff >= 0.0, jnp.exp(lg_f * jnp.maximum(diff, 0.0)), jnp.exp(lg_b * jnp.maximum(-diff, 0.0)))
    qdec_f = jnp.exp(lg_f * (idx + 1.0))
    kdec_f = jnp.exp(lg_f * (C - 1.0 - idx))
    qdec_b = jnp.exp(lg_b * (C - idx))
    kdec_b = jnp.exp(lg_b * idx)
    cdec_f = jnp.exp(lg_f[:D] * float(C))
    cdec_b = jnp.exp(lg_b[:D] * float(C))
    trans_b = (((1,), (1,)), ((), ()))
    trans_a = (((0,), (0,)), ((), ()))

    def rows(n):
        return pl.ds(n * C, C)

    for n in range(n_chunks):
        qn, kn, vn = q_ref[rows(n), :], k_ref[rows(n), :], v_ref[rows(n), :]
        s = lax.dot_general(qn, kn, trans_b, preferred_element_type=F32)
        sd_ref[n] = (s * decay_ref[...]).astype(BF16)
        knf = kn.astype(F32)
        kv_ref[0, n] = lax.dot_general((knf * kdec_f).astype(BF16), vn, trans_a, preferred_element_type=F32)
        kv_ref[1, n] = lax.dot_general((knf * kdec_b).astype(BF16), vn, trans_a, preferred_element_type=F32)

    state = jnp.zeros((D, D), F32)
    for n in range(n_chunks):
        st_ref[0, n] = state.astype(BF16)
        state = cdec_f * state + kv_ref[0, n]
    state = jnp.zeros((D, D), F32)
    for n in reversed(range(n_chunks)):
        st_ref[1, n] = state.astype(BF16)
        state = cdec_b * state + kv_ref[1, n]

    for n in range(n_chunks):
        qnf = q_ref[rows(n), :].astype(F32)
        lhs = jnp.concatenate([sd_ref[n], (qnf * qdec_f).astype(BF16), (qnf * qdec_b).astype(BF16)], axis=1)
        rhs = jnp.concatenate([v_ref[rows(n), :], st_ref[0, n], st_ref[1, n]], axis=0)
        out = jnp.dot(lhs, rhs, preferred_element_type=F32)
        mu = jnp.mean(out, axis=-1, keepdims=True)
        cen = out - mu
        var = jnp.mean(cen * cen, axis=-1, keepdims=True)
        o_ref[rows(n), :] = (cen * lax.rsqrt(var + EPS)).astype(o_ref.dtype)


def _retention(rqkv, decay_tile, batch, seq):
    chunk = 256 if seq % 256 == 0 else RET_CHUNK
    n_chunks = seq // chunk
    blk = lambda part: pl.BlockSpec((seq, RET_HD), lambda b, h: (b, part * RET_HEADS + h))
    return pl.pallas_call(
        functools.partial(_ret_kernel, seq=seq, chunk=chunk),
        grid=(batch, RET_HEADS),
        in_specs=[
            pl.BlockSpec((2, 1, 8, LANES), lambda b, h: (0, h, 0, 0)),
            blk(0), blk(1), blk(2),
        ],
        out_specs=pl.BlockSpec((seq, RET_HD), lambda b, h: (b, h)),
        out_shape=jax.ShapeDtypeStruct((batch * seq, RET_W), BF16),
        scratch_shapes=[
            pltpu.VMEM((chunk, chunk), F32),
            pltpu.VMEM((n_chunks, chunk, chunk), BF16),
            pltpu.VMEM((2, n_chunks, RET_HD, RET_HD), F32),
            pltpu.VMEM((2, n_chunks, RET_HD, RET_HD), BF16),
        ],
        compiler_params=pltpu.CompilerParams(dimension_semantics=("arbitrary", "arbitrary")),
        name="retention",
    )(decay_tile, rqkv, rqkv, rqkv)


def _mla_kernel(q_ref, k_ref, v_ref, o_ref, *, sub):
    k = k_ref[...]
    v = v_ref[...]
    for r in range(q_ref.shape[0] // sub):
        rows = slice(r * sub, (r + 1) * sub)
        s = lax.dot_general(q_ref[rows, :], k, (((1,), (1,)), ((), ())), preferred_element_type=F32)
        m = jnp.max(s, axis=-1, keepdims=True)
        p = jnp.exp2(s - m)
        l = jnp.sum(p, axis=-1, keepdims=True)
        o = jnp.dot(p.astype(BF16), v, preferred_element_type=F32)
        o_ref[rows, :] = (o / l).astype(o_ref.dtype)


def _mla(mq, mk, mv, batch, seq, tq, sub):
    nq = seq // tq
    return pl.pallas_call(
        functools.partial(_mla_kernel, sub=sub),
        grid=(batch, MLA_HEADS, nq),
        in_specs=[
            pl.BlockSpec((tq, MLA_QK_PAD), lambda b, h, i: (b * nq + i, h)),
            pl.BlockSpec((seq, MLA_QK_PAD), lambda b, h, i: (b, h)),
            pl.BlockSpec((seq, MLA_V_HD), lambda b, h, i: (b, h)),
        ],
        out_specs=pl.BlockSpec((tq, MLA_V_HD), lambda b, h, i: (b * nq + i, h)),
        out_shape=jax.ShapeDtypeStruct((batch * seq, MLA_W), BF16),
        compiler_params=pltpu.CompilerParams(
            dimension_semantics=("arbitrary", "arbitrary", "arbitrary"), vmem_limit_bytes=VMEM_LIMIT),
        name="mla_attention",
    )(mq, mk, mv)


def _outproj_kernel(x_ref, yc_ref, yr_ref, ym_ref, gate_ref, w_ref, fg_ref, o_ref, *, final):
    def gated(y_ref, lo, hi):
        return (y_ref[...].astype(F32) * gate_ref[:, lo:hi].astype(F32)).astype(BF16)

    acc = x_ref[...]
    acc = acc + jnp.dot(gated(yc_ref, 0, CONV_W), w_ref[0:CONV_W, :], preferred_element_type=F32)
    acc = acc + jnp.dot(gated(yr_ref, CONV_W, CONV_W + RET_W), w_ref[CONV_W:CONV_W + RET_W, :],
                        preferred_element_type=F32)
    acc = acc + jnp.dot(gated(ym_ref, CONV_W + RET_W, D_MIX), w_ref[CONV_W + RET_W:, :],
                        preferred_element_type=F32)
    if final:
        acc = _rmsnorm(acc, fg_ref[...])
    o_ref[...] = acc


def _outproj(x2, yc, yr, ym, gate, w, fg, tm, final):
    tokens = x2.shape[0]
    row = lambda i: (i, 0)
    const = lambda i: (0, 0)
    return pl.pallas_call(
        functools.partial(_outproj_kernel, final=final),
        grid=(tokens // tm,),
        in_specs=[
            pl.BlockSpec((tm, D_MODEL), row),
            pl.BlockSpec((tm, CONV_W), row),
            pl.BlockSpec((tm, RET_W), row),
            pl.BlockSpec((tm, MLA_W), row),
            pl.BlockSpec((tm, D_MIX), row),
            pl.BlockSpec((D_MIX, D_MODEL), const),
            pl.BlockSpec((1, D_MODEL), const),
        ],
        out_specs=pl.BlockSpec((tm, D_MODEL), row),
        out_shape=jax.ShapeDtypeStruct(x2.shape, F32),
        compiler_params=pltpu.CompilerParams(
            dimension_semantics=("arbitrary",), vmem_limit_bytes=VMEM_LIMIT),
        name="outproj",
    )(x2, yc, yr, ym, gate, w, fg)


def _pad_cols(w, width):
    return jnp.pad(w, ((0, 0), (0, width - w.shape[1])))


def _prep_layer_weights(w_in, w_uq, w_ukv):
    win = jnp.concatenate(
        [w_in[:, :OFF_KROPE], _pad_cols(w_in[:, OFF_KROPE:OFF_GATE], LANES), w_in[:, OFF_GATE:]], axis=1)
    uq = w_uq.reshape(MLA_Q_RANK, MLA_HEADS, MLA_NOPE + MLA_ROPE)
    uq = jnp.pad(uq, ((0, 0), (0, 0), (0, MLA_QK_PAD - MLA_NOPE - MLA_ROPE)))
    uq = uq.reshape(MLA_Q_RANK, MLA_HEADS * MLA_QK_PAD)
    ukv = w_ukv.reshape(MLA_KV_RANK, MLA_HEADS, MLA_NOPE + MLA_V_HD)
    ukv = jnp.concatenate(
        [ukv[:, :, :MLA_NOPE].reshape(MLA_KV_RANK, MLA_HEADS * MLA_NOPE),
         ukv[:, :, MLA_NOPE:].reshape(MLA_KV_RANK, MLA_W)], axis=1)
    return win.astype(BF16), uq.astype(BF16), ukv.astype(BF16)


def _rope_inputs(seq):
    cos_r, sin_r = _rope_tables(seq, RET_HD)
    half = RET_HD // 2
    sin_r = jnp.concatenate([-sin_r[:, :half], sin_r[:, half:]], axis=1)
    cos_m, sin_m = _rope_tables(seq, MLA_ROPE)
    half = MLA_ROPE // 2
    zeros = jnp.zeros((seq, half), F32)
    cosm = _pad_cols(cos_m, LANES)
    sinma = _pad_cols(jnp.concatenate([-sin_m[:, :half], zeros], axis=1), LANES)
    sinmb = _pad_cols(jnp.concatenate([zeros, sin_m[:, half:]], axis=1), LANES)
    return cos_r, sin_r, cosm, sinma, sinmb


def kernel(x, norm_g, w_in, conv_dw_w, conv_dw_b, conv_ln_g, conv_ln_b, ret_decay_logit,
           mla_qa_g, mla_w_uq, mla_kva_g, mla_w_ukv, w_out, final_g):
    batch, seq, d_model = x.shape
    depth = norm_g.shape[0]
    assert d_model == D_MODEL and seq % RET_CHUNK == 0
    tm = min(512, seq)
    tq = min(1024, seq)
    sub = 256
    tabs = _rope_inputs(seq)
    x2 = x.reshape(batch * seq, d_model)
    for l in range(depth):
        win, wuq, wukv = _prep_layer_weights(w_in[l], mla_w_uq[l], mla_w_ukv[l])
        hglu, rqkv, mq, mk, mv, gate = _inproj(
            x2, norm_g[l][None], win, wuq, wukv, mla_qa_g[l][None], mla_kva_g[l][None], tabs, seq, tm, sub)
        yc = _conv(hglu, conv_dw_w[l], conv_dw_b[l][None], conv_ln_g[l][None], conv_ln_b[l][None], batch, seq)
        decay_tile = jnp.broadcast_to(ret_decay_logit[l][:, :, None, None], (2, RET_HEADS, 8, LANES))
        yr = _retention(rqkv, decay_tile, batch, seq)
        ym = _mla(mq, mk, mv, batch, seq, tq, sub)
        x2 = _outproj(x2, yc, yr, ym, gate, w_out[l].astype(BF16), final_g[None], tm, final=(l == depth - 1))
    return x2.reshape(batch, seq, d_model)
```

```python
import functools
import math

import jax
import jax.numpy as jnp
from jax import lax
from jax.experimental import pallas as pl
from jax.experimental.pallas import tpu as pltpu

D_MODEL = 1024
D_MIX = 2 * D_MODEL
CONV_W = 512
CONV_K = 31
RET_W = 512
RET_HEADS = 4
RET_HD = 128
RET_CHUNK = 128
MLA_W = 1024
MLA_HEADS = 8
MLA_V_HD = 128
MLA_NOPE = 128
MLA_ROPE = 64
MLA_Q_RANK = 384
MLA_KV_RANK = 256
ROPE_BASE = 10000.0
EPS = 1e-6

OFF_RET = 2 * CONV_W
OFF_QLAT = OFF_RET + 3 * RET_W
OFF_KVLAT = OFF_QLAT + MLA_Q_RANK
OFF_KROPE = OFF_KVLAT + MLA_KV_RANK
OFF_GATE = OFF_KROPE + MLA_ROPE
N_IN = OFF_GATE + D_MIX

LANES = 128
SUBLANES = 8
MLA_QK_PAD = 2 * LANES
CONV_HALO = 16
VMEM_LIMIT = 56 * 1024 * 1024

BF16 = jnp.bfloat16
F32 = jnp.float32


def _tiles(seq):
    return min(512, seq), 256, min(2048, seq), 64


def _rope_tables(seq, dim):
    inv = 1.0 / (ROPE_BASE ** (jnp.arange(0, dim, 2, dtype=F32) / dim))
    ang = jnp.arange(seq, dtype=F32)[:, None] * inv[None, :]
    ang = jnp.concatenate([ang, ang], axis=-1)
    return jnp.cos(ang), jnp.sin(ang)


def _rmsnorm(x, g):
    return x * lax.rsqrt(jnp.mean(x * x, axis=-1, keepdims=True) + EPS) * g


def _layer_block(layer, shape):
    return pl.BlockSpec((None,) + shape, lambda *_: (layer,) + (0,) * len(shape), pipeline_mode=pl.Buffered(1))


def _inproj_kernel(x_ref, g_ref, wmain_ref, wkr_ref, wgate_ref, wuq_ref, wukt_ref, wuv_ref, qag_ref, kvag_ref,
                   cosr_ref, sinr_ref, cosm_ref, sinma_ref, sinmb_ref,
                   hglu_ref, rqkv_ref, mq_ref, mkt_ref, mv_ref, gate_ref, *, sub):
    for r in range(x_ref.shape[0] // sub):
        _inproj_rows(slice(r * sub, (r + 1) * sub), x_ref, g_ref, wmain_ref, wkr_ref, wgate_ref,
                     wuq_ref, wukt_ref, wuv_ref, qag_ref, kvag_ref,
                     cosr_ref, sinr_ref, cosm_ref, sinma_ref, sinmb_ref,
                     hglu_ref, rqkv_ref, mq_ref, mkt_ref, mv_ref, gate_ref)


def _inproj_rows(rows, x_ref, g_ref, wmain_ref, wkr_ref, wgate_ref, wuq_ref, wukt_ref, wuv_ref, qag_ref, kvag_ref,
                 cosr_ref, sinr_ref, cosm_ref, sinma_ref, sinmb_ref,
                 hglu_ref, rqkv_ref, mq_ref, mkt_ref, mv_ref, gate_ref):
    h = _rmsnorm(x_ref[rows, :], g_ref[...]).astype(BF16)

    def proj(lo, hi):
        return jnp.dot(h, wmain_ref[:, lo:hi], preferred_element_type=F32)

    a = proj(0, CONV_W)
    b = proj(CONV_W, 2 * CONV_W)
    hglu_ref[rows, :] = (a * jax.nn.sigmoid(b)).astype(BF16)

    cosr = cosr_ref[rows, :]
    sinr = sinr_ref[rows, :]
    k_scale = RET_HD ** -0.5
    for part in range(3):
        u = proj(OFF_RET + part * RET_W, OFF_RET + (part + 1) * RET_W)
        for hd in range(RET_HEADS):
            blk = u[:, hd * RET_HD:(hd + 1) * RET_HD]
            if part < 2:
                blk = blk * cosr + pltpu.roll(blk, RET_HD // 2, 1) * sinr
            if part == 1:
                blk = blk * k_scale
            col = part * RET_W + hd * RET_HD
            rqkv_ref[rows, col:col + RET_HD] = blk.astype(BF16)

    cosm = cosm_ref[rows, :]
    sinma = sinma_ref[rows, :]
    sinmb = sinmb_ref[rows, :]

    def rope64(blk):
        return (blk * cosm + pltpu.roll(blk, LANES - MLA_ROPE // 2, 1) * sinma
                + pltpu.roll(blk, MLA_ROPE // 2, 1) * sinmb)

    q_lat = proj(OFF_QLAT, OFF_KVLAT)
    qn = _rmsnorm(q_lat, qag_ref[...]).astype(BF16)
    q_scale = (MLA_NOPE + MLA_ROPE) ** -0.5 * math.log2(math.e)
    for hd in range(MLA_HEADS):
        qh = jnp.dot(qn, wuq_ref[:, hd * MLA_QK_PAD:(hd + 1) * MLA_QK_PAD], preferred_element_type=F32)
        mq_ref[rows, hd * MLA_QK_PAD:hd * MLA_QK_PAD + LANES] = (qh[:, :LANES] * q_scale).astype(BF16)
        mq_ref[rows, hd * MLA_QK_PAD + LANES:(hd + 1) * MLA_QK_PAD] = (rope64(qh[:, LANES:]) * q_scale).astype(BF16)

    kv_lat = proj(OFF_KVLAT, OFF_KROPE)
    kvn = _rmsnorm(kv_lat, kvag_ref[...]).astype(BF16)
    k_rope = jnp.dot(h, wkr_ref[...], preferred_element_type=F32)
    k_rope_t = rope64(k_rope).T.astype(BF16)
    k_nope_t = lax.dot_general(wukt_ref[...], kvn, (((1,), (1,)), ((), ())),
                               preferred_element_type=F32).astype(BF16)
    for hd in range(MLA_HEADS):
        mkt_ref[hd * MLA_QK_PAD:hd * MLA_QK_PAD + LANES, rows] = k_nope_t[hd * MLA_NOPE:(hd + 1) * MLA_NOPE, :]
        mkt_ref[hd * MLA_QK_PAD + LANES:(hd + 1) * MLA_QK_PAD, rows] = k_rope_t
    mv_ref[rows, :] = jnp.dot(kvn, wuv_ref[...], preferred_element_type=F32).astype(BF16)

    for c in range(D_MIX // CONV_W):
        gt = jnp.dot(h, wgate_ref[:, c * CONV_W:(c + 1) * CONV_W], preferred_element_type=F32)
        gate_ref[rows, c * CONV_W:(c + 1) * CONV_W] = (gt * jax.nn.sigmoid(gt)).astype(BF16)


def _inproj(x2, p, layer, tabs, seq):
    tm, sub, _, _ = _tiles(seq)
    tokens = x2.shape[0]
    nseq = seq // tm
    row = lambda i: (i, 0)
    pos = lambda i: (i % nseq, 0)
    return pl.pallas_call(
        functools.partial(_inproj_kernel, sub=sub),
        grid=(tokens // tm,),
        in_specs=[
            pl.BlockSpec((tm, D_MODEL), row),
            _layer_block(layer, (1, D_MODEL)),
            _layer_block(layer, (D_MODEL, OFF_KROPE)),
            _layer_block(layer, (D_MODEL, LANES)),
            _layer_block(layer, (D_MODEL, D_MIX)),
            _layer_block(layer, (MLA_Q_RANK, MLA_HEADS * MLA_QK_PAD)),
            _layer_block(layer, (MLA_HEADS * MLA_NOPE, MLA_KV_RANK)),
            _layer_block(layer, (MLA_KV_RANK, MLA_W)),
            _layer_block(layer, (1, MLA_Q_RANK)),
            _layer_block(layer, (1, MLA_KV_RANK)),
        ] + [pl.BlockSpec((tm, LANES), pos)] * len(tabs),
        out_specs=[
            pl.BlockSpec((tm, CONV_W), row),
            pl.BlockSpec((tm, 3 * RET_W), row),
            pl.BlockSpec((tm, MLA_HEADS * MLA_QK_PAD), row),
            pl.BlockSpec((MLA_HEADS * MLA_QK_PAD, tm), lambda i: (0, i)),
            pl.BlockSpec((tm, MLA_W), row),
            pl.BlockSpec((tm, D_MIX), row),
        ],
        out_shape=[
            jax.ShapeDtypeStruct((tokens, CONV_W), BF16),
            jax.ShapeDtypeStruct((tokens, 3 * RET_W), BF16),
            jax.ShapeDtypeStruct((tokens, MLA_HEADS * MLA_QK_PAD), BF16),
            jax.ShapeDtypeStruct((MLA_HEADS * MLA_QK_PAD, tokens), BF16),
            jax.ShapeDtypeStruct((tokens, MLA_W), BF16),
            jax.ShapeDtypeStruct((tokens, D_MIX), BF16),
        ],
        compiler_params=pltpu.CompilerParams(
            dimension_semantics=("arbitrary",), vmem_limit_bytes=VMEM_LIMIT),
        name="inproj",
    )(x2, p["norm_g"], p["w_main"], p["w_kr"], p["w_gate"], p["w_uq"], p["w_ukt"], p["w_uv"],
      p["qa_g"], p["kva_g"], *tabs)


def _conv_kernel(h_ref, w_ref, b_ref, lg_ref, lb_ref, o_ref, pad_ref, shift_ref, sh_ref, *, seq, rows):
    span = rows + 2 * CONV_HALO
    zeros = jnp.zeros((CONV_HALO, CONV_W), BF16)
    pad_ref[0:CONV_HALO, :] = zeros
    pad_ref[CONV_HALO + seq:, :] = zeros
    pad_ref[CONV_HALO:CONV_HALO + seq, :] = h_ref[...]
    ri = lax.broadcasted_iota(jnp.int32, (span, span), 0)
    ci = lax.broadcasted_iota(jnp.int32, (span, span), 1)
    for r in range(SUBLANES):
        shift_ref[r * span:(r + 1) * span, :] = jnp.where(ci == ri + r, 1.0, 0.0).astype(BF16)
    first_tap = CONV_HALO - CONV_K // 2
    n_tiles = seq // rows

    def shift_copies(t, dst_ref):
        r0 = pl.multiple_of(t * rows, rows)
        dst_ref[...] = jnp.dot(shift_ref[...], pad_ref[pl.ds(r0, span), :], preferred_element_type=F32)

    def taps(t, src_ref):
        r0 = pl.multiple_of(t * rows, rows)
        accs = []
        for c in range(CONV_W // LANES):
            lanes = slice(c * LANES, (c + 1) * LANES)
            bias = jnp.broadcast_to(b_ref[:, lanes], (SUBLANES, LANES))
            acc = [bias] * (rows // SUBLANES)
            for k in range(CONV_K):
                r = (first_tap + k) % SUBLANES
                base = r * span + first_tap + k - r
                wk = jnp.broadcast_to(w_ref[k:k + 1, lanes], (SUBLANES, LANES))
                acc = [a + src_ref[base + SUBLANES * j:base + SUBLANES * (j + 1), lanes] * wk
                       for j, a in enumerate(acc)]
            accs.append(jnp.concatenate(acc, axis=0))
        acc = jnp.concatenate(accs, axis=1)
        mu = jnp.mean(acc, axis=-1, keepdims=True)
        cen = acc - mu
        var = jnp.mean(cen * cen, axis=-1, keepdims=True)
        hn = cen * lax.rsqrt(var + EPS) * lg_ref[...] + lb_ref[...]
        o_ref[pl.ds(r0, rows), :] = (hn * jax.nn.sigmoid(hn)).astype(o_ref.dtype)

    sh = [sh_ref.at[i] for i in range(4)]
    shift_copies(0, sh[0])
    shift_copies(1, sh[1])

    def four_tiles(i, carry):
        t = 4 * i
        shift_copies(t + 2, sh[2])
        shift_copies(t + 3, sh[3])
        taps(t, sh[0])
        taps(t + 1, sh[1])
        shift_copies(jnp.minimum(t + 4, n_tiles - 2), sh[0])
        shift_copies(jnp.minimum(t + 5, n_tiles - 1), sh[1])
        taps(t + 2, sh[2])
        taps(t + 3, sh[3])
        return carry

    lax.fori_loop(0, n_tiles // 4, four_tiles, 0)


def _conv(hglu, p, layer, batch, seq):
    _, _, _, rows = _tiles(seq)
    assert seq % (4 * rows) == 0
    span = rows + 2 * CONV_HALO
    return pl.pallas_call(
        functools.partial(_conv_kernel, seq=seq, rows=rows),
        grid=(batch,),
        in_specs=[
            pl.BlockSpec((seq, CONV_W), lambda i: (i, 0)),
            _layer_block(layer, (CONV_K, CONV_W)),
            _layer_block(layer, (1, CONV_W)),
            _layer_block(layer, (1, CONV_W)),
            _layer_block(layer, (1, CONV_W)),
        ],
        out_specs=pl.BlockSpec((seq, CONV_W), lambda i: (i, 0)),
        out_shape=jax.ShapeDtypeStruct(hglu.shape, BF16),
        scratch_shapes=[
            pltpu.VMEM((seq + 2 * CONV_HALO, CONV_W), BF16),
            pltpu.VMEM((SUBLANES * span, span), BF16),
            pltpu.VMEM((4, SUBLANES * span, CONV_W), F32),
        ],
        compiler_params=pltpu.CompilerParams(dimension_semantics=("arbitrary",)),
        name="conv",
    )(hglu, p["conv_w"], p["conv_b"], p["conv_ln_g"], p["conv_ln_b"])


def _log_sigmoid(x):
    return jnp.minimum(x, 0.0) - jnp.log1p(jnp.exp(-jnp.abs(x)))


def _ret_kernel(dl_ref, q_ref, k_ref, v_ref, o_ref, decay_ref, sd_ref, kv_ref, st_ref, *, seq, chunk):
    C = chunk
    n_chunks = seq // C
    D = RET_HD
    lg_f = jnp.broadcast_to(_log_sigmoid(dl_ref[0, 0])[0:1, :], (C, LANES))
    lg_b = jnp.broadcast_to(_log_sigmoid(dl_ref[1, 0])[0:1, :], (C, LANES))
    ri = lax.broadcasted_iota(jnp.int32, (C, LANES), 0)
    idx = ri.astype(F32)
    for c in range(C // LANES):
        diff = (ri - (lax.broadcasted_iota(jnp.int32, (C, LANES), 1) + c * LANES)).astype(F32)
        decay_ref[:, c * LANES:(c + 1) * LANES] = jnp.where(
            diff >= 0.0, jnp.exp(lg_f * jnp.maximum(diff, 0.0)), jnp.exp(lg_b * jnp.maximum(-diff, 0.0)))
    qdec_f = jnp.exp(lg_f * (idx + 1.0))
    kdec_f = jnp.exp(lg_f * (C - 1.0 - idx))
    qdec_b = jnp.exp(lg_b * (C - idx))
    kdec_b = jnp.exp(lg_b * idx)
    cdec_f = jnp.exp(lg_f[:D] * float(C))
    cdec_b = jnp.exp(lg_b[:D] * float(C))
    trans_b = (((1,), (1,)), ((), ()))
    trans_a = (((0,), (0,)), ((), ()))

    def rows(n):
        return pl.ds(n * C, C)

    for n in range(n_chunks):
        qn, kn, vn = q_ref[rows(n), :], k_ref[rows(n), :], v_ref[rows(n), :]
        s = lax.dot_general(qn, kn, trans_b, preferred_element_type=F32)
        sd_ref[n] = (s * decay_ref[...]).astype(BF16)
        knf = kn.astype(F32)
        kv_ref[0, n] = lax.dot_general((knf * kdec_f).astype(BF16), vn, trans_a, preferred_element_type=F32)
        kv_ref[1, n] = lax.dot_general((knf * kdec_b).astype(BF16), vn, trans_a, preferred_element_type=F32)

    state = jnp.zeros((D, D), F32)
    for n in range(n_chunks):
        st_ref[0, n] = state.astype(BF16)
        state = cdec_f * state + kv_ref[0, n]
    state = jnp.zeros((D, D), F32)
    for n in reversed(range(n_chunks)):
        st_ref[1, n] = state.astype(BF16)
        state = cdec_b * state + kv_ref[1, n]

    for n in range(n_chunks):
        qnf = q_ref[rows(n), :].astype(F32)
        lhs = jnp.concatenate([sd_ref[n], (qnf * qdec_f).astype(BF16), (qnf * qdec_b).astype(BF16)], axis=1)
        rhs = jnp.concatenate([v_ref[rows(n), :], st_ref[0, n], st_ref[1, n]], axis=0)
        out = jnp.dot(lhs, rhs, preferred_element_type=F32)
        mu = jnp.mean(out, axis=-1, keepdims=True)
        cen = out - mu
        var = jnp.mean(cen * cen, axis=-1, keepdims=True)
        o_ref[rows(n), :] = (cen * lax.rsqrt(var + EPS)).astype(o_ref.dtype)


def _retention(rqkv, p, layer, batch, seq):
    chunk = 256 if seq % 256 == 0 else RET_CHUNK
    n_chunks = seq // chunk
    blk = lambda part: pl.BlockSpec((seq, RET_HD), lambda b, h: (b, part * RET_HEADS + h))
    return pl.pallas_call(
        functools.partial(_ret_kernel, seq=seq, chunk=chunk),
        grid=(batch, RET_HEADS),
        in_specs=[
            pl.BlockSpec((None, 2, 1, SUBLANES, LANES), lambda b, h: (layer, 0, h, 0, 0)),
            blk(0), blk(1), blk(2),
        ],
        out_specs=pl.BlockSpec((seq, RET_HD), lambda b, h: (b, h)),
        out_shape=jax.ShapeDtypeStruct((batch * seq, RET_W), BF16),
        scratch_shapes=[
            pltpu.VMEM((chunk, chunk), F32),
            pltpu.VMEM((n_chunks, chunk, chunk), BF16),
            pltpu.VMEM((2, n_chunks, RET_HD, RET_HD), F32),
            pltpu.VMEM((2, n_chunks, RET_HD, RET_HD), BF16),
        ],
        compiler_params=pltpu.CompilerParams(dimension_semantics=("arbitrary", "arbitrary")),
        name="retention",
    )(p["ret_decay"], rqkv, rqkv, rqkv)


def _mla_kernel(q_ref, kt_ref, v_ref, o_ref, *, sub):
    kt = kt_ref[...]
    v = v_ref[...]
    n_sub = q_ref.shape[0] // sub

    def scores(r):
        return jnp.dot(q_ref[r * sub:(r + 1) * sub, :], kt, preferred_element_type=F32)

    s_next = scores(0)
    for r in range(n_sub):
        rows = slice(r * sub, (r + 1) * sub)
        s = s_next
        if r + 1 < n_sub:
            s_next = scores(r + 1)
        m = jnp.max(s, axis=-1, keepdims=True)
        p = jnp.exp2(s - m)
        l = jnp.sum(p, axis=-1, keepdims=True)
        o = jnp.dot(p.astype(BF16), v, preferred_element_type=F32)
        o_ref[rows, :] = (o / l).astype(o_ref.dtype)


def _mla(mq, mkt, mv, batch, seq):
    _, sub, tq, _ = _tiles(seq)
    nq = seq // tq
    return pl.pallas_call(
        functools.partial(_mla_kernel, sub=sub),
        grid=(batch, MLA_HEADS, nq),
        in_specs=[
            pl.BlockSpec((tq, MLA_QK_PAD), lambda b, h, i: (b * nq + i, h)),
            pl.BlockSpec((MLA_QK_PAD, seq), lambda b, h, i: (h, b)),
            pl.BlockSpec((seq, MLA_V_HD), lambda b, h, i: (b, h)),
        ],
        out_specs=pl.BlockSpec((tq, MLA_V_HD), lambda b, h, i: (b * nq + i, h)),
        out_shape=jax.ShapeDtypeStruct((batch * seq, MLA_W), BF16),
        compiler_params=pltpu.CompilerParams(
            dimension_semantics=("arbitrary", "arbitrary", "arbitrary"), vmem_limit_bytes=VMEM_LIMIT),
        name="mla_attention",
    )(mq, mkt, mv)


def _outproj_kernel(x_ref, yc_ref, yr_ref, ym_ref, gate_ref, w_ref, fg_ref, o_ref, *, final):
    def gated(y_ref, lo, hi):
        return (y_ref[...].astype(F32) * gate_ref[:, lo:hi].astype(F32)).astype(BF16)

    acc = x_ref[...]
    acc = acc + jnp.dot(gated(yc_ref, 0, CONV_W), w_ref[0:CONV_W, :], preferred_element_type=F32)
    acc = acc + jnp.dot(gated(yr_ref, CONV_W, CONV_W + RET_W), w_ref[CONV_W:CONV_W + RET_W, :],
                        preferred_element_type=F32)
    acc = acc + jnp.dot(gated(ym_ref, CONV_W + RET_W, D_MIX), w_ref[CONV_W + RET_W:, :],
                        preferred_element_type=F32)
    if final:
        acc = _rmsnorm(acc, fg_ref[...])
    o_ref[...] = acc


def _outproj(x2, yc, yr, ym, gate, p, layer, seq, final):
    tm, _, _, _ = _tiles(seq)
    tokens = x2.shape[0]
    row = lambda i: (i, 0)
    return pl.pallas_call(
        functools.partial(_outproj_kernel, final=final),
        grid=(tokens // tm,),
        in_specs=[
            pl.BlockSpec((tm, D_MODEL), row),
            pl.BlockSpec((tm, CONV_W), row),
            pl.BlockSpec((tm, RET_W), row),
            pl.BlockSpec((tm, MLA_W), row),
            pl.BlockSpec((tm, D_MIX), row),
            _layer_block(layer, (D_MIX, D_MODEL)),
            pl.BlockSpec((1, D_MODEL), lambda i: (0, 0)),
        ],
        out_specs=pl.BlockSpec((tm, D_MODEL), row),
        out_shape=jax.ShapeDtypeStruct(x2.shape, F32),
        compiler_params=pltpu.CompilerParams(
            dimension_semantics=("arbitrary",), vmem_limit_bytes=VMEM_LIMIT),
        name="outproj",
    )(x2, yc, yr, ym, gate, p["w_out"], p["final_g"])


def _pad_last(w, width):
    return jnp.pad(w, [(0, 0)] * (w.ndim - 1) + [(0, width - w.shape[-1])])


def _prep_params(norm_g, w_in, conv_dw_w, conv_dw_b, conv_ln_g, conv_ln_b, ret_decay_logit,
                 mla_qa_g, mla_w_uq, mla_kva_g, mla_w_ukv, w_out, final_g):
    depth = norm_g.shape[0]
    uq = mla_w_uq.reshape(depth, MLA_Q_RANK, MLA_HEADS, MLA_NOPE + MLA_ROPE)
    uq = _pad_last(uq, MLA_QK_PAD).reshape(depth, MLA_Q_RANK, MLA_HEADS * MLA_QK_PAD)
    ukv = mla_w_ukv.reshape(depth, MLA_KV_RANK, MLA_HEADS, MLA_NOPE + MLA_V_HD)
    ukt = ukv[..., :MLA_NOPE].reshape(depth, MLA_KV_RANK, MLA_HEADS * MLA_NOPE).transpose(0, 2, 1)
    uv = ukv[..., MLA_NOPE:].reshape(depth, MLA_KV_RANK, MLA_W)
    row = lambda a: a[:, None, :]
    return {
        "norm_g": row(norm_g),
        "w_main": w_in[:, :, :OFF_KROPE].astype(BF16),
        "w_kr": _pad_last(w_in[:, :, OFF_KROPE:OFF_GATE], LANES).astype(BF16),
        "w_gate": w_in[:, :, OFF_GATE:].astype(BF16),
        "w_uq": uq.astype(BF16),
        "w_ukt": ukt.astype(BF16),
        "w_uv": uv.astype(BF16),
        "qa_g": row(mla_qa_g),
        "kva_g": row(mla_kva_g),
        "conv_w": conv_dw_w,
        "conv_b": row(conv_dw_b),
        "conv_ln_g": row(conv_ln_g),
        "conv_ln_b": row(conv_ln_b),
        "ret_decay": jnp.broadcast_to(ret_decay_logit[:, :, :, None, None], (depth, 2, RET_HEADS, SUBLANES, LANES)),
        "w_out": w_out.astype(BF16),
        "final_g": final_g[None, :],
    }


def _rope_inputs(seq):
    cos_r, sin_r = _rope_tables(seq, RET_HD)
    half = RET_HD // 2
    sin_r = jnp.concatenate([-sin_r[:, :half], sin_r[:, half:]], axis=1)
    cos_m, sin_m = _rope_tables(seq, MLA_ROPE)
    half = MLA_ROPE // 2
    zeros = jnp.zeros((seq, half), F32)
    cosm = _pad_last(cos_m, LANES)
    sinma = _pad_last(jnp.concatenate([-sin_m[:, :half], zeros], axis=1), LANES)
    sinmb = _pad_last(jnp.concatenate([zeros, sin_m[:, half:]], axis=1), LANES)
    return cos_r, sin_r, cosm, sinma, sinmb


def kernel(x, norm_g, w_in, conv_dw_w, conv_dw_b, conv_ln_g, conv_ln_b, ret_decay_logit,
           mla_qa_g, mla_w_uq, mla_kva_g, mla_w_ukv, w_out, final_g):
    batch, seq, d_model = x.shape
    depth = norm_g.shape[0]
    assert d_model == D_MODEL and seq % RET_CHUNK == 0
    p = _prep_params(norm_g, w_in, conv_dw_w, conv_dw_b, conv_ln_g, conv_ln_b, ret_decay_logit,
                     mla_qa_g, mla_w_uq, mla_kva_g, mla_w_ukv, w_out, final_g)
    tabs = _rope_inputs(seq)
    x2 = x.reshape(batch * seq, d_model)
    for layer in range(depth):
        hglu, rqkv, mq, mkt, mv, gate = _inproj(x2, p, layer, tabs, seq)
        yc = _conv(hglu, p, layer, batch, seq)
        yr = _retention(rqkv, p, layer, batch, seq)
        ym = _mla(mq, mkt, mv, batch, seq)
        x2 = _outproj(x2, yc, yr, ym, gate, p, layer, seq, final=(layer == depth - 1))
    return x2.reshape(batch, seq, d_model)
```

```python
import functools
import math

import jax
import jax.numpy as jnp
from jax import lax
from jax.experimental import pallas as pl
from jax.experimental.pallas import tpu as pltpu

D_MODEL = 1024
D_MIX = 2 * D_MODEL
CONV_W = 512
CONV_K = 31
RET_W = 512
RET_HEADS = 4
RET_HD = 128
RET_CHUNK = 128
MLA_W = 1024
MLA_HEADS = 8
MLA_V_HD = 128
MLA_NOPE = 128
MLA_ROPE = 64
MLA_Q_RANK = 384
MLA_KV_RANK = 256
ROPE_BASE = 10000.0
EPS = 1e-6

OFF_RET = 2 * CONV_W
OFF_QLAT = OFF_RET + 3 * RET_W
OFF_KVLAT = OFF_QLAT + MLA_Q_RANK
OFF_KROPE = OFF_KVLAT + MLA_KV_RANK
OFF_GATE = OFF_KROPE + MLA_ROPE
N_IN = OFF_GATE + D_MIX
W_MAIN = OFF_QLAT + MLA_KV_RANK

LANES = 128
SUBLANES = 8
MLA_QK_PAD = 2 * LANES
CONV_HALO = 16
VMEM_LIMIT = 56 * 1024 * 1024

BF16 = jnp.bfloat16
F32 = jnp.float32


def _tiles(seq):
    return min(512, seq), 256, min(2048, seq), 64


def _rope_tables(seq, dim):
    inv = 1.0 / (ROPE_BASE ** (jnp.arange(0, dim, 2, dtype=F32) / dim))
    ang = jnp.arange(seq, dtype=F32)[:, None] * inv[None, :]
    ang = jnp.concatenate([ang, ang], axis=-1)
    return jnp.cos(ang), jnp.sin(ang)


def _rmsnorm(x, g):
    return x * lax.rsqrt(jnp.mean(x * x, axis=-1, keepdims=True) + EPS) * g


def _layer_block(layer, shape):
    return pl.BlockSpec((None,) + shape, lambda *_: (layer,) + (0,) * len(shape), pipeline_mode=pl.Buffered(1))


def _inproj_kernel(x_ref, g_ref, wmain_ref, wlat_ref, wgate_ref, wuq_ref, wukt_ref, wuv_ref, qag_ref, kvag_ref,
                   cosr_ref, sinr_ref, cosm_ref, sinma_ref, sinmb_ref,
                   hglu_ref, rqkv_ref, mq_ref, mkt_ref, mv_ref, gate_ref, *, sub):
    for r in range(x_ref.shape[0] // sub):
        _inproj_rows(slice(r * sub, (r + 1) * sub), x_ref, g_ref, wmain_ref, wlat_ref, wgate_ref,
                     wuq_ref, wukt_ref, wuv_ref, qag_ref, kvag_ref,
                     cosr_ref, sinr_ref, cosm_ref, sinma_ref, sinmb_ref,
                     hglu_ref, rqkv_ref, mq_ref, mkt_ref, mv_ref, gate_ref)


def _inproj_rows(rows, x_ref, g_ref, wmain_ref, wlat_ref, wgate_ref, wuq_ref, wukt_ref, wuv_ref, qag_ref, kvag_ref,
                 cosr_ref, sinr_ref, cosm_ref, sinma_ref, sinmb_ref,
                 hglu_ref, rqkv_ref, mq_ref, mkt_ref, mv_ref, gate_ref):
    h = _rmsnorm(x_ref[rows, :], g_ref[...]).astype(BF16)

    lat = jnp.dot(h, wlat_ref[...], preferred_element_type=F32)
    q_lat, k_rope = lat[:, :MLA_Q_RANK], lat[:, MLA_Q_RANK:]

    gt = jnp.dot(h, wgate_ref[...], preferred_element_type=F32)
    gate_ref[rows, :] = (gt * jax.nn.sigmoid(gt)).astype(BF16)

    u_main = jnp.dot(h, wmain_ref[...], preferred_element_type=F32)

    hglu_ref[rows, :] = (u_main[:, :CONV_W] * jax.nn.sigmoid(u_main[:, CONV_W:2 * CONV_W])).astype(BF16)

    cosr = cosr_ref[rows, :]
    sinr = sinr_ref[rows, :]
    k_scale = RET_HD ** -0.5
    for part in range(3):
        for hd in range(RET_HEADS):
            col = part * RET_W + hd * RET_HD
            blk = u_main[:, OFF_RET + col:OFF_RET + col + RET_HD]
            if part < 2:
                blk = blk * cosr + pltpu.roll(blk, RET_HD // 2, 1) * sinr
            if part == 1:
                blk = blk * k_scale
            rqkv_ref[rows, col:col + RET_HD] = blk.astype(BF16)

    cosm = cosm_ref[rows, :]
    sinma = sinma_ref[rows, :]
    sinmb = sinmb_ref[rows, :]

    def rope_pair(blk):
        return (blk * cosm + pltpu.roll(blk, LANES - MLA_ROPE // 2, 1) * sinma
                + pltpu.roll(blk, MLA_ROPE // 2, 1) * sinmb)

    qn = _rmsnorm(q_lat, qag_ref[...]).astype(BF16)
    q_scale = (MLA_NOPE + MLA_ROPE) ** -0.5 * math.log2(math.e)
    q_all = jnp.dot(qn, wuq_ref[...], preferred_element_type=F32) * q_scale
    rope_off = MLA_HEADS * MLA_NOPE
    lane = lax.broadcasted_iota(jnp.int32, (q_all.shape[0], LANES), 1)
    for pair in range(MLA_HEADS // 2):
        roped = rope_pair(q_all[:, rope_off + pair * LANES:rope_off + (pair + 1) * LANES])
        for j in range(2):
            hd = 2 * pair + j
            own = (lane < MLA_ROPE) if j == 0 else (lane >= MLA_ROPE)
            mq_ref[rows, hd * MLA_QK_PAD:hd * MLA_QK_PAD + LANES] = q_all[:, hd * MLA_NOPE:(hd + 1) * MLA_NOPE].astype(BF16)
            mq_ref[rows, hd * MLA_QK_PAD + LANES:(hd + 1) * MLA_QK_PAD] = jnp.where(own, roped, 0.0).astype(BF16)

    kvn = _rmsnorm(u_main[:, OFF_QLAT:W_MAIN], kvag_ref[...]).astype(BF16)
    k_rope_t = rope_pair(k_rope).T.astype(BF16)
    k_nope_t = lax.dot_general(wukt_ref[...], kvn, (((1,), (1,)), ((), ())),
                               preferred_element_type=F32).astype(BF16)
    for hd in range(MLA_HEADS):
        mkt_ref[hd * MLA_QK_PAD:hd * MLA_QK_PAD + LANES, rows] = k_nope_t[hd * MLA_NOPE:(hd + 1) * MLA_NOPE, :]
        mkt_ref[hd * MLA_QK_PAD + LANES:(hd + 1) * MLA_QK_PAD, rows] = k_rope_t
    mv_ref[rows, :] = jnp.dot(kvn, wuv_ref[...], preferred_element_type=F32).astype(BF16)


def _inproj(x2, p, layer, tabs, seq):
    tm, sub, _, _ = _tiles(seq)
    tokens = x2.shape[0]
    nseq = seq // tm
    row = lambda i: (i, 0)
    pos = lambda i: (i % nseq, 0)
    return pl.pallas_call(
        functools.partial(_inproj_kernel, sub=sub),
        grid=(tokens // tm,),
        in_specs=[
            pl.BlockSpec((tm, D_MODEL), row),
            _layer_block(layer, (1, D_MODEL)),
            _layer_block(layer, (D_MODEL, W_MAIN)),
            _layer_block(layer, (D_MODEL, MLA_Q_RANK + LANES)),
            _layer_block(layer, (D_MODEL, D_MIX)),
            _layer_block(layer, (MLA_Q_RANK, MLA_HEADS * (MLA_NOPE + MLA_ROPE))),
            _layer_block(layer, (MLA_HEADS * MLA_NOPE, MLA_KV_RANK)),
            _layer_block(layer, (MLA_KV_RANK, MLA_W)),
            _layer_block(layer, (1, MLA_Q_RANK)),
            _layer_block(layer, (1, MLA_KV_RANK)),
        ] + [pl.BlockSpec((tm, LANES), pos)] * len(tabs),
        out_specs=[
            pl.BlockSpec((tm, CONV_W), row),
            pl.BlockSpec((tm, 3 * RET_W), row),
            pl.BlockSpec((tm, MLA_HEADS * MLA_QK_PAD), row),
            pl.BlockSpec((MLA_HEADS * MLA_QK_PAD, tm), lambda i: (0, i)),
            pl.BlockSpec((tm, MLA_W), row),
            pl.BlockSpec((tm, D_MIX), row),
        ],
        out_shape=[
            jax.ShapeDtypeStruct((tokens, CONV_W), BF16),
            jax.ShapeDtypeStruct((tokens, 3 * RET_W), BF16),
            jax.ShapeDtypeStruct((tokens, MLA_HEADS * MLA_QK_PAD), BF16),
            jax.ShapeDtypeStruct((MLA_HEADS * MLA_QK_PAD, tokens), BF16),
            jax.ShapeDtypeStruct((tokens, MLA_W), BF16),
            jax.ShapeDtypeStruct((tokens, D_MIX), BF16),
        ],
        compiler_params=pltpu.CompilerParams(
            dimension_semantics=("arbitrary",), vmem_limit_bytes=VMEM_LIMIT),
        name="inproj",
    )(x2, p["norm_g"], p["w_main"], p["w_lat"], p["w_gate"], p["w_uq"], p["w_ukt"], p["w_uv"],
      p["qa_g"], p["kva_g"], *tabs)


def _conv_kernel(h_ref, w_ref, b_ref, lg_ref, lb_ref, o_ref, pad_ref, shift_ref, sh_ref, *, seq, rows):
    span = rows + 2 * CONV_HALO
    zeros = jnp.zeros((CONV_HALO, CONV_W), BF16)
    pad_ref[0:CONV_HALO, :] = zeros
    pad_ref[CONV_HALO + seq:, :] = zeros
    pad_ref[CONV_HALO:CONV_HALO + seq, :] = h_ref[...]
    ri = lax.broadcasted_iota(jnp.int32, (span, span), 0)
    ci = lax.broadcasted_iota(jnp.int32, (span, span), 1)
    for r in range(SUBLANES):
        shift_ref[r * span:(r + 1) * span, :] = jnp.where(ci == ri + r, 1.0, 0.0).astype(BF16)
    first_tap = CONV_HALO - CONV_K // 2
    n_tiles = seq // rows

    def shift_copies(t, dst_ref):
        r0 = pl.multiple_of(t * rows, rows)
        dst_ref[...] = jnp.dot(shift_ref[...], pad_ref[pl.ds(r0, span), :], preferred_element_type=F32)

    def taps(t, src_ref):
        r0 = pl.multiple_of(t * rows, rows)
        accs = []
        for c in range(CONV_W // LANES):
            lanes = slice(c * LANES, (c + 1) * LANES)
            bias = jnp.broadcast_to(b_ref[:, lanes], (SUBLANES, LANES))
            acc = [bias] * (rows // SUBLANES)
            for k in range(CONV_K):
                r = (first_tap + k) % SUBLANES
                base = r * span + first_tap + k - r
                wk = jnp.broadcast_to(w_ref[k:k + 1, lanes], (SUBLANES, LANES))
                acc = [a + src_ref[base + SUBLANES * j:base + SUBLANES * (j + 1), lanes] * wk
                       for j, a in enumerate(acc)]
            accs.append(jnp.concatenate(acc, axis=0))
        acc = jnp.concatenate(accs, axis=1)
        mu = jnp.mean(acc, axis=-1, keepdims=True)
        cen = acc - mu
        var = jnp.mean(cen * cen, axis=-1, keepdims=True)
        hn = cen * lax.rsqrt(var + EPS) * lg_ref[...] + lb_ref[...]
        o_ref[pl.ds(r0, rows), :] = (hn * jax.nn.sigmoid(hn)).astype(o_ref.dtype)

    sh = [sh_ref.at[i] for i in range(4)]
    shift_copies(0, sh[0])
    shift_copies(1, sh[1])

    def four_tiles(i, carry):
        t = 4 * i
        shift_copies(t + 2, sh[2])
        shift_copies(t + 3, sh[3])
        taps(t, sh[0])
        taps(t + 1, sh[1])
        shift_copies(jnp.minimum(t + 4, n_tiles - 2), sh[0])
        shift_copies(jnp.minimum(t + 5, n_tiles - 1), sh[1])
        taps(t + 2, sh[2])
        taps(t + 3, sh[3])
        return carry

    lax.fori_loop(0, n_tiles // 4, four_tiles, 0)


def _conv(hglu, p, layer, batch, seq):
    _, _, _, rows = _tiles(seq)
    assert seq % (4 * rows) == 0
    span = rows + 2 * CONV_HALO
    return pl.pallas_call(
        functools.partial(_conv_kernel, seq=seq, rows=rows),
        grid=(batch,),
        in_specs=[
            pl.BlockSpec((seq, CONV_W), lambda i: (i, 0)),
            _layer_block(layer, (CONV_K, CONV_W)),
            _layer_block(layer, (1, CONV_W)),
            _layer_block(layer, (1, CONV_W)),
            _layer_block(layer, (1, CONV_W)),
        ],
        out_specs=pl.BlockSpec((seq, CONV_W), lambda i: (i, 0)),
        out_shape=jax.ShapeDtypeStruct(hglu.shape, BF16),
        scratch_shapes=[
            pltpu.VMEM((seq + 2 * CONV_HALO, CONV_W), BF16),
            pltpu.VMEM((SUBLANES * span, span), BF16),
            pltpu.VMEM((4, SUBLANES * span, CONV_W), F32),
        ],
        compiler_params=pltpu.CompilerParams(dimension_semantics=("arbitrary",)),
        name="conv",
    )(hglu, p["conv_w"], p["conv_b"], p["conv_ln_g"], p["conv_ln_b"])


def _log_sigmoid(x):
    return jnp.minimum(x, 0.0) - jnp.log1p(jnp.exp(-jnp.abs(x)))


def _ret_kernel(dl_ref, q_ref, k_ref, v_ref, o_ref, decay_ref, sd_ref, kv_ref, st_ref, *, seq, chunk):
    C = chunk
    n_chunks = seq // C
    D = RET_HD
    lg_f = jnp.broadcast_to(_log_sigmoid(dl_ref[0, 0])[0:1, :], (C, LANES))
    lg_b = jnp.broadcast_to(_log_sigmoid(dl_ref[1, 0])[0:1, :], (C, LANES))
    ri = lax.broadcasted_iota(jnp.int32, (C, LANES), 0)
    idx = ri.astype(F32)
    for c in range(C // LANES):
        diff = (ri - (lax.broadcasted_iota(jnp.int32, (C, LANES), 1) + c * LANES)).astype(F32)
        decay_ref[:, c * LANES:(c + 1) * LANES] = jnp.where(
            diff >= 0.0, jnp.exp(lg_f * jnp.maximum(diff, 0.0)), jnp.exp(lg_b * jnp.maximum(-diff, 0.0)))
    qdec_f = jnp.exp(lg_f * (idx + 1.0))
    kdec_f = jnp.exp(lg_f * (C - 1.0 - idx))
    qdec_b = jnp.exp(lg_b * (C - idx))
    kdec_b = jnp.exp(lg_b * idx)
    cdec_f = jnp.exp(lg_f[:D] * float(C))
    cdec_b = jnp.exp(lg_b[:D] * float(C))
    trans_b = (((1,), (1,)), ((), ()))
    trans_a = (((0,), (0,)), ((), ()))

    def rows(n):
        return pl.ds(n * C, C)

    for n in range(n_chunks):
        qn, kn, vn = q_ref[rows(n), :], k_ref[rows(n), :], v_ref[rows(n), :]
        s = lax.dot_general(qn, kn, trans_b, preferred_element_type=F32)
        sd_ref[n] = (s * decay_ref[...]).astype(BF16)
        knf = kn.astype(F32)
        kv_ref[0, n] = lax.dot_general((knf * kdec_f).astype(BF16), vn, trans_a, preferred_element_type=F32)
        kv_ref[1, n] = lax.dot_general((knf * kdec_b).astype(BF16), vn, trans_a, preferred_element_type=F32)

    state = jnp.zeros((D, D), F32)
    for n in range(n_chunks):
        st_ref[0, n] = state.astype(BF16)
        state = cdec_f * state + kv_ref[0, n]
    state = jnp.zeros((D, D), F32)
    for n in reversed(range(n_chunks)):
        st_ref[1, n] = state.astype(BF16)
        state = cdec_b * state + kv_ref[1, n]

    for n in range(n_chunks):
        qnf = q_ref[rows(n), :].astype(F32)
        lhs = jnp.concatenate([sd_ref[n], (qnf * qdec_f).astype(BF16), (qnf * qdec_b).astype(BF16)], axis=1)
        rhs = jnp.concatenate([v_ref[rows(n), :], st_ref[0, n], st_ref[1, n]], axis=0)
        out = jnp.dot(lhs, rhs, preferred_element_type=F32)
        mu = jnp.mean(out, axis=-1, keepdims=True)
        cen = out - mu
        var = jnp.mean(cen * cen, axis=-1, keepdims=True)
        o_ref[rows(n), :] = (cen * lax.rsqrt(var + EPS)).astype(o_ref.dtype)


def _retention(rqkv, p, layer, batch, seq):
    chunk = 256 if seq % 256 == 0 else RET_CHUNK
    n_chunks = seq // chunk
    blk = lambda part: pl.BlockSpec((seq, RET_HD), lambda b, h: (b, part * RET_HEADS + h))
    return pl.pallas_call(
        functools.partial(_ret_kernel, seq=seq, chunk=chunk),
        grid=(batch, RET_HEADS),
        in_specs=[
            pl.BlockSpec((None, 2, 1, SUBLANES, LANES), lambda b, h: (layer, 0, h, 0, 0)),
            blk(0), blk(1), blk(2),
        ],
        out_specs=pl.BlockSpec((seq, RET_HD), lambda b, h: (b, h)),
        out_shape=jax.ShapeDtypeStruct((batch * seq, RET_W), BF16),
        scratch_shapes=[
            pltpu.VMEM((chunk, chunk), F32),
            pltpu.VMEM((n_chunks, chunk, chunk), BF16),
            pltpu.VMEM((2, n_chunks, RET_HD, RET_HD), F32),
            pltpu.VMEM((2, n_chunks, RET_HD, RET_HD), BF16),
        ],
        compiler_params=pltpu.CompilerParams(dimension_semantics=("arbitrary", "arbitrary")),
        name="retention",
    )(p["ret_decay"], rqkv, rqkv, rqkv)


def _mla_kernel(q_ref, kt_ref, v_ref, o_ref, *, sub):
    kt = kt_ref[...]
    v = v_ref[...]
    n_sub = q_ref.shape[0] // sub

    def scores(r):
        return jnp.dot(q_ref[r * sub:(r + 1) * sub, :], kt, preferred_element_type=F32)

    s_next = scores(0)
    for r in range(n_sub):
        rows = slice(r * sub, (r + 1) * sub)
        s = s_next
        if r + 1 < n_sub:
            s_next = scores(r + 1)
        m = jnp.max(s, axis=-1, keepdims=True)
        p = jnp.exp2(s - m)
        l = jnp.sum(p, axis=-1, keepdims=True)
        o = jnp.dot(p.astype(BF16), v, preferred_element_type=F32)
        o_ref[rows, :] = (o / l).astype(o_ref.dtype)


def _mla(mq, mkt, mv, batch, seq):
    _, sub, tq, _ = _tiles(seq)
    nq = seq // tq
    return pl.pallas_call(
        functools.partial(_mla_kernel, sub=sub),
        grid=(batch, MLA_HEADS, nq),
        in_specs=[
            pl.BlockSpec((tq, MLA_QK_PAD), lambda b, h, i: (b * nq + i, h)),
            pl.BlockSpec((MLA_QK_PAD, seq), lambda b, h, i: (h, b)),
            pl.BlockSpec((seq, MLA_V_HD), lambda b, h, i: (b, h)),
        ],
        out_specs=pl.BlockSpec((tq, MLA_V_HD), lambda b, h, i: (b * nq + i, h)),
        out_shape=jax.ShapeDtypeStruct((batch * seq, MLA_W), BF16),
        compiler_params=pltpu.CompilerParams(
            dimension_semantics=("arbitrary", "arbitrary", "arbitrary"), vmem_limit_bytes=VMEM_LIMIT),
        name="mla_attention",
    )(mq, mkt, mv)


def _outproj_kernel(x_ref, yc_ref, yr_ref, ym_ref, gate_ref, w_ref, fg_ref, o_ref, *, final):
    def gated(y_ref, lo, hi):
        return (y_ref[...].astype(F32) * gate_ref[:, lo:hi].astype(F32)).astype(BF16)

    acc = x_ref[...]
    acc = acc + jnp.dot(gated(yc_ref, 0, CONV_W), w_ref[0:CONV_W, :], preferred_element_type=F32)
    acc = acc + jnp.dot(gated(yr_ref, CONV_W, CONV_W + RET_W), w_ref[CONV_W:CONV_W + RET_W, :],
                        preferred_element_type=F32)
    acc = acc + jnp.dot(gated(ym_ref, CONV_W + RET_W, D_MIX), w_ref[CONV_W + RET_W:, :],
                        preferred_element_type=F32)
    if final:
        acc = _rmsnorm(acc, fg_ref[...])
    o_ref[...] = acc


def _outproj(x2, yc, yr, ym, gate, p, layer, seq, final):
    tm, _, _, _ = _tiles(seq)
    tokens = x2.shape[0]
    row = lambda i: (i, 0)
    return pl.pallas_call(
        functools.partial(_outproj_kernel, final=final),
        grid=(tokens // tm,),
        in_specs=[
            pl.BlockSpec((tm, D_MODEL), row),
            pl.BlockSpec((tm, CONV_W), row),
            pl.BlockSpec((tm, RET_W), row),
            pl.BlockSpec((tm, MLA_W), row),
            pl.BlockSpec((tm, D_MIX), row),
            _layer_block(layer, (D_MIX, D_MODEL)),
            pl.BlockSpec((1, D_MODEL), lambda i: (0, 0)),
        ],
        out_specs=pl.BlockSpec((tm, D_MODEL), row),
        out_shape=jax.ShapeDtypeStruct(x2.shape, F32),
        compiler_params=pltpu.CompilerParams(
            dimension_semantics=("arbitrary",), vmem_limit_bytes=VMEM_LIMIT),
        name="outproj",
    )(x2, yc, yr, ym, gate, p["w_out"], p["final_g"])


def _pad_last(w, width):
    return jnp.pad(w, [(0, 0)] * (w.ndim - 1) + [(0, width - w.shape[-1])])


def _prep_params(norm_g, w_in, conv_dw_w, conv_dw_b, conv_ln_g, conv_ln_b, ret_decay_logit,
                 mla_qa_g, mla_w_uq, mla_kva_g, mla_w_ukv, w_out, final_g):
    depth = norm_g.shape[0]
    uq = mla_w_uq.reshape(depth, MLA_Q_RANK, MLA_HEADS, MLA_NOPE + MLA_ROPE)
    uq = jnp.concatenate([uq[..., :MLA_NOPE].reshape(depth, MLA_Q_RANK, MLA_HEADS * MLA_NOPE),
                          uq[..., MLA_NOPE:].reshape(depth, MLA_Q_RANK, MLA_HEADS * MLA_ROPE)], axis=-1)
    ukv = mla_w_ukv.reshape(depth, MLA_KV_RANK, MLA_HEADS, MLA_NOPE + MLA_V_HD)
    ukt = ukv[..., :MLA_NOPE].reshape(depth, MLA_KV_RANK, MLA_HEADS * MLA_NOPE).transpose(0, 2, 1)
    uv = ukv[..., MLA_NOPE:].reshape(depth, MLA_KV_RANK, MLA_W)
    row = lambda a: a[:, None, :]
    return {
        "norm_g": row(norm_g),
        "w_main": jnp.concatenate([w_in[:, :, :OFF_QLAT], w_in[:, :, OFF_KVLAT:OFF_KROPE]], axis=-1).astype(BF16),
        "w_lat": jnp.concatenate([w_in[:, :, OFF_QLAT:OFF_KVLAT], w_in[:, :, OFF_KROPE:OFF_GATE],
                                  w_in[:, :, OFF_KROPE:OFF_GATE]], axis=-1).astype(BF16),
        "w_gate": w_in[:, :, OFF_GATE:].astype(BF16),
        "w_uq": uq.astype(BF16),
        "w_ukt": ukt.astype(BF16),
        "w_uv": uv.astype(BF16),
        "qa_g": row(mla_qa_g),
        "kva_g": row(mla_kva_g),
        "conv_w": conv_dw_w,
        "conv_b": row(conv_dw_b),
        "conv_ln_g": row(conv_ln_g),
        "conv_ln_b": row(conv_ln_b),
        "ret_decay": jnp.broadcast_to(ret_decay_logit[:, :, :, None, None], (depth, 2, RET_HEADS, SUBLANES, LANES)),
        "w_out": w_out.astype(BF16),
        "final_g": final_g[None, :],
    }


def _rope_inputs(seq):
    cos_r, sin_r = _rope_tables(seq, RET_HD)
    half = RET_HD // 2
    sin_r = jnp.concatenate([-sin_r[:, :half], sin_r[:, half:]], axis=1)
    cos_m, sin_m = _rope_tables(seq, MLA_ROPE)
    half = MLA_ROPE // 2
    zeros = jnp.zeros((seq, half), F32)
    cosm = jnp.concatenate([cos_m, cos_m], axis=1)
    sinma = jnp.concatenate([-sin_m[:, :half], zeros, -sin_m[:, :half], zeros], axis=1)
    sinmb = jnp.concatenate([zeros, sin_m[:, half:], zeros, sin_m[:, half:]], axis=1)
    return cos_r, sin_r, cosm, sinma, sinmb


def kernel(x, norm_g, w_in, conv_dw_w, conv_dw_b, conv_ln_g, conv_ln_b, ret_decay_logit,
           mla_qa_g, mla_w_uq, mla_kva_g, mla_w_ukv, w_out, final_g):
    batch, seq, d_model = x.shape
    depth = norm_g.shape[0]
    assert d_model == D_MODEL and seq % RET_CHUNK == 0
    p = _prep_params(norm_g, w_in, conv_dw_w, conv_dw_b, conv_ln_g, conv_ln_b, ret_decay_logit,
                     mla_qa_g, mla_w_uq, mla_kva_g, mla_w_ukv, w_out, final_g)
    tabs = _rope_inputs(seq)
    x2 = x.reshape(batch * seq, d_model)
    for layer in range(depth):
        hglu, rqkv, mq, mkt, mv, gate = _inproj(x2, p, layer, tabs, seq)
        yc = _conv(hglu, p, layer, batch, seq)
        yr = _retention(rqkv, p, layer, batch, seq)
        ym = _mla(mq, mkt, mv, batch, seq)
        x2 = _outproj(x2, yc, yr, ym, gate, p, layer, seq, final=(layer == depth - 1))
    return x2.reshape(batch, seq, d_model)
```

```python
import functools
import math

import jax
import jax.numpy as jnp
import numpy as np
from jax import lax
from jax.experimental import pallas as pl
from jax.experimental.pallas import tpu as pltpu

D_MODEL = 1024
D_MIX = 2 * D_MODEL
CONV_W = 512
CONV_K = 31
RET_W = 512
RET_HEADS = 4
RET_HD = 128
RET_CHUNK = 128
MLA_W = 1024
MLA_HEADS = 8
MLA_V_HD = 128
MLA_NOPE = 128
MLA_ROPE = 64
MLA_Q_RANK = 384
MLA_KV_RANK = 256
ROPE_BASE = 10000.0
EPS = 1e-6

OFF_RET = 2 * CONV_W
OFF_QLAT = OFF_RET + 3 * RET_W
OFF_KVLAT = OFF_QLAT + MLA_Q_RANK
OFF_KROPE = OFF_KVLAT + MLA_KV_RANK
OFF_GATE = OFF_KROPE + MLA_ROPE
N_IN = OFF_GATE + D_MIX
W_MAIN = OFF_QLAT + MLA_KV_RANK

LANES = 128
SUBLANES = 8
MLA_QK_PAD = 2 * LANES
CONV_HALO = 16
CONV_HOP = 128
CONV_WIN = CONV_HOP + 2 * CONV_HALO
VMEM_LIMIT = 56 * 1024 * 1024

BF16 = jnp.bfloat16
F32 = jnp.float32


def _tiles(seq):
    return min(512, seq), 256, min(2048, seq)


def _rope_tables(seq, dim):
    inv = 1.0 / (ROPE_BASE ** (jnp.arange(0, dim, 2, dtype=F32) / dim))
    ang = jnp.arange(seq, dtype=F32)[:, None] * inv[None, :]
    ang = jnp.concatenate([ang, ang], axis=-1)
    return jnp.cos(ang), jnp.sin(ang)


def _rmsnorm(x, g):
    return x * lax.rsqrt(jnp.mean(x * x, axis=-1, keepdims=True) + EPS) * g


def _layer_block(layer, shape):
    return pl.BlockSpec((None,) + shape, lambda *_: (layer,) + (0,) * len(shape), pipeline_mode=pl.Buffered(1))


def _inproj_kernel(x_ref, g_ref, wmain_ref, wlat_ref, wgate_ref, wuq_ref, wukt_ref, wuv_ref, qag_ref, kvag_ref,
                   cosr_ref, sinr_ref, cosm_ref, sinma_ref, sinmb_ref,
                   hglu_ref, rqkv_ref, mq_ref, mkt_ref, mv_ref, gate_ref, *, sub):
    for r in range(x_ref.shape[0] // sub):
        _inproj_rows(slice(r * sub, (r + 1) * sub), x_ref, g_ref, wmain_ref, wlat_ref, wgate_ref,
                     wuq_ref, wukt_ref, wuv_ref, qag_ref, kvag_ref,
                     cosr_ref, sinr_ref, cosm_ref, sinma_ref, sinmb_ref,
                     hglu_ref, rqkv_ref, mq_ref, mkt_ref, mv_ref, gate_ref)


def _inproj_rows(rows, x_ref, g_ref, wmain_ref, wlat_ref, wgate_ref, wuq_ref, wukt_ref, wuv_ref, qag_ref, kvag_ref,
                 cosr_ref, sinr_ref, cosm_ref, sinma_ref, sinmb_ref,
                 hglu_ref, rqkv_ref, mq_ref, mkt_ref, mv_ref, gate_ref):
    h = _rmsnorm(x_ref[rows, :], g_ref[...]).astype(BF16)

    lat = jnp.dot(h, wlat_ref[...], preferred_element_type=F32)
    q_lat, k_rope = lat[:, :MLA_Q_RANK], lat[:, MLA_Q_RANK:]

    gt = jnp.dot(h, wgate_ref[...], preferred_element_type=F32)
    gate_ref[rows, :] = (gt * jax.nn.sigmoid(gt)).astype(BF16)

    u_main = jnp.dot(h, wmain_ref[...], preferred_element_type=F32)

    hglu_ref[rows, :] = (u_main[:, :CONV_W] * jax.nn.sigmoid(u_main[:, CONV_W:2 * CONV_W])).astype(BF16)

    cosr = cosr_ref[rows, :]
    sinr = sinr_ref[rows, :]
    k_scale = RET_HD ** -0.5
    for part in range(3):
        for hd in range(RET_HEADS):
            col = part * RET_W + hd * RET_HD
            blk = u_main[:, OFF_RET + col:OFF_RET + col + RET_HD]
            if part < 2:
                blk = blk * cosr + pltpu.roll(blk, RET_HD // 2, 1) * sinr
            if part == 1:
                blk = blk * k_scale
            rqkv_ref[rows, col:col + RET_HD] = blk.astype(BF16)

    cosm = cosm_ref[rows, :]
    sinma = sinma_ref[rows, :]
    sinmb = sinmb_ref[rows, :]

    def rope_pair(blk):
        return (blk * cosm + pltpu.roll(blk, LANES - MLA_ROPE // 2, 1) * sinma
                + pltpu.roll(blk, MLA_ROPE // 2, 1) * sinmb)

    qn = _rmsnorm(q_lat, qag_ref[...]).astype(BF16)
    q_scale = (MLA_NOPE + MLA_ROPE) ** -0.5 * math.log2(math.e)
    q_all = jnp.dot(qn, wuq_ref[...], preferred_element_type=F32) * q_scale
    rope_off = MLA_HEADS * MLA_NOPE
    lane = lax.broadcasted_iota(jnp.int32, (q_all.shape[0], LANES), 1)
    for pair in range(MLA_HEADS // 2):
        roped = rope_pair(q_all[:, rope_off + pair * LANES:rope_off + (pair + 1) * LANES])
        for j in range(2):
            hd = 2 * pair + j
            own = (lane < MLA_ROPE) if j == 0 else (lane >= MLA_ROPE)
            mq_ref[rows, hd * MLA_QK_PAD:hd * MLA_QK_PAD + LANES] = q_all[:, hd * MLA_NOPE:(hd + 1) * MLA_NOPE].astype(BF16)
            mq_ref[rows, hd * MLA_QK_PAD + LANES:(hd + 1) * MLA_QK_PAD] = jnp.where(own, roped, 0.0).astype(BF16)

    kvn = _rmsnorm(u_main[:, OFF_QLAT:W_MAIN], kvag_ref[...]).astype(BF16)
    k_rope_t = rope_pair(k_rope).T.astype(BF16)
    k_nope_t = lax.dot_general(wukt_ref[...], kvn, (((1,), (1,)), ((), ())),
                               preferred_element_type=F32).astype(BF16)
    for hd in range(MLA_HEADS):
        mkt_ref[hd * MLA_QK_PAD:hd * MLA_QK_PAD + LANES, rows] = k_nope_t[hd * MLA_NOPE:(hd + 1) * MLA_NOPE, :]
        mkt_ref[hd * MLA_QK_PAD + LANES:(hd + 1) * MLA_QK_PAD, rows] = k_rope_t
    mv_ref[rows, :] = jnp.dot(kvn, wuv_ref[...], preferred_element_type=F32).astype(BF16)


def _inproj(x2, p, layer, tabs, seq):
    tm, sub, _ = _tiles(seq)
    tokens = x2.shape[0]
    nseq = seq // tm
    row = lambda i: (i, 0)
    pos = lambda i: (i % nseq, 0)
    return pl.pallas_call(
        functools.partial(_inproj_kernel, sub=sub),
        grid=(tokens // tm,),
        in_specs=[
            pl.BlockSpec((tm, D_MODEL), row),
            _layer_block(layer, (1, D_MODEL)),
            _layer_block(layer, (D_MODEL, W_MAIN)),
            _layer_block(layer, (D_MODEL, MLA_Q_RANK + LANES)),
            _layer_block(layer, (D_MODEL, D_MIX)),
            _layer_block(layer, (MLA_Q_RANK, MLA_HEADS * (MLA_NOPE + MLA_ROPE))),
            _layer_block(layer, (MLA_HEADS * MLA_NOPE, MLA_KV_RANK)),
            _layer_block(layer, (MLA_KV_RANK, MLA_W)),
            _layer_block(layer, (1, MLA_Q_RANK)),
            _layer_block(layer, (1, MLA_KV_RANK)),
        ] + [pl.BlockSpec((tm, LANES), pos)] * len(tabs),
        out_specs=[
            pl.BlockSpec((tm, CONV_W), row),
            pl.BlockSpec((tm, 3 * RET_W), row),
            pl.BlockSpec((tm, MLA_HEADS * MLA_QK_PAD), row),
            pl.BlockSpec((MLA_HEADS * MLA_QK_PAD, tm), lambda i: (0, i)),
            pl.BlockSpec((tm, MLA_W), row),
            pl.BlockSpec((tm, D_MIX), row),
        ],
        out_shape=[
            jax.ShapeDtypeStruct((tokens, CONV_W), BF16),
            jax.ShapeDtypeStruct((tokens, 3 * RET_W), BF16),
            jax.ShapeDtypeStruct((tokens, MLA_HEADS * MLA_QK_PAD), BF16),
            jax.ShapeDtypeStruct((MLA_HEADS * MLA_QK_PAD, tokens), BF16),
            jax.ShapeDtypeStruct((tokens, MLA_W), BF16),
            jax.ShapeDtypeStruct((tokens, D_MIX), BF16),
        ],
        compiler_params=pltpu.CompilerParams(
            dimension_semantics=("arbitrary",), vmem_limit_bytes=VMEM_LIMIT),
        name="inproj",
    )(x2, p["norm_g"], p["w_main"], p["w_lat"], p["w_gate"], p["w_uq"], p["w_ukt"], p["w_uv"],
      p["qa_g"], p["kva_g"], *tabs)


def _conv_tables():
    n = CONV_WIN
    k = np.arange(n)
    ang = 2.0 * np.pi * (np.outer(k, k) % n) / n
    hart = np.cos(ang) + np.sin(ang)
    both = np.concatenate([hart, hart[(-k) % n]], axis=0)

    def split(m):
        hi = m.astype(BF16)
        return hi, (m - hi.astype(np.float64)).astype(BF16)

    fwd_hi, fwd_lo = split(both)
    inv_hi, inv_lo = split(hart[:CONV_HOP])
    inv = np.concatenate([inv_hi, inv_lo, inv_hi], axis=1)
    taps = np.zeros((2 * n, CONV_K + 1), np.float32)
    taps[:, :CONV_K] = both[:, n - 1 - np.arange(CONV_K)]
    return jnp.asarray(fwd_hi), jnp.asarray(fwd_lo), jnp.asarray(inv), jnp.asarray(taps)


def _conv_kernel(h_ref, w_ref, b_ref, lg_ref, lb_ref, fwd_hi_ref, fwd_lo_ref, inv_ref, taps_ref, o_ref,
                 pad_ref, ge_ref, go_ref, *, seq):
    n = CONV_WIN
    zeros = jnp.zeros((CONV_HALO, CONV_W), BF16)
    pad_ref[0:CONV_HALO, :] = zeros
    pad_ref[CONV_HALO + seq:, :] = zeros
    pad_ref[CONV_HALO:CONV_HALO + seq, :] = h_ref[...]

    @pl.when(pl.program_id(0) == 0)
    def _():
        g = jnp.dot(taps_ref[...], w_ref[...], preferred_element_type=F32, precision=lax.Precision.HIGHEST)
        ge_ref[...] = (g[:n] + g[n:]) * (0.5 / n)
        go_ref[...] = (g[:n] - g[n:]) * (0.5 / n)

    def forward(i):
        win = pad_ref[i * CONV_HOP:i * CONV_HOP + n, :]
        return (jnp.dot(fwd_hi_ref[...], win, preferred_element_type=F32)
                + jnp.dot(fwd_lo_ref[...], win, preferred_element_type=F32))

    n_win = seq // CONV_HOP
    xs_next = forward(0)
    for i in range(n_win):
        xs = xs_next
        if i + 1 < n_win:
            xs_next = forward(i + 1)
        z = xs[:n] * ge_ref[...] + xs[n:] * go_ref[...]
        z_hi = z.astype(BF16)
        z_lo = (z - z_hi.astype(F32)).astype(BF16)
        acc = jnp.dot(inv_ref[...], jnp.concatenate([z_hi, z_hi, z_lo], axis=0), preferred_element_type=F32)
        acc = acc + b_ref[...]
        mu = jnp.mean(acc, axis=-1, keepdims=True)
        cen = acc - mu
        var = jnp.mean(cen * cen, axis=-1, keepdims=True)
        hn = cen * lax.rsqrt(var + EPS) * lg_ref[...] + lb_ref[...]
        o_ref[i * CONV_HOP:(i + 1) * CONV_HOP, :] = (hn * jax.nn.sigmoid(hn)).astype(o_ref.dtype)


def _conv(hglu, p, layer, batch, seq):
    assert seq % CONV_HOP == 0
    tables = _conv_tables()
    whole = lambda a: pl.BlockSpec(a.shape, lambda i: (0,) * a.ndim, pipeline_mode=pl.Buffered(1))
    return pl.pallas_call(
        functools.partial(_conv_kernel, seq=seq),
        grid=(batch,),
        in_specs=[
            pl.BlockSpec((seq, CONV_W), lambda i: (i, 0)),
            _layer_block(layer, (CONV_K + 1, CONV_W)),
            _layer_block(layer, (1, CONV_W)),
            _layer_block(layer, (1, CONV_W)),
            _layer_block(layer, (1, CONV_W)),
        ] + [whole(t) for t in tables],
        out_specs=pl.BlockSpec((seq, CONV_W), lambda i: (i, 0)),
        out_shape=jax.ShapeDtypeStruct(hglu.shape, BF16),
        scratch_shapes=[
            pltpu.VMEM((seq + 2 * CONV_HALO, CONV_W), BF16),
            pltpu.VMEM((CONV_WIN, CONV_W), F32),
            pltpu.VMEM((CONV_WIN, CONV_W), F32),
        ],
        compiler_params=pltpu.CompilerParams(dimension_semantics=("arbitrary",)),
        name="conv",
    )(hglu, p["conv_w"], p["conv_b"], p["conv_ln_g"], p["conv_ln_b"], *tables)


def _log_sigmoid(x):
    return jnp.minimum(x, 0.0) - jnp.log1p(jnp.exp(-jnp.abs(x)))


def _ret_kernel(dl_ref, q_ref, k_ref, v_ref, o_ref, decay_ref, sd_ref, kv_ref, st_ref, *, seq, chunk):
    C = chunk
    n_chunks = seq // C
    D = RET_HD
    lg_f = jnp.broadcast_to(_log_sigmoid(dl_ref[0, 0])[0:1, :], (C, LANES))
    lg_b = jnp.broadcast_to(_log_sigmoid(dl_ref[1, 0])[0:1, :], (C, LANES))
    ri = lax.broadcasted_iota(jnp.int32, (C, LANES), 0)
    idx = ri.astype(F32)
    for c in range(C // LANES):
        diff = (ri - (lax.broadcasted_iota(jnp.int32, (C, LANES), 1) + c * LANES)).astype(F32)
        decay_ref[:, c * LANES:(c + 1) * LANES] = jnp.where(
            diff >= 0.0, jnp.exp(lg_f * jnp.maximum(diff, 0.0)), jnp.exp(lg_b * jnp.maximum(-diff, 0.0)))
    qdec_f = jnp.exp(lg_f * (idx + 1.0))
    kdec_f = jnp.exp(lg_f * (C - 1.0 - idx))
    qdec_b = jnp.exp(lg_b * (C - idx))
    kdec_b = jnp.exp(lg_b * idx)
    cdec_f = jnp.exp(lg_f[:D] * float(C))
    cdec_b = jnp.exp(lg_b[:D] * float(C))
    trans_b = (((1,), (1,)), ((), ()))
    trans_a = (((0,), (0,)), ((), ()))

    def rows(n):
        return pl.ds(n * C, C)

    for n in range(n_chunks):
        qn, kn, vn = q_ref[rows(n), :], k_ref[rows(n), :], v_ref[rows(n), :]
        s = lax.dot_general(qn, kn, trans_b, preferred_element_type=F32)
        sd_ref[n] = (s * decay_ref[...]).astype(BF16)
        knf = kn.astype(F32)
        kv_ref[0, n] = lax.dot_general((knf * kdec_f).astype(BF16), vn, trans_a, preferred_element_type=F32)
        kv_ref[1, n] = lax.dot_general((knf * kdec_b).astype(BF16), vn, trans_a, preferred_element_type=F32)

    state = jnp.zeros((D, D), F32)
    for n in range(n_chunks):
        st_ref[0, n] = state.astype(BF16)
        state = cdec_f * state + kv_ref[0, n]
    state = jnp.zeros((D, D), F32)
    for n in reversed(range(n_chunks)):
        st_ref[1, n] = state.astype(BF16)
        state = cdec_b * state + kv_ref[1, n]

    for n in range(n_chunks):
        qnf = q_ref[rows(n), :].astype(F32)
        lhs = jnp.concatenate([sd_ref[n], (qnf * qdec_f).astype(BF16), (qnf * qdec_b).astype(BF16)], axis=1)
        rhs = jnp.concatenate([v_ref[rows(n), :], st_ref[0, n], st_ref[1, n]], axis=0)
        out = jnp.dot(lhs, rhs, preferred_element_type=F32)
        mu = jnp.mean(out, axis=-1, keepdims=True)
        cen = out - mu
        var = jnp.mean(cen * cen, axis=-1, keepdims=True)
        o_ref[rows(n), :] = (cen * lax.rsqrt(var + EPS)).astype(o_ref.dtype)


def _retention(rqkv, p, layer, batch, seq):
    chunk = 256 if seq % 256 == 0 else RET_CHUNK
    n_chunks = seq // chunk
    blk = lambda part: pl.BlockSpec((seq, RET_HD), lambda b, h: (b, part * RET_HEADS + h))
    return pl.pallas_call(
        functools.partial(_ret_kernel, seq=seq, chunk=chunk),
        grid=(batch, RET_HEADS),
        in_specs=[
            pl.BlockSpec((None, 2, 1, SUBLANES, LANES), lambda b, h: (layer, 0, h, 0, 0)),
            blk(0), blk(1), blk(2),
        ],
        out_specs=pl.BlockSpec((seq, RET_HD), lambda b, h: (b, h)),
        out_shape=jax.ShapeDtypeStruct((batch * seq, RET_W), BF16),
        scratch_shapes=[
            pltpu.VMEM((chunk, chunk), F32),
            pltpu.VMEM((n_chunks, chunk, chunk), BF16),
            pltpu.VMEM((2, n_chunks, RET_HD, RET_HD), F32),
            pltpu.VMEM((2, n_chunks, RET_HD, RET_HD), BF16),
        ],
        compiler_params=pltpu.CompilerParams(dimension_semantics=("arbitrary", "arbitrary")),
        name="retention",
    )(p["ret_decay"], rqkv, rqkv, rqkv)


def _mla_kernel(q_ref, kt_ref, v_ref, o_ref, *, sub):
    kt = kt_ref[...]
    v = jnp.concatenate([v_ref[...], jnp.ones(v_ref.shape, BF16)], axis=1)
    n_sub = q_ref.shape[0] // sub

    def scores(r):
        return jnp.dot(q_ref[r * sub:(r + 1) * sub, :], kt, preferred_element_type=F32)

    s_next = scores(0)
    for r in range(n_sub):
        rows = slice(r * sub, (r + 1) * sub)
        s = s_next
        if r + 1 < n_sub:
            s_next = scores(r + 1)
        m = jnp.max(s, axis=-1, keepdims=True)
        p = jnp.exp2(s - m)
        o = jnp.dot(p.astype(BF16), v, preferred_element_type=F32)
        o_ref[rows, :] = (o[:, :MLA_V_HD] / o[:, MLA_V_HD:]).astype(o_ref.dtype)


def _mla(mq, mkt, mv, batch, seq):
    _, sub, tq = _tiles(seq)
    nq = seq // tq
    return pl.pallas_call(
        functools.partial(_mla_kernel, sub=sub),
        grid=(batch, MLA_HEADS, nq),
        in_specs=[
            pl.BlockSpec((tq, MLA_QK_PAD), lambda b, h, i: (b * nq + i, h)),
            pl.BlockSpec((MLA_QK_PAD, seq), lambda b, h, i: (h, b)),
            pl.BlockSpec((seq, MLA_V_HD), lambda b, h, i: (b, h)),
        ],
        out_specs=pl.BlockSpec((tq, MLA_V_HD), lambda b, h, i: (b * nq + i, h)),
        out_shape=jax.ShapeDtypeStruct((batch * seq, MLA_W), BF16),
        compiler_params=pltpu.CompilerParams(
            dimension_semantics=("arbitrary", "arbitrary", "arbitrary"), vmem_limit_bytes=VMEM_LIMIT),
        name="mla_attention",
    )(mq, mkt, mv)


def _outproj_kernel(x_ref, yc_ref, yr_ref, ym_ref, gate_ref, w_ref, fg_ref, o_ref, *, final):
    def gated(y_ref, lo, hi):
        return (y_ref[...].astype(F32) * gate_ref[:, lo:hi].astype(F32)).astype(BF16)

    acc = x_ref[...]
    acc = acc + jnp.dot(gated(yc_ref, 0, CONV_W), w_ref[0:CONV_W, :], preferred_element_type=F32)
    acc = acc + jnp.dot(gated(yr_ref, CONV_W, CONV_W + RET_W), w_ref[CONV_W:CONV_W + RET_W, :],
                        preferred_element_type=F32)
    acc = acc + jnp.dot(gated(ym_ref, CONV_W + RET_W, D_MIX), w_ref[CONV_W + RET_W:, :],
                        preferred_element_type=F32)
    if final:
        acc = _rmsnorm(acc, fg_ref[...])
    o_ref[...] = acc


def _outproj(x2, yc, yr, ym, gate, p, layer, seq, final):
    tm, _, _ = _tiles(seq)
    tokens = x2.shape[0]
    row = lambda i: (i, 0)
    return pl.pallas_call(
        functools.partial(_outproj_kernel, final=final),
        grid=(tokens // tm,),
        in_specs=[
            pl.BlockSpec((tm, D_MODEL), row),
            pl.BlockSpec((tm, CONV_W), row),
            pl.BlockSpec((tm, RET_W), row),
            pl.BlockSpec((tm, MLA_W), row),
            pl.BlockSpec((tm, D_MIX), row),
            _layer_block(layer, (D_MIX, D_MODEL)),
            pl.BlockSpec((1, D_MODEL), lambda i: (0, 0)),
        ],
        out_specs=pl.BlockSpec((tm, D_MODEL), row),
        out_shape=jax.ShapeDtypeStruct(x2.shape, F32),
        compiler_params=pltpu.CompilerParams(
            dimension_semantics=("arbitrary",), vmem_limit_bytes=VMEM_LIMIT),
        name="outproj",
    )(x2, yc, yr, ym, gate, p["w_out"], p["final_g"])


def _pad_last(w, width):
    return jnp.pad(w, [(0, 0)] * (w.ndim - 1) + [(0, width - w.shape[-1])])


def _prep_params(norm_g, w_in, conv_dw_w, conv_dw_b, conv_ln_g, conv_ln_b, ret_decay_logit,
                 mla_qa_g, mla_w_uq, mla_kva_g, mla_w_ukv, w_out, final_g):
    depth = norm_g.shape[0]
    uq = mla_w_uq.reshape(depth, MLA_Q_RANK, MLA_HEADS, MLA_NOPE + MLA_ROPE)
    uq = jnp.concatenate([uq[..., :MLA_NOPE].reshape(depth, MLA_Q_RANK, MLA_HEADS * MLA_NOPE),
                          uq[..., MLA_NOPE:].reshape(depth, MLA_Q_RANK, MLA_HEADS * MLA_ROPE)], axis=-1)
    ukv = mla_w_ukv.reshape(depth, MLA_KV_RANK, MLA_HEADS, MLA_NOPE + MLA_V_HD)
    ukt = ukv[..., :MLA_NOPE].reshape(depth, MLA_KV_RANK, MLA_HEADS * MLA_NOPE).transpose(0, 2, 1)
    uv = ukv[..., MLA_NOPE:].reshape(depth, MLA_KV_RANK, MLA_W)
    row = lambda a: a[:, None, :]
    return {
        "norm_g": row(norm_g),
        "w_main": jnp.concatenate([w_in[:, :, :OFF_QLAT], w_in[:, :, OFF_KVLAT:OFF_KROPE]], axis=-1).astype(BF16),
        "w_lat": jnp.concatenate([w_in[:, :, OFF_QLAT:OFF_KVLAT], w_in[:, :, OFF_KROPE:OFF_GATE],
                                  w_in[:, :, OFF_KROPE:OFF_GATE]], axis=-1).astype(BF16),
        "w_gate": w_in[:, :, OFF_GATE:].astype(BF16),
        "w_uq": uq.astype(BF16),
        "w_ukt": ukt.astype(BF16),
        "w_uv": uv.astype(BF16),
        "qa_g": row(mla_qa_g),
        "kva_g": row(mla_kva_g),
        "conv_w": jnp.pad(conv_dw_w, ((0, 0), (0, 1), (0, 0))),
        "conv_b": row(conv_dw_b),
        "conv_ln_g": row(conv_ln_g),
        "conv_ln_b": row(conv_ln_b),
        "ret_decay": jnp.broadcast_to(ret_decay_logit[:, :, :, None, None], (depth, 2, RET_HEADS, SUBLANES, LANES)),
        "w_out": w_out.astype(BF16),
        "final_g": final_g[None, :],
    }


def _rope_inputs(seq):
    cos_r, sin_r = _rope_tables(seq, RET_HD)
    half = RET_HD // 2
    sin_r = jnp.concatenate([-sin_r[:, :half], sin_r[:, half:]], axis=1)
    cos_m, sin_m = _rope_tables(seq, MLA_ROPE)
    half = MLA_ROPE // 2
    zeros = jnp.zeros((seq, half), F32)
    cosm = jnp.concatenate([cos_m, cos_m], axis=1)
    sinma = jnp.concatenate([-sin_m[:, :half], zeros, -sin_m[:, :half], zeros], axis=1)
    sinmb = jnp.concatenate([zeros, sin_m[:, half:], zeros, sin_m[:, half:]], axis=1)
    return cos_r, sin_r, cosm, sinma, sinmb


def kernel(x, norm_g, w_in, conv_dw_w, conv_dw_b, conv_ln_g, conv_ln_b, ret_decay_logit,
           mla_qa_g, mla_w_uq, mla_kva_g, mla_w_ukv, w_out, final_g):
    batch, seq, d_model = x.shape
    depth = norm_g.shape[0]
    assert d_model == D_MODEL and seq % RET_CHUNK == 0
    p = _prep_params(norm_g, w_in, conv_dw_w, conv_dw_b, conv_ln_g, conv_ln_b, ret_decay_logit,
                     mla_qa_g, mla_w_uq, mla_kva_g, mla_w_ukv, w_out, final_g)
    tabs = _rope_inputs(seq)
    x2 = x.reshape(batch * seq, d_model)
    for layer in range(depth):
        hglu, rqkv, mq, mkt, mv, gate = _inproj(x2, p, layer, tabs, seq)
        yc = _conv(hglu, p, layer, batch, seq)
        yr = _retention(rqkv, p, layer, batch, seq)
        ym = _mla(mq, mkt, mv, batch, seq)
        x2 = _outproj(x2, yc, yr, ym, gate, p, layer, seq, final=(layer == depth - 1))
    return x2.reshape(batch, seq, d_model)
```

```python
import functools
import math

import jax
import jax.numpy as jnp
import numpy as np
from jax import lax
from jax.experimental import pallas as pl
from jax.experimental.pallas import tpu as pltpu

D_MODEL = 1024
D_MIX = 2 * D_MODEL
CONV_W = 512
CONV_K = 31
RET_W = 512
RET_HEADS = 4
RET_HD = 128
RET_CHUNK = 128
MLA_W = 1024
MLA_HEADS = 8
MLA_V_HD = 128
MLA_NOPE = 128
MLA_ROPE = 64
MLA_Q_RANK = 384
MLA_KV_RANK = 256
ROPE_BASE = 10000.0
EPS = 1e-6

OFF_RET = 2 * CONV_W
OFF_QLAT = OFF_RET + 3 * RET_W
OFF_KVLAT = OFF_QLAT + MLA_Q_RANK
OFF_KROPE = OFF_KVLAT + MLA_KV_RANK
OFF_GATE = OFF_KROPE + MLA_ROPE
N_IN = OFF_GATE + D_MIX
W_MAIN = OFF_QLAT + MLA_KV_RANK

LANES = 128
SUBLANES = 8
MLA_QK_PAD = 2 * LANES
CONV_HALO = 16
CONV_HOP = 128
CONV_WIN = CONV_HOP + 2 * CONV_HALO
VMEM_LIMIT = 56 * 1024 * 1024

BF16 = jnp.bfloat16
F32 = jnp.float32


def _tiles(seq):
    return min(512, seq), 256, min(2048, seq)


def _rope_tables(seq, dim):
    inv = 1.0 / (ROPE_BASE ** (jnp.arange(0, dim, 2, dtype=F32) / dim))
    ang = jnp.arange(seq, dtype=F32)[:, None] * inv[None, :]
    ang = jnp.concatenate([ang, ang], axis=-1)
    return jnp.cos(ang), jnp.sin(ang)


def _rmsnorm(x, g):
    return x * lax.rsqrt(jnp.mean(x * x, axis=-1, keepdims=True) + EPS) * g


def _layer_block(layer, shape):
    return pl.BlockSpec((None,) + shape, lambda *_: (layer,) + (0,) * len(shape), pipeline_mode=pl.Buffered(1))


def _inproj_kernel(x_ref, g_ref, wmain_ref, wlat_ref, wgate_ref, wuq_ref, wukt_ref, wuv_ref, qag_ref, kvag_ref,
                   cosr_ref, sinr_ref, cosm_ref, sinma_ref, sinmb_ref,
                   hglu_ref, rqkv_ref, mq_ref, mkt_ref, mv_ref, gate_ref, *, sub):
    for r in range(x_ref.shape[0] // sub):
        _inproj_rows(slice(r * sub, (r + 1) * sub), x_ref, g_ref, wmain_ref, wlat_ref, wgate_ref,
                     wuq_ref, wukt_ref, wuv_ref, qag_ref, kvag_ref,
                     cosr_ref, sinr_ref, cosm_ref, sinma_ref, sinmb_ref,
                     hglu_ref, rqkv_ref, mq_ref, mkt_ref, mv_ref, gate_ref)


def _inproj_rows(rows, x_ref, g_ref, wmain_ref, wlat_ref, wgate_ref, wuq_ref, wukt_ref, wuv_ref, qag_ref, kvag_ref,
                 cosr_ref, sinr_ref, cosm_ref, sinma_ref, sinmb_ref,
                 hglu_ref, rqkv_ref, mq_ref, mkt_ref, mv_ref, gate_ref):
    h = _rmsnorm(x_ref[rows, :], g_ref[...]).astype(BF16)

    lat = jnp.dot(h, wlat_ref[...], preferred_element_type=F32)
    q_lat, k_rope = lat[:, :MLA_Q_RANK], lat[:, MLA_Q_RANK:]

    gt = jnp.dot(h, wgate_ref[...], preferred_element_type=F32)
    gate_ref[rows, :] = (gt * jax.nn.sigmoid(gt)).astype(BF16)

    u_main = jnp.dot(h, wmain_ref[...], preferred_element_type=F32)

    hglu_ref[rows, :] = (u_main[:, :CONV_W] * jax.nn.sigmoid(u_main[:, CONV_W:2 * CONV_W])).astype(BF16)

    cosr = cosr_ref[rows, :]
    sinr = sinr_ref[rows, :]
    k_scale = RET_HD ** -0.5
    for part in range(3):
        for hd in range(RET_HEADS):
            col = part * RET_W + hd * RET_HD
            blk = u_main[:, OFF_RET + col:OFF_RET + col + RET_HD]
            if part < 2:
                blk = blk * cosr + pltpu.roll(blk, RET_HD // 2, 1) * sinr
            if part == 1:
                blk = blk * k_scale
            rqkv_ref[rows, col:col + RET_HD] = blk.astype(BF16)

    cosm = cosm_ref[rows, :]
    sinma = sinma_ref[rows, :]
    sinmb = sinmb_ref[rows, :]

    def rope_pair(blk):
        return (blk * cosm + pltpu.roll(blk, LANES - MLA_ROPE // 2, 1) * sinma
                + pltpu.roll(blk, MLA_ROPE // 2, 1) * sinmb)

    qn = _rmsnorm(q_lat, qag_ref[...]).astype(BF16)
    q_scale = (MLA_NOPE + MLA_ROPE) ** -0.5 * math.log2(math.e)
    q_all = jnp.dot(qn, wuq_ref[...], preferred_element_type=F32) * q_scale
    rope_off = MLA_HEADS * MLA_NOPE
    lane = lax.broadcasted_iota(jnp.int32, (q_all.shape[0], LANES), 1)
    for pair in range(MLA_HEADS // 2):
        roped = rope_pair(q_all[:, rope_off + pair * LANES:rope_off + (pair + 1) * LANES])
        for j in range(2):
            hd = 2 * pair + j
            own = (lane < MLA_ROPE) if j == 0 else (lane >= MLA_ROPE)
            mq_ref[rows, hd * MLA_QK_PAD:hd * MLA_QK_PAD + LANES] = q_all[:, hd * MLA_NOPE:(hd + 1) * MLA_NOPE].astype(BF16)
            mq_ref[rows, hd * MLA_QK_PAD + LANES:(hd + 1) * MLA_QK_PAD] = jnp.where(own, roped, 0.0).astype(BF16)

    kvn = _rmsnorm(u_main[:, OFF_QLAT:W_MAIN], kvag_ref[...]).astype(BF16)
    k_rope_t = rope_pair(k_rope).T.astype(BF16)
    k_nope_t = lax.dot_general(wukt_ref[...], kvn, (((1,), (1,)), ((), ())),
                               preferred_element_type=F32).astype(BF16)
    for hd in range(MLA_HEADS):
        mkt_ref[hd * MLA_QK_PAD:hd * MLA_QK_PAD + LANES, rows] = k_nope_t[hd * MLA_NOPE:(hd + 1) * MLA_NOPE, :]
        mkt_ref[hd * MLA_QK_PAD + LANES:(hd + 1) * MLA_QK_PAD, rows] = k_rope_t
    mv_ref[rows, :] = jnp.dot(kvn, wuv_ref[...], preferred_element_type=F32).astype(BF16)


def _inproj(x2, p, layer, tabs, seq):
    tm, sub, _ = _tiles(seq)
    tokens = x2.shape[0]
    nseq = seq // tm
    row = lambda i: (i, 0)
    pos = lambda i: (i % nseq, 0)
    return pl.pallas_call(
        functools.partial(_inproj_kernel, sub=sub),
        grid=(tokens // tm,),
        in_specs=[
            pl.BlockSpec((tm, D_MODEL), row),
            _layer_block(layer, (1, D_MODEL)),
            _layer_block(layer, (D_MODEL, W_MAIN)),
            _layer_block(layer, (D_MODEL, MLA_Q_RANK + LANES)),
            _layer_block(layer, (D_MODEL, D_MIX)),
            _layer_block(layer, (MLA_Q_RANK, MLA_HEADS * (MLA_NOPE + MLA_ROPE))),
            _layer_block(layer, (MLA_HEADS * MLA_NOPE, MLA_KV_RANK)),
            _layer_block(layer, (MLA_KV_RANK, MLA_W)),
            _layer_block(layer, (1, MLA_Q_RANK)),
            _layer_block(layer, (1, MLA_KV_RANK)),
        ] + [pl.BlockSpec((tm, LANES), pos)] * len(tabs),
        out_specs=[
            pl.BlockSpec((tm, CONV_W), row),
            pl.BlockSpec((tm, 3 * RET_W), row),
            pl.BlockSpec((tm, MLA_HEADS * MLA_QK_PAD), row),
            pl.BlockSpec((MLA_HEADS * MLA_QK_PAD, tm), lambda i: (0, i)),
            pl.BlockSpec((tm, MLA_W), row),
            pl.BlockSpec((tm, D_MIX), row),
        ],
        out_shape=[
            jax.ShapeDtypeStruct((tokens, CONV_W), BF16),
            jax.ShapeDtypeStruct((tokens, 3 * RET_W), BF16),
            jax.ShapeDtypeStruct((tokens, MLA_HEADS * MLA_QK_PAD), BF16),
            jax.ShapeDtypeStruct((MLA_HEADS * MLA_QK_PAD, tokens), BF16),
            jax.ShapeDtypeStruct((tokens, MLA_W), BF16),
            jax.ShapeDtypeStruct((tokens, D_MIX), BF16),
        ],
        compiler_params=pltpu.CompilerParams(
            dimension_semantics=("arbitrary",), vmem_limit_bytes=VMEM_LIMIT),
        name="inproj",
    )(x2, p["norm_g"], p["w_main"], p["w_lat"], p["w_gate"], p["w_uq"], p["w_ukt"], p["w_uv"],
      p["qa_g"], p["kva_g"], *tabs)


def _conv_tables():
    n = CONV_WIN
    k = np.arange(n)
    ang = 2.0 * np.pi * (np.outer(k, k) % n) / n
    hart = np.cos(ang) + np.sin(ang)
    both = np.concatenate([hart, hart[(-k) % n]], axis=0)

    def split(m):
        hi = m.astype(BF16)
        return hi, (m - hi.astype(np.float64)).astype(BF16)

    fwd_hi, fwd_lo = split(both)
    inv_hi, inv_lo = split(hart[:CONV_HOP])
    inv = np.concatenate([inv_hi, inv_lo, inv_hi], axis=1)
    taps = np.zeros((2 * n, CONV_K + 1), np.float32)
    taps[:, :CONV_K] = both[:, n - 1 - np.arange(CONV_K)]
    return jnp.asarray(fwd_hi), jnp.asarray(fwd_lo), jnp.asarray(inv), jnp.asarray(taps)


def _conv_kernel(h_ref, gate_ref, w_ref, b_ref, lg_ref, lb_ref, fwd_hi_ref, fwd_lo_ref, inv_ref, taps_ref, o_ref,
                 pad_ref, ge_ref, go_ref, *, seq):
    n = CONV_WIN
    zeros = jnp.zeros((CONV_HALO, CONV_W), BF16)
    pad_ref[0:CONV_HALO, :] = zeros
    pad_ref[CONV_HALO + seq:, :] = zeros
    pad_ref[CONV_HALO:CONV_HALO + seq, :] = h_ref[...]

    @pl.when(pl.program_id(0) == 0)
    def _():
        g = jnp.dot(taps_ref[...], w_ref[...], preferred_element_type=F32, precision=lax.Precision.HIGHEST)
        ge_ref[...] = (g[:n] + g[n:]) * (0.5 / n)
        go_ref[...] = (g[:n] - g[n:]) * (0.5 / n)

    def forward(i):
        win = pad_ref[i * CONV_HOP:i * CONV_HOP + n, :]
        return (jnp.dot(fwd_hi_ref[...], win, preferred_element_type=F32)
                + jnp.dot(fwd_lo_ref[...], win, preferred_element_type=F32))

    n_win = seq // CONV_HOP
    xs_next = forward(0)
    for i in range(n_win):
        xs = xs_next
        if i + 1 < n_win:
            xs_next = forward(i + 1)
        z = xs[:n] * ge_ref[...] + xs[n:] * go_ref[...]
        z_hi = z.astype(BF16)
        z_lo = (z - z_hi.astype(F32)).astype(BF16)
        acc = jnp.dot(inv_ref[...], jnp.concatenate([z_hi, z_hi, z_lo], axis=0), preferred_element_type=F32)
        acc = acc + b_ref[...]
        mu = jnp.mean(acc, axis=-1, keepdims=True)
        cen = acc - mu
        var = jnp.mean(cen * cen, axis=-1, keepdims=True)
        hn = cen * lax.rsqrt(var + EPS) * lg_ref[...] + lb_ref[...]
        out_rows = slice(i * CONV_HOP, (i + 1) * CONV_HOP)
        o_ref[out_rows, :] = (hn * jax.nn.sigmoid(hn) * gate_ref[out_rows, :].astype(F32)).astype(o_ref.dtype)


def _conv(hglu, gate, p, layer, batch, seq):
    assert seq % CONV_HOP == 0
    tables = _conv_tables()
    whole = lambda a: pl.BlockSpec(a.shape, lambda i: (0,) * a.ndim, pipeline_mode=pl.Buffered(1))
    return pl.pallas_call(
        functools.partial(_conv_kernel, seq=seq),
        grid=(batch,),
        in_specs=[
            pl.BlockSpec((seq, CONV_W), lambda i: (i, 0)),
            pl.BlockSpec((seq, CONV_W), lambda i: (i, 0)),
            _layer_block(layer, (CONV_K + 1, CONV_W)),
            _layer_block(layer, (1, CONV_W)),
            _layer_block(layer, (1, CONV_W)),
            _layer_block(layer, (1, CONV_W)),
        ] + [whole(t) for t in tables],
        out_specs=pl.BlockSpec((seq, CONV_W), lambda i: (i, 0)),
        out_shape=jax.ShapeDtypeStruct(hglu.shape, BF16),
        scratch_shapes=[
            pltpu.VMEM((seq + 2 * CONV_HALO, CONV_W), BF16),
            pltpu.VMEM((CONV_WIN, CONV_W), F32),
            pltpu.VMEM((CONV_WIN, CONV_W), F32),
        ],
        compiler_params=pltpu.CompilerParams(dimension_semantics=("arbitrary",)),
        name="conv",
    )(hglu, gate, p["conv_w"], p["conv_b"], p["conv_ln_g"], p["conv_ln_b"], *tables)


def _log_sigmoid(x):
    return jnp.minimum(x, 0.0) - jnp.log1p(jnp.exp(-jnp.abs(x)))


def _ret_kernel(dl_ref, q_ref, k_ref, v_ref, gate_ref, o_ref, decay_ref, sd_ref, kv_ref, st_ref, *, seq, chunk):
    C = chunk
    n_chunks = seq // C
    D = RET_HD
    lg_f = jnp.broadcast_to(_log_sigmoid(dl_ref[0, 0])[0:1, :], (C, LANES))
    lg_b = jnp.broadcast_to(_log_sigmoid(dl_ref[1, 0])[0:1, :], (C, LANES))
    ri = lax.broadcasted_iota(jnp.int32, (C, LANES), 0)
    idx = ri.astype(F32)
    for c in range(C // LANES):
        diff = (ri - (lax.broadcasted_iota(jnp.int32, (C, LANES), 1) + c * LANES)).astype(F32)
        decay_ref[:, c * LANES:(c + 1) * LANES] = jnp.where(
            diff >= 0.0, jnp.exp(lg_f * jnp.maximum(diff, 0.0)), jnp.exp(lg_b * jnp.maximum(-diff, 0.0)))
    qdec_f = jnp.exp(lg_f * (idx + 1.0))
    kdec_f = jnp.exp(lg_f * (C - 1.0 - idx))
    qdec_b = jnp.exp(lg_b * (C - idx))
    kdec_b = jnp.exp(lg_b * idx)
    cdec_f = jnp.exp(lg_f[:D] * float(C))
    cdec_b = jnp.exp(lg_b[:D] * float(C))
    trans_b = (((1,), (1,)), ((), ()))
    trans_a = (((0,), (0,)), ((), ()))

    def rows(n):
        return pl.ds(n * C, C)

    for n in range(n_chunks):
        qn, kn, vn = q_ref[rows(n), :], k_ref[rows(n), :], v_ref[rows(n), :]
        s = lax.dot_general(qn, kn, trans_b, preferred_element_type=F32)
        sd_ref[n] = (s * decay_ref[...]).astype(BF16)
        knf = kn.astype(F32)
        kv_ref[0, n] = lax.dot_general((knf * kdec_f).astype(BF16), vn, trans_a, preferred_element_type=F32)
        kv_ref[1, n] = lax.dot_general((knf * kdec_b).astype(BF16), vn, trans_a, preferred_element_type=F32)

    state = jnp.zeros((D, D), F32)
    for n in range(n_chunks):
        st_ref[0, n] = state.astype(BF16)
        state = cdec_f * state + kv_ref[0, n]
    state = jnp.zeros((D, D), F32)
    for n in reversed(range(n_chunks)):
        st_ref[1, n] = state.astype(BF16)
        state = cdec_b * state + kv_ref[1, n]

    for n in range(n_chunks):
        qnf = q_ref[rows(n), :].astype(F32)
        lhs = jnp.concatenate([sd_ref[n], (qnf * qdec_f).astype(BF16), (qnf * qdec_b).astype(BF16)], axis=1)
        rhs = jnp.concatenate([v_ref[rows(n), :], st_ref[0, n], st_ref[1, n]], axis=0)
        out = jnp.dot(lhs, rhs, preferred_element_type=F32)
        mu = jnp.mean(out, axis=-1, keepdims=True)
        cen = out - mu
        var = jnp.mean(cen * cen, axis=-1, keepdims=True)
        o_ref[rows(n), :] = (cen * lax.rsqrt(var + EPS) * gate_ref[rows(n), :].astype(F32)).astype(o_ref.dtype)


def _retention(rqkv, gate, p, layer, batch, seq):
    chunk = 256 if seq % 256 == 0 else RET_CHUNK
    n_chunks = seq // chunk
    blk = lambda part: pl.BlockSpec((seq, RET_HD), lambda b, h: (b, part * RET_HEADS + h))
    return pl.pallas_call(
        functools.partial(_ret_kernel, seq=seq, chunk=chunk),
        grid=(batch, RET_HEADS),
        in_specs=[
            pl.BlockSpec((None, 2, 1, SUBLANES, LANES), lambda b, h: (layer, 0, h, 0, 0)),
            blk(0), blk(1), blk(2),
            pl.BlockSpec((seq, RET_HD), lambda b, h: (b, CONV_W // RET_HD + h)),
        ],
        out_specs=pl.BlockSpec((seq, RET_HD), lambda b, h: (b, h)),
        out_shape=jax.ShapeDtypeStruct((batch * seq, RET_W), BF16),
        scratch_shapes=[
            pltpu.VMEM((chunk, chunk), F32),
            pltpu.VMEM((n_chunks, chunk, chunk), BF16),
            pltpu.VMEM((2, n_chunks, RET_HD, RET_HD), F32),
            pltpu.VMEM((2, n_chunks, RET_HD, RET_HD), BF16),
        ],
        compiler_params=pltpu.CompilerParams(dimension_semantics=("arbitrary", "arbitrary")),
        name="retention",
    )(p["ret_decay"], rqkv, rqkv, rqkv, gate)


def _mla_kernel(q_ref, kt_ref, v_ref, gate_ref, o_ref, *, sub):
    kt = kt_ref[...]
    v = jnp.concatenate([v_ref[...], jnp.ones(v_ref.shape, BF16)], axis=1)
    n_sub = q_ref.shape[0] // sub

    def scores(r):
        return jnp.dot(q_ref[r * sub:(r + 1) * sub, :], kt, preferred_element_type=F32)

    s_next = scores(0)
    for r in range(n_sub):
        rows = slice(r * sub, (r + 1) * sub)
        s = s_next
        if r + 1 < n_sub:
            s_next = scores(r + 1)
        m = jnp.max(s, axis=-1, keepdims=True)
        p = jnp.exp2(s - m)
        o = jnp.dot(p.astype(BF16), v, preferred_element_type=F32)
        o_ref[rows, :] = (o[:, :MLA_V_HD] / o[:, MLA_V_HD:] * gate_ref[rows, :].astype(F32)).astype(o_ref.dtype)


def _mla(mq, mkt, mv, gate, batch, seq):
    _, sub, tq = _tiles(seq)
    nq = seq // tq
    return pl.pallas_call(
        functools.partial(_mla_kernel, sub=sub),
        grid=(batch, MLA_HEADS, nq),
        in_specs=[
            pl.BlockSpec((tq, MLA_QK_PAD), lambda b, h, i: (b * nq + i, h)),
            pl.BlockSpec((MLA_QK_PAD, seq), lambda b, h, i: (h, b)),
            pl.BlockSpec((seq, MLA_V_HD), lambda b, h, i: (b, h)),
            pl.BlockSpec((tq, MLA_V_HD), lambda b, h, i: (b * nq + i, (CONV_W + RET_W) // MLA_V_HD + h)),
        ],
        out_specs=pl.BlockSpec((tq, MLA_V_HD), lambda b, h, i: (b * nq + i, h)),
        out_shape=jax.ShapeDtypeStruct((batch * seq, MLA_W), BF16),
        compiler_params=pltpu.CompilerParams(
            dimension_semantics=("arbitrary", "arbitrary", "arbitrary"), vmem_limit_bytes=VMEM_LIMIT),
        name="mla_attention",
    )(mq, mkt, mv, gate)


def _outproj_kernel(x_ref, yc_ref, yr_ref, ym_ref, w_ref, fg_ref, o_ref, *, final):
    acc = x_ref[...]
    acc = acc + jnp.dot(yc_ref[...], w_ref[0:CONV_W, :], preferred_element_type=F32)
    acc = acc + jnp.dot(yr_ref[...], w_ref[CONV_W:CONV_W + RET_W, :], preferred_element_type=F32)
    acc = acc + jnp.dot(ym_ref[...], w_ref[CONV_W + RET_W:, :], preferred_element_type=F32)
    if final:
        acc = _rmsnorm(acc, fg_ref[...])
    o_ref[...] = acc


def _outproj(x2, yc, yr, ym, p, layer, seq, final):
    tm, _, _ = _tiles(seq)
    tokens = x2.shape[0]
    row = lambda i: (i, 0)
    return pl.pallas_call(
        functools.partial(_outproj_kernel, final=final),
        grid=(tokens // tm,),
        in_specs=[
            pl.BlockSpec((tm, D_MODEL), row),
            pl.BlockSpec((tm, CONV_W), row),
            pl.BlockSpec((tm, RET_W), row),
            pl.BlockSpec((tm, MLA_W), row),
            _layer_block(layer, (D_MIX, D_MODEL)),
            pl.BlockSpec((1, D_MODEL), lambda i: (0, 0)),
        ],
        out_specs=pl.BlockSpec((tm, D_MODEL), row),
        out_shape=jax.ShapeDtypeStruct(x2.shape, F32),
        compiler_params=pltpu.CompilerParams(
            dimension_semantics=("arbitrary",), vmem_limit_bytes=VMEM_LIMIT),
        name="outproj",
    )(x2, yc, yr, ym, p["w_out"], p["final_g"])


def _pad_last(w, width):
    return jnp.pad(w, [(0, 0)] * (w.ndim - 1) + [(0, width - w.shape[-1])])


def _prep_params(norm_g, w_in, conv_dw_w, conv_dw_b, conv_ln_g, conv_ln_b, ret_decay_logit,
                 mla_qa_g, mla_w_uq, mla_kva_g, mla_w_ukv, w_out, final_g):
    depth = norm_g.shape[0]
    uq = mla_w_uq.reshape(depth, MLA_Q_RANK, MLA_HEADS, MLA_NOPE + MLA_ROPE)
    uq = jnp.concatenate([uq[..., :MLA_NOPE].reshape(depth, MLA_Q_RANK, MLA_HEADS * MLA_NOPE),
                          uq[..., MLA_NOPE:].reshape(depth, MLA_Q_RANK, MLA_HEADS * MLA_ROPE)], axis=-1)
    ukv = mla_w_ukv.reshape(depth, MLA_KV_RANK, MLA_HEADS, MLA_NOPE + MLA_V_HD)
    ukt = ukv[..., :MLA_NOPE].reshape(depth, MLA_KV_RANK, MLA_HEADS * MLA_NOPE).transpose(0, 2, 1)
    uv = ukv[..., MLA_NOPE:].reshape(depth, MLA_KV_RANK, MLA_W)
    row = lambda a: a[:, None, :]
    return {
        "norm_g": row(norm_g),
        "w_main": jnp.concatenate([w_in[:, :, :OFF_QLAT], w_in[:, :, OFF_KVLAT:OFF_KROPE]], axis=-1).astype(BF16),
        "w_lat": jnp.concatenate([w_in[:, :, OFF_QLAT:OFF_KVLAT], w_in[:, :, OFF_KROPE:OFF_GATE],
                                  w_in[:, :, OFF_KROPE:OFF_GATE]], axis=-1).astype(BF16),
        "w_gate": w_in[:, :, OFF_GATE:].astype(BF16),
        "w_uq": uq.astype(BF16),
        "w_ukt": ukt.astype(BF16),
        "w_uv": uv.astype(BF16),
        "qa_g": row(mla_qa_g),
        "kva_g": row(mla_kva_g),
        "conv_w": jnp.pad(conv_dw_w, ((0, 0), (0, 1), (0, 0))),
        "conv_b": row(conv_dw_b),
        "conv_ln_g": row(conv_ln_g),
        "conv_ln_b": row(conv_ln_b),
        "ret_decay": jnp.broadcast_to(ret_decay_logit[:, :, :, None, None], (depth, 2, RET_HEADS, SUBLANES, LANES)),
        "w_out": w_out.astype(BF16),
        "final_g": final_g[None, :],
    }


def _rope_inputs(seq):
    cos_r, sin_r = _rope_tables(seq, RET_HD)
    half = RET_HD // 2
    sin_r = jnp.concatenate([-sin_r[:, :half], sin_r[:, half:]], axis=1)
    cos_m, sin_m = _rope_tables(seq, MLA_ROPE)
    half = MLA_ROPE // 2
    zeros = jnp.zeros((seq, half), F32)
    cosm = jnp.concatenate([cos_m, cos_m], axis=1)
    sinma = jnp.concatenate([-sin_m[:, :half], zeros, -sin_m[:, :half], zeros], axis=1)
    sinmb = jnp.concatenate([zeros, sin_m[:, half:], zeros, sin_m[:, half:]], axis=1)
    return cos_r, sin_r, cosm, sinma, sinmb


def kernel(x, norm_g, w_in, conv_dw_w, conv_dw_b, conv_ln_g, conv_ln_b, ret_decay_logit,
           mla_qa_g, mla_w_uq, mla_kva_g, mla_w_ukv, w_out, final_g):
    batch, seq, d_model = x.shape
    depth = norm_g.shape[0]
    assert d_model == D_MODEL and seq % RET_CHUNK == 0
    p = _prep_params(norm_g, w_in, conv_dw_w, conv_dw_b, conv_ln_g, conv_ln_b, ret_decay_logit,
                     mla_qa_g, mla_w_uq, mla_kva_g, mla_w_ukv, w_out, final_g)
    tabs = _rope_inputs(seq)
    x2 = x.reshape(batch * seq, d_model)
    for layer in range(depth):
        hglu, rqkv, mq, mkt, mv, gate = _inproj(x2, p, layer, tabs, seq)
        yc = _conv(hglu, gate, p, layer, batch, seq)
        yr = _retention(rqkv, gate, p, layer, batch, seq)
        ym = _mla(mq, mkt, mv, gate, batch, seq)
        x2 = _outproj(x2, yc, yr, ym, p, layer, seq, final=(layer == depth - 1))
    return x2.reshape(batch, seq, d_model)
```

```python
import functools
import math

import jax
import jax.numpy as jnp
import numpy as np
from jax import lax
from jax.experimental import pallas as pl
from jax.experimental.pallas import tpu as pltpu

D_MODEL = 1024
D_MIX = 2 * D_MODEL
CONV_W = 512
CONV_K = 31
RET_W = 512
RET_HEADS = 4
RET_HD = 128
RET_CHUNK = 128
MLA_W = 1024
MLA_HEADS = 8
MLA_V_HD = 128
MLA_NOPE = 128
MLA_ROPE = 64
MLA_Q_RANK = 384
MLA_KV_RANK = 256
ROPE_BASE = 10000.0
EPS = 1e-6

OFF_RET = 2 * CONV_W
OFF_QLAT = OFF_RET + 3 * RET_W
OFF_KVLAT = OFF_QLAT + MLA_Q_RANK
OFF_KROPE = OFF_KVLAT + MLA_KV_RANK
OFF_GATE = OFF_KROPE + MLA_ROPE
N_IN = OFF_GATE + D_MIX
W_MAIN = OFF_QLAT + MLA_KV_RANK

LANES = 128
SUBLANES = 8
MLA_QK_PAD = 2 * LANES
CONV_HALO = 16
CONV_HOP = 128
CONV_WIN = CONV_HOP + 2 * CONV_HALO
VMEM_LIMIT = 56 * 1024 * 1024

BF16 = jnp.bfloat16
F32 = jnp.float32


def _tiles(seq):
    return min(512, seq), 256, min(2048, seq)


def _rope_tables(seq, dim):
    f32 = np.float32
    inv = f32(1.0) / (f32(ROPE_BASE) ** (np.arange(0, dim, 2, dtype=f32) / f32(dim)))
    ang = np.arange(seq, dtype=f32)[:, None] * inv[None, :]
    ang = np.concatenate([ang, ang], axis=-1).astype(np.float64)
    return np.cos(ang).astype(f32), np.sin(ang).astype(f32)


def _rmsnorm(x, g):
    return x * lax.rsqrt(jnp.mean(x * x, axis=-1, keepdims=True) + EPS) * g


def _layer_block(layer, shape):
    return pl.BlockSpec((None,) + shape, lambda *_: (layer,) + (0,) * len(shape), pipeline_mode=pl.Buffered(1))


def _inproj_kernel(x_ref, g_ref, wmain_ref, wlat_ref, wgate_ref, wuq_ref, wukt_ref, wuv_ref, qag_ref, kvag_ref,
                   cosr_ref, sinr_ref, cosm_ref, sinma_ref, sinmb_ref,
                   hglu_ref, rqkv_ref, mq_ref, mkt_ref, mv_ref, gate_ref, *, sub):
    for r in range(x_ref.shape[0] // sub):
        _inproj_rows(slice(r * sub, (r + 1) * sub), x_ref, g_ref, wmain_ref, wlat_ref, wgate_ref,
                     wuq_ref, wukt_ref, wuv_ref, qag_ref, kvag_ref,
                     cosr_ref, sinr_ref, cosm_ref, sinma_ref, sinmb_ref,
                     hglu_ref, rqkv_ref, mq_ref, mkt_ref, mv_ref, gate_ref)


def _inproj_rows(rows, x_ref, g_ref, wmain_ref, wlat_ref, wgate_ref, wuq_ref, wukt_ref, wuv_ref, qag_ref, kvag_ref,
                 cosr_ref, sinr_ref, cosm_ref, sinma_ref, sinmb_ref,
                 hglu_ref, rqkv_ref, mq_ref, mkt_ref, mv_ref, gate_ref):
    h = _rmsnorm(x_ref[rows, :], g_ref[...]).astype(BF16)

    lat = jnp.dot(h, wlat_ref[...], preferred_element_type=F32)
    q_lat, k_rope = lat[:, :MLA_Q_RANK], lat[:, MLA_Q_RANK:]

    gt = jnp.dot(h, wgate_ref[...], preferred_element_type=F32)
    gate_ref[rows, :] = (gt * jax.nn.sigmoid(gt)).astype(BF16)

    u_main = jnp.dot(h, wmain_ref[...], preferred_element_type=F32)

    hglu_ref[rows, :] = (u_main[:, :CONV_W] * jax.nn.sigmoid(u_main[:, CONV_W:2 * CONV_W])).astype(BF16)

    cosr = cosr_ref[rows, :]
    sinr = sinr_ref[rows, :]
    k_scale = RET_HD ** -0.5
    for part in range(3):
        for hd in range(RET_HEADS):
            col = part * RET_W + hd * RET_HD
            blk = u_main[:, OFF_RET + col:OFF_RET + col + RET_HD]
            if part < 2:
                blk = blk * cosr + pltpu.roll(blk, RET_HD // 2, 1) * sinr
            if part == 1:
                blk = blk * k_scale
            rqkv_ref[rows, col:col + RET_HD] = blk.astype(BF16)

    cosm = cosm_ref[rows, :]
    sinma = sinma_ref[rows, :]
    sinmb = sinmb_ref[rows, :]

    def rope_pair(blk):
        return (blk * cosm + pltpu.roll(blk, LANES - MLA_ROPE // 2, 1) * sinma
                + pltpu.roll(blk, MLA_ROPE // 2, 1) * sinmb)

    qn = _rmsnorm(q_lat, qag_ref[...]).astype(BF16)
    q_scale = (MLA_NOPE + MLA_ROPE) ** -0.5 * math.log2(math.e)
    q_all = jnp.dot(qn, wuq_ref[...], preferred_element_type=F32) * q_scale
    rope_off = MLA_HEADS * MLA_NOPE
    lane = lax.broadcasted_iota(jnp.int32, (q_all.shape[0], LANES), 1)
    for pair in range(MLA_HEADS // 2):
        roped = rope_pair(q_all[:, rope_off + pair * LANES:rope_off + (pair + 1) * LANES])
        for j in range(2):
            hd = 2 * pair + j
            own = (lane < MLA_ROPE) if j == 0 else (lane >= MLA_ROPE)
            mq_ref[rows, hd * MLA_QK_PAD:hd * MLA_QK_PAD + LANES] = q_all[:, hd * MLA_NOPE:(hd + 1) * MLA_NOPE].astype(BF16)
            mq_ref[rows, hd * MLA_QK_PAD + LANES:(hd + 1) * MLA_QK_PAD] = jnp.where(own, roped, 0.0).astype(BF16)

    kvn = _rmsnorm(u_main[:, OFF_QLAT:W_MAIN], kvag_ref[...]).astype(BF16)
    k_rope_t = rope_pair(k_rope).T.astype(BF16)
    k_nope_t = lax.dot_general(wukt_ref[...], kvn, (((1,), (1,)), ((), ())),
                               preferred_element_type=F32).astype(BF16)
    for hd in range(MLA_HEADS):
        mkt_ref[hd * MLA_QK_PAD:hd * MLA_QK_PAD + LANES, rows] = k_nope_t[hd * MLA_NOPE:(hd + 1) * MLA_NOPE, :]
        mkt_ref[hd * MLA_QK_PAD + LANES:(hd + 1) * MLA_QK_PAD, rows] = k_rope_t
    mv_ref[rows, :] = jnp.dot(kvn, wuv_ref[...], preferred_element_type=F32).astype(BF16)


def _inproj(x2, p, layer, tabs, seq):
    tm, sub, _ = _tiles(seq)
    tokens = x2.shape[0]
    nseq = seq // tm
    row = lambda i: (i, 0)
    pos = lambda i: (i % nseq, 0)
    return pl.pallas_call(
        functools.partial(_inproj_kernel, sub=sub),
        grid=(tokens // tm,),
        in_specs=[
            pl.BlockSpec((tm, D_MODEL), row),
            _layer_block(layer, (1, D_MODEL)),
            _layer_block(layer, (D_MODEL, W_MAIN)),
            _layer_block(layer, (D_MODEL, MLA_Q_RANK + LANES)),
            _layer_block(layer, (D_MODEL, D_MIX)),
            _layer_block(layer, (MLA_Q_RANK, MLA_HEADS * (MLA_NOPE + MLA_ROPE))),
            _layer_block(layer, (MLA_HEADS * MLA_NOPE, MLA_KV_RANK)),
            _layer_block(layer, (MLA_KV_RANK, MLA_W)),
            _layer_block(layer, (1, MLA_Q_RANK)),
            _layer_block(layer, (1, MLA_KV_RANK)),
        ] + [pl.BlockSpec((tm, LANES), pos)] * len(tabs),
        out_specs=[
            pl.BlockSpec((tm, CONV_W), row),
            pl.BlockSpec((tm, 3 * RET_W), row),
            pl.BlockSpec((tm, MLA_HEADS * MLA_QK_PAD), row),
            pl.BlockSpec((MLA_HEADS * MLA_QK_PAD, tm), lambda i: (0, i)),
            pl.BlockSpec((tm, MLA_W), row),
            pl.BlockSpec((tm, D_MIX), row),
        ],
        out_shape=[
            jax.ShapeDtypeStruct((tokens, CONV_W), BF16),
            jax.ShapeDtypeStruct((tokens, 3 * RET_W), BF16),
            jax.ShapeDtypeStruct((tokens, MLA_HEADS * MLA_QK_PAD), BF16),
            jax.ShapeDtypeStruct((MLA_HEADS * MLA_QK_PAD, tokens), BF16),
            jax.ShapeDtypeStruct((tokens, MLA_W), BF16),
            jax.ShapeDtypeStruct((tokens, D_MIX), BF16),
        ],
        compiler_params=pltpu.CompilerParams(
            dimension_semantics=("arbitrary",), vmem_limit_bytes=VMEM_LIMIT),
        name="inproj",
    )(x2, p["norm_g"], p["w_main"], p["w_lat"], p["w_gate"], p["w_uq"], p["w_ukt"], p["w_uv"],
      p["qa_g"], p["kva_g"], *tabs)


def _conv_tables():
    n = CONV_WIN
    k = np.arange(n)
    ang = 2.0 * np.pi * (np.outer(k, k) % n) / n
    hart = np.cos(ang) + np.sin(ang)
    both = np.concatenate([hart, hart[(-k) % n]], axis=0)

    def split(m):
        hi = m.astype(BF16)
        return hi, (m - hi.astype(np.float64)).astype(BF16)

    fwd_hi, fwd_lo = split(both)
    inv_hi, inv_lo = split(hart[:CONV_HOP])
    inv = np.concatenate([inv_hi, inv_lo, inv_hi], axis=1)
    taps = np.zeros((2 * n, CONV_K + 1), np.float32)
    taps[:, :CONV_K] = both[:, n - 1 - np.arange(CONV_K)]
    return jnp.asarray(fwd_hi), jnp.asarray(fwd_lo), jnp.asarray(inv), jnp.asarray(taps)


def _conv_kernel(h_ref, gate_ref, w_ref, b_ref, lg_ref, lb_ref, fwd_hi_ref, fwd_lo_ref, inv_ref, taps_ref, o_ref,
                 pad_ref, ge_ref, go_ref, *, seq):
    n = CONV_WIN
    zeros = jnp.zeros((CONV_HALO, CONV_W), BF16)
    pad_ref[0:CONV_HALO, :] = zeros
    pad_ref[CONV_HALO + seq:, :] = zeros
    pad_ref[CONV_HALO:CONV_HALO + seq, :] = h_ref[...]

    @pl.when(pl.program_id(0) == 0)
    def _():
        g = jnp.dot(taps_ref[...], w_ref[...], preferred_element_type=F32, precision=lax.Precision.HIGHEST)
        ge_ref[...] = (g[:n] + g[n:]) * (0.5 / n)
        go_ref[...] = (g[:n] - g[n:]) * (0.5 / n)

    def forward(i):
        win = pad_ref[i * CONV_HOP:i * CONV_HOP + n, :]
        return (jnp.dot(fwd_hi_ref[...], win, preferred_element_type=F32)
                + jnp.dot(fwd_lo_ref[...], win, preferred_element_type=F32))

    n_win = seq // CONV_HOP
    xs_next = forward(0)
    for i in range(n_win):
        xs = xs_next
        if i + 1 < n_win:
            xs_next = forward(i + 1)
        z = xs[:n] * ge_ref[...] + xs[n:] * go_ref[...]
        z_hi = z.astype(BF16)
        z_lo = (z - z_hi.astype(F32)).astype(BF16)
        acc = jnp.dot(inv_ref[...], jnp.concatenate([z_hi, z_hi, z_lo], axis=0), preferred_element_type=F32)
        acc = acc + b_ref[...]
        mu = jnp.mean(acc, axis=-1, keepdims=True)
        cen = acc - mu
        var = jnp.mean(cen * cen, axis=-1, keepdims=True)
        hn = cen * lax.rsqrt(var + EPS) * lg_ref[...] + lb_ref[...]
        out_rows = slice(i * CONV_HOP, (i + 1) * CONV_HOP)
        o_ref[out_rows, :] = (hn * jax.nn.sigmoid(hn) * gate_ref[out_rows, :].astype(F32)).astype(o_ref.dtype)


def _conv(hglu, gate, p, layer, batch, seq):
    assert seq % CONV_HOP == 0
    tables = _conv_tables()
    whole = lambda a: pl.BlockSpec(a.shape, lambda i: (0,) * a.ndim, pipeline_mode=pl.Buffered(1))
    return pl.pallas_call(
        functools.partial(_conv_kernel, seq=seq),
        grid=(batch,),
        in_specs=[
            pl.BlockSpec((seq, CONV_W), lambda i: (i, 0)),
            pl.BlockSpec((seq, CONV_W), lambda i: (i, 0)),
            _layer_block(layer, (CONV_K + 1, CONV_W)),
            _layer_block(layer, (1, CONV_W)),
            _layer_block(layer, (1, CONV_W)),
            _layer_block(layer, (1, CONV_W)),
        ] + [whole(t) for t in tables],
        out_specs=pl.BlockSpec((seq, CONV_W), lambda i: (i, 0)),
        out_shape=jax.ShapeDtypeStruct(hglu.shape, BF16),
        scratch_shapes=[
            pltpu.VMEM((seq + 2 * CONV_HALO, CONV_W), BF16),
            pltpu.VMEM((CONV_WIN, CONV_W), F32),
            pltpu.VMEM((CONV_WIN, CONV_W), F32),
        ],
        compiler_params=pltpu.CompilerParams(dimension_semantics=("arbitrary",)),
        name="conv",
    )(hglu, gate, p["conv_w"], p["conv_b"], p["conv_ln_g"], p["conv_ln_b"], *tables)


def _log_sigmoid(x):
    return jnp.minimum(x, 0.0) - jnp.log1p(jnp.exp(-jnp.abs(x)))


def _ret_kernel(dl_ref, q_ref, k_ref, v_ref, gate_ref, o_ref, decay_ref, sd_ref, kv_ref, st_ref, *, seq, chunk):
    C = chunk
    n_chunks = seq // C
    D = RET_HD
    lg_f = jnp.broadcast_to(_log_sigmoid(dl_ref[0, 0])[0:1, :], (C, LANES))
    lg_b = jnp.broadcast_to(_log_sigmoid(dl_ref[1, 0])[0:1, :], (C, LANES))
    ri = lax.broadcasted_iota(jnp.int32, (C, LANES), 0)
    idx = ri.astype(F32)
    for c in range(C // LANES):
        diff = (ri - (lax.broadcasted_iota(jnp.int32, (C, LANES), 1) + c * LANES)).astype(F32)
        decay_ref[:, c * LANES:(c + 1) * LANES] = jnp.where(
            diff >= 0.0, jnp.exp(lg_f * jnp.maximum(diff, 0.0)), jnp.exp(lg_b * jnp.maximum(-diff, 0.0)))
    qdec_f = jnp.exp(lg_f * (idx + 1.0))
    kdec_f = jnp.exp(lg_f * (C - 1.0 - idx))
    qdec_b = jnp.exp(lg_b * (C - idx))
    kdec_b = jnp.exp(lg_b * idx)
    cdec_f = jnp.exp(lg_f[:D] * float(C))
    cdec_b = jnp.exp(lg_b[:D] * float(C))
    trans_b = (((1,), (1,)), ((), ()))
    trans_a = (((0,), (0,)), ((), ()))

    def rows(n):
        return pl.ds(n * C, C)

    for n in range(n_chunks):
        qn, kn, vn = q_ref[rows(n), :], k_ref[rows(n), :], v_ref[rows(n), :]
        s = lax.dot_general(qn, kn, trans_b, preferred_element_type=F32)
        sd_ref[n] = (s * decay_ref[...]).astype(BF16)
        knf = kn.astype(F32)
        kv_ref[0, n] = lax.dot_general((knf * kdec_f).astype(BF16), vn, trans_a, preferred_element_type=F32)
        kv_ref[1, n] = lax.dot_general((knf * kdec_b).astype(BF16), vn, trans_a, preferred_element_type=F32)

    state = jnp.zeros((D, D), F32)
    for n in range(n_chunks):
        st_ref[0, n] = state.astype(BF16)
        state = cdec_f * state + kv_ref[0, n]
    state = jnp.zeros((D, D), F32)
    for n in reversed(range(n_chunks)):
        st_ref[1, n] = state.astype(BF16)
        state = cdec_b * state + kv_ref[1, n]

    for n in range(n_chunks):
        qnf = q_ref[rows(n), :].astype(F32)
        lhs = jnp.concatenate([sd_ref[n], (qnf * qdec_f).astype(BF16), (qnf * qdec_b).astype(BF16)], axis=1)
        rhs = jnp.concatenate([v_ref[rows(n), :], st_ref[0, n], st_ref[1, n]], axis=0)
        out = jnp.dot(lhs, rhs, preferred_element_type=F32)
        mu = jnp.mean(out, axis=-1, keepdims=True)
        cen = out - mu
        var = jnp.mean(cen * cen, axis=-1, keepdims=True)
        o_ref[rows(n), :] = (cen * lax.rsqrt(var + EPS) * gate_ref[rows(n), :].astype(F32)).astype(o_ref.dtype)


def _retention(rqkv, gate, p, layer, batch, seq):
    chunk = 256 if seq % 256 == 0 else RET_CHUNK
    n_chunks = seq // chunk
    blk = lambda part: pl.BlockSpec((seq, RET_HD), lambda b, h: (b, part * RET_HEADS + h))
    return pl.pallas_call(
        functools.partial(_ret_kernel, seq=seq, chunk=chunk),
        grid=(batch, RET_HEADS),
        in_specs=[
            pl.BlockSpec((None, 2, 1, SUBLANES, LANES), lambda b, h: (layer, 0, h, 0, 0)),
            blk(0), blk(1), blk(2),
            pl.BlockSpec((seq, RET_HD), lambda b, h: (b, CONV_W // RET_HD + h)),
        ],
        out_specs=pl.BlockSpec((seq, RET_HD), lambda b, h: (b, h)),
        out_shape=jax.ShapeDtypeStruct((batch * seq, RET_W), BF16),
        scratch_shapes=[
            pltpu.VMEM((chunk, chunk), F32),
            pltpu.VMEM((n_chunks, chunk, chunk), BF16),
            pltpu.VMEM((2, n_chunks, RET_HD, RET_HD), F32),
            pltpu.VMEM((2, n_chunks, RET_HD, RET_HD), BF16),
        ],
        compiler_params=pltpu.CompilerParams(dimension_semantics=("arbitrary", "arbitrary")),
        name="retention",
    )(p["ret_decay"], rqkv, rqkv, rqkv, gate)


def _mla_kernel(q_ref, kt_ref, v_ref, gate_ref, o_ref, *, sub):
    kt = kt_ref[...]
    v = jnp.concatenate([v_ref[...], jnp.ones(v_ref.shape, BF16)], axis=1)
    n_sub = q_ref.shape[0] // sub

    def scores(r):
        return jnp.dot(q_ref[r * sub:(r + 1) * sub, :], kt, preferred_element_type=F32)

    s_next = scores(0)
    for r in range(n_sub):
        rows = slice(r * sub, (r + 1) * sub)
        s = s_next
        if r + 1 < n_sub:
            s_next = scores(r + 1)
        m = jnp.max(s, axis=-1, keepdims=True)
        p = jnp.exp2(s - m)
        o = jnp.dot(p.astype(BF16), v, preferred_element_type=F32)
        o_ref[rows, :] = (o[:, :MLA_V_HD] / o[:, MLA_V_HD:] * gate_ref[rows, :].astype(F32)).astype(o_ref.dtype)


def _mla(mq, mkt, mv, gate, batch, seq):
    _, sub, tq = _tiles(seq)
    nq = seq // tq
    return pl.pallas_call(
        functools.partial(_mla_kernel, sub=sub),
        grid=(batch, MLA_HEADS, nq),
        in_specs=[
            pl.BlockSpec((tq, MLA_QK_PAD), lambda b, h, i: (b * nq + i, h)),
            pl.BlockSpec((MLA_QK_PAD, seq), lambda b, h, i: (h, b)),
            pl.BlockSpec((seq, MLA_V_HD), lambda b, h, i: (b, h)),
            pl.BlockSpec((tq, MLA_V_HD), lambda b, h, i: (b * nq + i, (CONV_W + RET_W) // MLA_V_HD + h)),
        ],
        out_specs=pl.BlockSpec((tq, MLA_V_HD), lambda b, h, i: (b * nq + i, h)),
        out_shape=jax.ShapeDtypeStruct((batch * seq, MLA_W), BF16),
        compiler_params=pltpu.CompilerParams(
            dimension_semantics=("arbitrary", "arbitrary", "arbitrary"), vmem_limit_bytes=VMEM_LIMIT),
        name="mla_attention",
    )(mq, mkt, mv, gate)


def _outproj_kernel(x_ref, yc_ref, yr_ref, ym_ref, w_ref, fg_ref, o_ref, *, final):
    acc = x_ref[...]
    acc = acc + jnp.dot(yc_ref[...], w_ref[0:CONV_W, :], preferred_element_type=F32)
    acc = acc + jnp.dot(yr_ref[...], w_ref[CONV_W:CONV_W + RET_W, :], preferred_element_type=F32)
    acc = acc + jnp.dot(ym_ref[...], w_ref[CONV_W + RET_W:, :], preferred_element_type=F32)
    if final:
        acc = _rmsnorm(acc, fg_ref[...])
    o_ref[...] = acc


def _outproj(x2, yc, yr, ym, p, layer, seq, final):
    tm, _, _ = _tiles(seq)
    tokens = x2.shape[0]
    row = lambda i: (i, 0)
    return pl.pallas_call(
        functools.partial(_outproj_kernel, final=final),
        grid=(tokens // tm,),
        in_specs=[
            pl.BlockSpec((tm, D_MODEL), row),
            pl.BlockSpec((tm, CONV_W), row),
            pl.BlockSpec((tm, RET_W), row),
            pl.BlockSpec((tm, MLA_W), row),
            _layer_block(layer, (D_MIX, D_MODEL)),
            pl.BlockSpec((1, D_MODEL), lambda i: (0, 0)),
        ],
        out_specs=pl.BlockSpec((tm, D_MODEL), row),
        out_shape=jax.ShapeDtypeStruct(x2.shape, F32),
        compiler_params=pltpu.CompilerParams(
            dimension_semantics=("arbitrary",), vmem_limit_bytes=VMEM_LIMIT),
        name="outproj",
    )(x2, yc, yr, ym, p["w_out"], p["final_g"])


def _pad_last(w, width):
    return jnp.pad(w, [(0, 0)] * (w.ndim - 1) + [(0, width - w.shape[-1])])


def _split_w_in_kernel(w_ref, main_ref, lat_ref, gate_ref):
    w = w_ref[...]
    k_rope = w[:, OFF_KROPE:OFF_GATE]
    main_ref[...] = jnp.concatenate([w[:, :OFF_QLAT], w[:, OFF_KVLAT:OFF_KROPE]], axis=1).astype(BF16)
    lat_ref[...] = jnp.concatenate([w[:, OFF_QLAT:OFF_KVLAT], k_rope, k_rope], axis=1).astype(BF16)
    gate_ref[...] = w[:, OFF_GATE:].astype(BF16)


def _split_w_in(w_in):
    depth = w_in.shape[0]
    rows = 256
    blk = lambda width: pl.BlockSpec((None, rows, width), lambda l, i: (l, i, 0))
    widths = (W_MAIN, MLA_Q_RANK + LANES, D_MIX)
    return pl.pallas_call(
        _split_w_in_kernel,
        grid=(depth, D_MODEL // rows),
        in_specs=[blk(N_IN)],
        out_specs=[blk(w) for w in widths],
        out_shape=[jax.ShapeDtypeStruct((depth, D_MODEL, w), BF16) for w in widths],
        compiler_params=pltpu.CompilerParams(dimension_semantics=("arbitrary", "arbitrary")),
        name="split_w_in",
    )(w_in)


def _prep_params(norm_g, w_in, conv_dw_w, conv_dw_b, conv_ln_g, conv_ln_b, ret_decay_logit,
                 mla_qa_g, mla_w_uq, mla_kva_g, mla_w_ukv, w_out, final_g):
    depth = norm_g.shape[0]
    uq = mla_w_uq.reshape(depth, MLA_Q_RANK, MLA_HEADS, MLA_NOPE + MLA_ROPE)
    uq = jnp.concatenate([uq[..., :MLA_NOPE].reshape(depth, MLA_Q_RANK, MLA_HEADS * MLA_NOPE),
                          uq[..., MLA_NOPE:].reshape(depth, MLA_Q_RANK, MLA_HEADS * MLA_ROPE)], axis=-1)
    ukv = mla_w_ukv.reshape(depth, MLA_KV_RANK, MLA_HEADS, MLA_NOPE + MLA_V_HD)
    ukt = ukv[..., :MLA_NOPE].reshape(depth, MLA_KV_RANK, MLA_HEADS * MLA_NOPE).transpose(0, 2, 1)
    uv = ukv[..., MLA_NOPE:].reshape(depth, MLA_KV_RANK, MLA_W)
    row = lambda a: a[:, None, :]
    w_main, w_lat, w_gate = _split_w_in(w_in)
    return {
        "norm_g": row(norm_g),
        "w_main": w_main,
        "w_lat": w_lat,
        "w_gate": w_gate,
        "w_uq": uq.astype(BF16),
        "w_ukt": ukt.astype(BF16),
        "w_uv": uv.astype(BF16),
        "qa_g": row(mla_qa_g),
        "kva_g": row(mla_kva_g),
        "conv_w": jnp.pad(conv_dw_w, ((0, 0), (0, 1), (0, 0))),
        "conv_b": row(conv_dw_b),
        "conv_ln_g": row(conv_ln_g),
        "conv_ln_b": row(conv_ln_b),
        "ret_decay": jnp.broadcast_to(ret_decay_logit[:, :, :, None, None], (depth, 2, RET_HEADS, SUBLANES, LANES)),
        "w_out": w_out.astype(BF16),
        "final_g": final_g[None, :],
    }


def _rope_inputs(seq):
    cos_r, sin_r = _rope_tables(seq, RET_HD)
    half = RET_HD // 2
    sin_r = np.concatenate([-sin_r[:, :half], sin_r[:, half:]], axis=1)
    cos_m, sin_m = _rope_tables(seq, MLA_ROPE)
    half = MLA_ROPE // 2
    zeros = np.zeros((seq, half), np.float32)
    cosm = np.concatenate([cos_m, cos_m], axis=1)
    sinma = np.concatenate([-sin_m[:, :half], zeros, -sin_m[:, :half], zeros], axis=1)
    sinmb = np.concatenate([zeros, sin_m[:, half:], zeros, sin_m[:, half:]], axis=1)
    return tuple(jnp.asarray(t) for t in (cos_r, sin_r, cosm, sinma, sinmb))


def kernel(x, norm_g, w_in, conv_dw_w, conv_dw_b, conv_ln_g, conv_ln_b, ret_decay_logit,
           mla_qa_g, mla_w_uq, mla_kva_g, mla_w_ukv, w_out, final_g):
    batch, seq, d_model = x.shape
    depth = norm_g.shape[0]
    assert d_model == D_MODEL and seq % RET_CHUNK == 0
    p = _prep_params(norm_g, w_in, conv_dw_w, conv_dw_b, conv_ln_g, conv_ln_b, ret_decay_logit,
                     mla_qa_g, mla_w_uq, mla_kva_g, mla_w_ukv, w_out, final_g)
    tabs = _rope_inputs(seq)
    x2 = x.reshape(batch * seq, d_model)
    for layer in range(depth):
        hglu, rqkv, mq, mkt, mv, gate = _inproj(x2, p, layer, tabs, seq)
        yc = _conv(hglu, gate, p, layer, batch, seq)
        yr = _retention(rqkv, gate, p, layer, batch, seq)
        ym = _mla(mq, mkt, mv, gate, batch, seq)
        x2 = _outproj(x2, yc, yr, ym, p, layer, seq, final=(layer == depth - 1))
    return x2.reshape(batch, seq, d_model)
```

```python
import functools
import math

import jax
import jax.numpy as jnp
import numpy as np
from jax import lax
from jax.experimental import pallas as pl
from jax.experimental.pallas import tpu as pltpu

D_MODEL = 1024
D_MIX = 2 * D_MODEL
CONV_W = 512
CONV_K = 31
RET_W = 512
RET_HEADS = 4
RET_HD = 128
RET_CHUNK = 128
MLA_W = 1024
MLA_HEADS = 8
MLA_V_HD = 128
MLA_NOPE = 128
MLA_ROPE = 64
MLA_Q_RANK = 384
MLA_KV_RANK = 256
ROPE_BASE = 10000.0
EPS = 1e-6

OFF_RET = 2 * CONV_W
OFF_QLAT = OFF_RET + 3 * RET_W
OFF_KVLAT = OFF_QLAT + MLA_Q_RANK
OFF_KROPE = OFF_KVLAT + MLA_KV_RANK
OFF_GATE = OFF_KROPE + MLA_ROPE
N_IN = OFF_GATE + D_MIX
W_MAIN = OFF_QLAT + MLA_KV_RANK

LANES = 128
SUBLANES = 8
MLA_QK_PAD = 2 * LANES
CONV_HALO = 16
CONV_HOP = 128
CONV_WIN = CONV_HOP + 2 * CONV_HALO
VMEM_LIMIT = 56 * 1024 * 1024

BF16 = jnp.bfloat16
F32 = jnp.float32


def _tiles(seq):
    return min(512, seq), 256, min(2048, seq)


def _rope_tables(seq, dim):
    f32 = np.float32
    inv = f32(1.0) / (f32(ROPE_BASE) ** (np.arange(0, dim, 2, dtype=f32) / f32(dim)))
    ang = np.arange(seq, dtype=f32)[:, None] * inv[None, :]
    ang = np.concatenate([ang, ang], axis=-1).astype(np.float64)
    return np.cos(ang).astype(f32), np.sin(ang).astype(f32)


def _rmsnorm(x, g):
    return x * lax.rsqrt(jnp.mean(x * x, axis=-1, keepdims=True) + EPS) * g


def _layer_block(layer, shape):
    return pl.BlockSpec((None,) + shape, lambda *_: (layer,) + (0,) * len(shape), pipeline_mode=pl.Buffered(1))


def _inproj_kernel(x_ref, g_ref, wmain_ref, wlat_ref, wgate_ref, wuq_ref, wukt_ref, wuv_ref, qag_ref, kvag_ref,
                   cosr_ref, sinr_ref, cosm_ref, sinma_ref, sinmb_ref,
                   hglu_ref, rqkv_ref, mq_ref, mkt_ref, mv_ref, gate_ref, *, sub):
    for r in range(x_ref.shape[0] // sub):
        _inproj_rows(slice(r * sub, (r + 1) * sub), x_ref, g_ref, wmain_ref, wlat_ref, wgate_ref,
                     wuq_ref, wukt_ref, wuv_ref, qag_ref, kvag_ref,
                     cosr_ref, sinr_ref, cosm_ref, sinma_ref, sinmb_ref,
                     hglu_ref, rqkv_ref, mq_ref, mkt_ref, mv_ref, gate_ref)


def _inproj_rows(rows, x_ref, g_ref, wmain_ref, wlat_ref, wgate_ref, wuq_ref, wukt_ref, wuv_ref, qag_ref, kvag_ref,
                 cosr_ref, sinr_ref, cosm_ref, sinma_ref, sinmb_ref,
                 hglu_ref, rqkv_ref, mq_ref, mkt_ref, mv_ref, gate_ref):
    h = _rmsnorm(x_ref[rows, :], g_ref[...]).astype(BF16)

    lat = jnp.dot(h, wlat_ref[...], preferred_element_type=F32)
    q_lat, k_rope = lat[:, :MLA_Q_RANK], lat[:, MLA_Q_RANK:]

    gt = jnp.dot(h, wgate_ref[...], preferred_element_type=F32)
    gate_ref[rows, :] = (gt * jax.nn.sigmoid(gt)).astype(BF16)

    u_main = jnp.dot(h, wmain_ref[...], preferred_element_type=F32)

    hglu_ref[rows, :] = (u_main[:, :CONV_W] * jax.nn.sigmoid(u_main[:, CONV_W:2 * CONV_W])).astype(BF16)

    cosr = cosr_ref[rows, :]
    sinr = sinr_ref[rows, :]
    k_scale = RET_HD ** -0.5
    for part in range(3):
        for hd in range(RET_HEADS):
            col = part * RET_W + hd * RET_HD
            blk = u_main[:, OFF_RET + col:OFF_RET + col + RET_HD]
            if part < 2:
                blk = blk * cosr + pltpu.roll(blk, RET_HD // 2, 1) * sinr
            if part == 1:
                blk = blk * k_scale
            rqkv_ref[rows, col:col + RET_HD] = blk.astype(BF16)

    cosm = cosm_ref[rows, :]
    sinma = sinma_ref[rows, :]
    sinmb = sinmb_ref[rows, :]

    def rope_pair(blk):
        return (blk * cosm + pltpu.roll(blk, LANES - MLA_ROPE // 2, 1) * sinma
                + pltpu.roll(blk, MLA_ROPE // 2, 1) * sinmb)

    qn = _rmsnorm(q_lat, qag_ref[...]).astype(BF16)
    q_scale = (MLA_NOPE + MLA_ROPE) ** -0.5 * math.log2(math.e)
    q_all = jnp.dot(qn, wuq_ref[...], preferred_element_type=F32) * q_scale
    rope_off = MLA_HEADS * MLA_NOPE
    lane = lax.broadcasted_iota(jnp.int32, (q_all.shape[0], LANES), 1)
    for pair in range(MLA_HEADS // 2):
        roped = rope_pair(q_all[:, rope_off + pair * LANES:rope_off + (pair + 1) * LANES])
        for j in range(2):
            hd = 2 * pair + j
            own = (lane < MLA_ROPE) if j == 0 else (lane >= MLA_ROPE)
            mq_ref[rows, hd * MLA_QK_PAD:hd * MLA_QK_PAD + LANES] = q_all[:, hd * MLA_NOPE:(hd + 1) * MLA_NOPE].astype(BF16)
            mq_ref[rows, hd * MLA_QK_PAD + LANES:(hd + 1) * MLA_QK_PAD] = jnp.where(own, roped, 0.0).astype(BF16)

    kvn = _rmsnorm(u_main[:, OFF_QLAT:W_MAIN], kvag_ref[...]).astype(BF16)
    k_rope_t = rope_pair(k_rope).T.astype(BF16)
    k_nope_t = lax.dot_general(wukt_ref[...], kvn, (((1,), (1,)), ((), ())),
                               preferred_element_type=F32).astype(BF16)
    for hd in range(MLA_HEADS):
        mkt_ref[hd * MLA_QK_PAD:hd * MLA_QK_PAD + LANES, rows] = k_nope_t[hd * MLA_NOPE:(hd + 1) * MLA_NOPE, :]
        mkt_ref[hd * MLA_QK_PAD + LANES:(hd + 1) * MLA_QK_PAD, rows] = k_rope_t
    mv_ref[rows, :] = jnp.dot(kvn, wuv_ref[...], preferred_element_type=F32).astype(BF16)


def _inproj(x2, p, layer, tabs, seq):
    tm, sub, _ = _tiles(seq)
    tokens = x2.shape[0]
    nseq = seq // tm
    row = lambda i: (i, 0)
    pos = lambda i: (i % nseq, 0)
    return pl.pallas_call(
        functools.partial(_inproj_kernel, sub=sub),
        grid=(tokens // tm,),
        in_specs=[
            pl.BlockSpec((tm, D_MODEL), row),
            _layer_block(layer, (1, D_MODEL)),
            _layer_block(layer, (D_MODEL, W_MAIN)),
            _layer_block(layer, (D_MODEL, MLA_Q_RANK + LANES)),
            _layer_block(layer, (D_MODEL, D_MIX)),
            _layer_block(layer, (MLA_Q_RANK, MLA_HEADS * (MLA_NOPE + MLA_ROPE))),
            _layer_block(layer, (MLA_HEADS * MLA_NOPE, MLA_KV_RANK)),
            _layer_block(layer, (MLA_KV_RANK, MLA_W)),
            _layer_block(layer, (1, MLA_Q_RANK)),
            _layer_block(layer, (1, MLA_KV_RANK)),
        ] + [pl.BlockSpec((tm, LANES), pos)] * len(tabs),
        out_specs=[
            pl.BlockSpec((tm, CONV_W), row),
            pl.BlockSpec((tm, 3 * RET_W), row),
            pl.BlockSpec((tm, MLA_HEADS * MLA_QK_PAD), row),
            pl.BlockSpec((MLA_HEADS * MLA_QK_PAD, tm), lambda i: (0, i)),
            pl.BlockSpec((tm, MLA_W), row),
            pl.BlockSpec((tm, D_MIX), row),
        ],
        out_shape=[
            jax.ShapeDtypeStruct((tokens, CONV_W), BF16),
            jax.ShapeDtypeStruct((tokens, 3 * RET_W), BF16),
            jax.ShapeDtypeStruct((tokens, MLA_HEADS * MLA_QK_PAD), BF16),
            jax.ShapeDtypeStruct((MLA_HEADS * MLA_QK_PAD, tokens), BF16),
            jax.ShapeDtypeStruct((tokens, MLA_W), BF16),
            jax.ShapeDtypeStruct((tokens, D_MIX), BF16),
        ],
        compiler_params=pltpu.CompilerParams(
            dimension_semantics=("arbitrary",), vmem_limit_bytes=VMEM_LIMIT),
        name="inproj",
    )(x2, p["norm_g"], p["w_main"], p["w_lat"], p["w_gate"], p["w_uq"], p["w_ukt"], p["w_uv"],
      p["qa_g"], p["kva_g"], *tabs)


def _conv_tables():
    n = CONV_WIN
    k = np.arange(n)
    ang = 2.0 * np.pi * (np.outer(k, k) % n) / n
    hart = np.cos(ang) + np.sin(ang)
    both = np.concatenate([hart, hart[(-k) % n]], axis=0)

    def split(m):
        hi = m.astype(BF16)
        return hi, (m - hi.astype(np.float64)).astype(BF16)

    fwd_hi, fwd_lo = split(both)
    inv_hi, inv_lo = split(hart[:CONV_HOP])
    inv = np.concatenate([inv_hi, inv_lo, inv_hi], axis=1)
    taps = np.zeros((2 * n, CONV_K + 1), np.float32)
    taps[:, :CONV_K] = both[:, n - 1 - np.arange(CONV_K)]
    return jnp.asarray(fwd_hi), jnp.asarray(fwd_lo), jnp.asarray(inv), jnp.asarray(taps)


def _conv_kernel(h_ref, gate_ref, w_ref, b_ref, lg_ref, lb_ref, fwd_hi_ref, fwd_lo_ref, inv_ref, taps_ref, o_ref,
                 pad_ref, ge_ref, go_ref, *, seq):
    n = CONV_WIN
    zeros = jnp.zeros((CONV_HALO, CONV_W), BF16)
    pad_ref[0:CONV_HALO, :] = zeros
    pad_ref[CONV_HALO + seq:, :] = zeros
    pad_ref[CONV_HALO:CONV_HALO + seq, :] = h_ref[...]

    @pl.when(pl.program_id(0) == 0)
    def _():
        g = jnp.dot(taps_ref[...], w_ref[...], preferred_element_type=F32, precision=lax.Precision.HIGHEST)
        ge_ref[...] = (g[:n] + g[n:]) * (0.5 / n)
        go_ref[...] = (g[:n] - g[n:]) * (0.5 / n)

    def forward(i):
        win = pad_ref[i * CONV_HOP:i * CONV_HOP + n, :]
        return (jnp.dot(fwd_hi_ref[...], win, preferred_element_type=F32)
                + jnp.dot(fwd_lo_ref[...], win, preferred_element_type=F32))

    n_win = seq // CONV_HOP
    xs_next = forward(0)
    for i in range(n_win):
        xs = xs_next
        if i + 1 < n_win:
            xs_next = forward(i + 1)
        z = xs[:n] * ge_ref[...] + xs[n:] * go_ref[...]
        z_hi = z.astype(BF16)
        z_lo = (z - z_hi.astype(F32)).astype(BF16)
        acc = jnp.dot(inv_ref[...], jnp.concatenate([z_hi, z_hi, z_lo], axis=0), preferred_element_type=F32)
        acc = acc + b_ref[...]
        mu = jnp.mean(acc, axis=-1, keepdims=True)
        cen = acc - mu
        var = jnp.mean(cen * cen, axis=-1, keepdims=True)
        hn = cen * lax.rsqrt(var + EPS) * lg_ref[...] + lb_ref[...]
        out_rows = slice(i * CONV_HOP, (i + 1) * CONV_HOP)
        o_ref[out_rows, :] = (hn * jax.nn.sigmoid(hn) * gate_ref[out_rows, :].astype(F32)).astype(o_ref.dtype)


def _conv(hglu, gate, p, layer, batch, seq):
    assert seq % CONV_HOP == 0
    tables = _conv_tables()
    whole = lambda a: pl.BlockSpec(a.shape, lambda i: (0,) * a.ndim, pipeline_mode=pl.Buffered(1))
    return pl.pallas_call(
        functools.partial(_conv_kernel, seq=seq),
        grid=(batch,),
        in_specs=[
            pl.BlockSpec((seq, CONV_W), lambda i: (i, 0)),
            pl.BlockSpec((seq, CONV_W), lambda i: (i, 0)),
            _layer_block(layer, (CONV_K + 1, CONV_W)),
            _layer_block(layer, (1, CONV_W)),
            _layer_block(layer, (1, CONV_W)),
            _layer_block(layer, (1, CONV_W)),
        ] + [whole(t) for t in tables],
        out_specs=pl.BlockSpec((seq, CONV_W), lambda i: (i, 0)),
        out_shape=jax.ShapeDtypeStruct(hglu.shape, BF16),
        scratch_shapes=[
            pltpu.VMEM((seq + 2 * CONV_HALO, CONV_W), BF16),
            pltpu.VMEM((CONV_WIN, CONV_W), F32),
            pltpu.VMEM((CONV_WIN, CONV_W), F32),
        ],
        compiler_params=pltpu.CompilerParams(dimension_semantics=("arbitrary",)),
        name="conv",
    )(hglu, gate, p["conv_w"], p["conv_b"], p["conv_ln_g"], p["conv_ln_b"], *tables)


def _log_sigmoid(x):
    return jnp.minimum(x, 0.0) - jnp.log1p(jnp.exp(-jnp.abs(x)))


def _ret_kernel(dl_ref, q_ref, k_ref, v_ref, gate_ref, o_ref, decay_ref, sd_ref, kv_ref, st_ref, *, seq, chunk):
    C = chunk
    n_chunks = seq // C
    D = RET_HD
    lg_f = jnp.broadcast_to(_log_sigmoid(dl_ref[0, 0])[0:1, :], (C, LANES))
    lg_b = jnp.broadcast_to(_log_sigmoid(dl_ref[1, 0])[0:1, :], (C, LANES))
    ri = lax.broadcasted_iota(jnp.int32, (C, LANES), 0)
    idx = ri.astype(F32)
    for c in range(C // LANES):
        diff = (ri - (lax.broadcasted_iota(jnp.int32, (C, LANES), 1) + c * LANES)).astype(F32)
        decay_ref[:, c * LANES:(c + 1) * LANES] = jnp.where(
            diff >= 0.0, jnp.exp(lg_f * jnp.maximum(diff, 0.0)), jnp.exp(lg_b * jnp.maximum(-diff, 0.0)))
    qdec_f = jnp.exp(lg_f * (idx + 1.0))
    kdec_f = jnp.exp(lg_f * (C - 1.0 - idx))
    qdec_b = jnp.exp(lg_b * (C - idx))
    kdec_b = jnp.exp(lg_b * idx)
    cdec_f = jnp.exp(lg_f[:D] * float(C))
    cdec_b = jnp.exp(lg_b[:D] * float(C))
    trans_b = (((1,), (1,)), ((), ()))
    trans_a = (((0,), (0,)), ((), ()))

    def rows(n):
        return pl.ds(n * C, C)

    for n in range(n_chunks):
        qn, kn, vn = q_ref[rows(n), :], k_ref[rows(n), :], v_ref[rows(n), :]
        s = lax.dot_general(qn, kn, trans_b, preferred_element_type=F32)
        sd_ref[n] = (s * decay_ref[...]).astype(BF16)
        knf = kn.astype(F32)
        kv_ref[0, n] = lax.dot_general((knf * kdec_f).astype(BF16), vn, trans_a, preferred_element_type=F32)
        kv_ref[1, n] = lax.dot_general((knf * kdec_b).astype(BF16), vn, trans_a, preferred_element_type=F32)

    state = jnp.zeros((D, D), F32)
    for n in range(n_chunks):
        st_ref[0, n] = state.astype(BF16)
        state = cdec_f * state + kv_ref[0, n]
    state = jnp.zeros((D, D), F32)
    for n in reversed(range(n_chunks)):
        st_ref[1, n] = state.astype(BF16)
        state = cdec_b * state + kv_ref[1, n]

    for n in range(n_chunks):
        qnf = q_ref[rows(n), :].astype(F32)
        lhs = jnp.concatenate([sd_ref[n], (qnf * qdec_f).astype(BF16), (qnf * qdec_b).astype(BF16)], axis=1)
        rhs = jnp.concatenate([v_ref[rows(n), :], st_ref[0, n], st_ref[1, n]], axis=0)
        out = jnp.dot(lhs, rhs, preferred_element_type=F32)
        mu = jnp.mean(out, axis=-1, keepdims=True)
        cen = out - mu
        var = jnp.mean(cen * cen, axis=-1, keepdims=True)
        o_ref[rows(n), :] = (cen * lax.rsqrt(var + EPS) * gate_ref[rows(n), :].astype(F32)).astype(o_ref.dtype)


def _retention(rqkv, gate, p, layer, batch, seq):
    chunk = 256 if seq % 256 == 0 else RET_CHUNK
    n_chunks = seq // chunk
    blk = lambda part: pl.BlockSpec((seq, RET_HD), lambda b, h: (b, part * RET_HEADS + h))
    return pl.pallas_call(
        functools.partial(_ret_kernel, seq=seq, chunk=chunk),
        grid=(batch, RET_HEADS),
        in_specs=[
            pl.BlockSpec((None, 2, 1, SUBLANES, LANES), lambda b, h: (layer, 0, h, 0, 0)),
            blk(0), blk(1), blk(2),
            pl.BlockSpec((seq, RET_HD), lambda b, h: (b, CONV_W // RET_HD + h)),
        ],
        out_specs=pl.BlockSpec((seq, RET_HD), lambda b, h: (b, h)),
        out_shape=jax.ShapeDtypeStruct((batch * seq, RET_W), BF16),
        scratch_shapes=[
            pltpu.VMEM((chunk, chunk), F32),
            pltpu.VMEM((n_chunks, chunk, chunk), BF16),
            pltpu.VMEM((2, n_chunks, RET_HD, RET_HD), F32),
            pltpu.VMEM((2, n_chunks, RET_HD, RET_HD), BF16),
        ],
        compiler_params=pltpu.CompilerParams(dimension_semantics=("arbitrary", "arbitrary")),
        name="retention",
    )(p["ret_decay"], rqkv, rqkv, rqkv, gate)


def _mla_kernel(q_ref, kt_ref, v_ref, gate_ref, o_ref, *, sub):
    kt = kt_ref[...]
    v = jnp.concatenate([v_ref[...], jnp.ones(v_ref.shape, BF16)], axis=1)
    n_sub = q_ref.shape[0] // sub

    def scores(r):
        return jnp.dot(q_ref[r * sub:(r + 1) * sub, :], kt, preferred_element_type=F32)

    s_next = scores(0)
    for r in range(n_sub):
        rows = slice(r * sub, (r + 1) * sub)
        s = s_next
        if r + 1 < n_sub:
            s_next = scores(r + 1)
        m = jnp.max(s, axis=-1, keepdims=True)
        p = jnp.exp2(s - m)
        o = jnp.dot(p.astype(BF16), v, preferred_element_type=F32)
        o_ref[rows, :] = (o[:, :MLA_V_HD] / o[:, MLA_V_HD:] * gate_ref[rows, :].astype(F32)).astype(o_ref.dtype)


def _mla(mq, mkt, mv, gate, batch, seq):
    _, sub, tq = _tiles(seq)
    nq = seq // tq
    return pl.pallas_call(
        functools.partial(_mla_kernel, sub=sub),
        grid=(batch, MLA_HEADS, nq),
        in_specs=[
            pl.BlockSpec((tq, MLA_QK_PAD), lambda b, h, i: (b * nq + i, h)),
            pl.BlockSpec((MLA_QK_PAD, seq), lambda b, h, i: (h, b)),
            pl.BlockSpec((seq, MLA_V_HD), lambda b, h, i: (b, h)),
            pl.BlockSpec((tq, MLA_V_HD), lambda b, h, i: (b * nq + i, (CONV_W + RET_W) // MLA_V_HD + h)),
        ],
        out_specs=pl.BlockSpec((tq, MLA_V_HD), lambda b, h, i: (b * nq + i, h)),
        out_shape=jax.ShapeDtypeStruct((batch * seq, MLA_W), BF16),
        compiler_params=pltpu.CompilerParams(
            dimension_semantics=("arbitrary", "arbitrary", "arbitrary"), vmem_limit_bytes=VMEM_LIMIT),
        name="mla_attention",
    )(mq, mkt, mv, gate)


def _outproj_kernel(x_ref, yc_ref, yr_ref, ym_ref, w_ref, fg_ref, o_ref, *, final):
    acc = x_ref[...]
    acc = acc + jnp.dot(yc_ref[...], w_ref[0:CONV_W, :], preferred_element_type=F32)
    acc = acc + jnp.dot(yr_ref[...], w_ref[CONV_W:CONV_W + RET_W, :], preferred_element_type=F32)
    acc = acc + jnp.dot(ym_ref[...], w_ref[CONV_W + RET_W:, :], preferred_element_type=F32)
    if final:
        acc = _rmsnorm(acc, fg_ref[...])
    o_ref[...] = acc


def _outproj(x2, yc, yr, ym, p, layer, seq, final):
    tm = min(2 * _tiles(seq)[0], seq)
    tokens = x2.shape[0]
    row = lambda i: (i, 0)
    return pl.pallas_call(
        functools.partial(_outproj_kernel, final=final),
        grid=(tokens // tm,),
        in_specs=[
            pl.BlockSpec((tm, D_MODEL), row),
            pl.BlockSpec((tm, CONV_W), row),
            pl.BlockSpec((tm, RET_W), row),
            pl.BlockSpec((tm, MLA_W), row),
            _layer_block(layer, (D_MIX, D_MODEL)),
            pl.BlockSpec((1, D_MODEL), lambda i: (0, 0)),
        ],
        out_specs=pl.BlockSpec((tm, D_MODEL), row),
        out_shape=jax.ShapeDtypeStruct(x2.shape, F32),
        compiler_params=pltpu.CompilerParams(
            dimension_semantics=("arbitrary",), vmem_limit_bytes=VMEM_LIMIT),
        name="outproj",
    )(x2, yc, yr, ym, p["w_out"], p["final_g"])


def _pad_last(w, width):
    return jnp.pad(w, [(0, 0)] * (w.ndim - 1) + [(0, width - w.shape[-1])])


def _split_w_in_kernel(wt_ref, main_ref, lat_ref, gate_ref):
    wt = wt_ref[...]
    k_rope = wt[OFF_KROPE:OFF_GATE, :]
    main_ref[...] = jnp.concatenate([wt[:OFF_QLAT, :].T, wt[OFF_KVLAT:OFF_KROPE, :].T], axis=1).astype(BF16)
    lat_ref[...] = jnp.concatenate([wt[OFF_QLAT:OFF_KVLAT, :].T,
                                    jnp.concatenate([k_rope, k_rope], axis=0).T], axis=1).astype(BF16)
    gate_ref[...] = wt[OFF_GATE:, :].T.astype(BF16)


def _split_w_in(w_in):
    depth = w_in.shape[0]
    rows = 256
    widths = (W_MAIN, MLA_Q_RANK + LANES, D_MIX)
    return pl.pallas_call(
        _split_w_in_kernel,
        grid=(depth, D_MODEL // rows),
        in_specs=[pl.BlockSpec((None, N_IN, rows), lambda l, i: (l, 0, i))],
        out_specs=[pl.BlockSpec((None, rows, w), lambda l, i: (l, i, 0)) for w in widths],
        out_shape=[jax.ShapeDtypeStruct((depth, D_MODEL, w), BF16) for w in widths],
        compiler_params=pltpu.CompilerParams(dimension_semantics=("arbitrary", "arbitrary")),
        name="split_w_in",
    )(jnp.swapaxes(w_in, 1, 2))


def _prep_params(norm_g, w_in, conv_dw_w, conv_dw_b, conv_ln_g, conv_ln_b, ret_decay_logit,
                 mla_qa_g, mla_w_uq, mla_kva_g, mla_w_ukv, w_out, final_g):
    depth = norm_g.shape[0]
    uq = mla_w_uq.reshape(depth, MLA_Q_RANK, MLA_HEADS, MLA_NOPE + MLA_ROPE)
    uq = jnp.concatenate([uq[..., :MLA_NOPE].reshape(depth, MLA_Q_RANK, MLA_HEADS * MLA_NOPE),
                          uq[..., MLA_NOPE:].reshape(depth, MLA_Q_RANK, MLA_HEADS * MLA_ROPE)], axis=-1)
    ukv = mla_w_ukv.reshape(depth, MLA_KV_RANK, MLA_HEADS, MLA_NOPE + MLA_V_HD)
    ukt = ukv[..., :MLA_NOPE].reshape(depth, MLA_KV_RANK, MLA_HEADS * MLA_NOPE).transpose(0, 2, 1)
    uv = ukv[..., MLA_NOPE:].reshape(depth, MLA_KV_RANK, MLA_W)
    row = lambda a: a[:, None, :]
    w_main, w_lat, w_gate = _split_w_in(w_in)
    return {
        "norm_g": row(norm_g),
        "w_main": w_main,
        "w_lat": w_lat,
        "w_gate": w_gate,
        "w_uq": uq.astype(BF16),
        "w_ukt": ukt.astype(BF16),
        "w_uv": uv.astype(BF16),
        "qa_g": row(mla_qa_g),
        "kva_g": row(mla_kva_g),
        "conv_w": jnp.pad(conv_dw_w, ((0, 0), (0, 1), (0, 0))),
        "conv_b": row(conv_dw_b),
        "conv_ln_g": row(conv_ln_g),
        "conv_ln_b": row(conv_ln_b),
        "ret_decay": jnp.broadcast_to(ret_decay_logit[:, :, :, None, None], (depth, 2, RET_HEADS, SUBLANES, LANES)),
        "w_out": w_out.astype(BF16),
        "final_g": final_g[None, :],
    }


def _rope_inputs(seq):
    cos_r, sin_r = _rope_tables(seq, RET_HD)
    half = RET_HD // 2
    sin_r = np.concatenate([-sin_r[:, :half], sin_r[:, half:]], axis=1)
    cos_m, sin_m = _rope_tables(seq, MLA_ROPE)
    half = MLA_ROPE // 2
    zeros = np.zeros((seq, half), np.float32)
    cosm = np.concatenate([cos_m, cos_m], axis=1)
    sinma = np.concatenate([-sin_m[:, :half], zeros, -sin_m[:, :half], zeros], axis=1)
    sinmb = np.concatenate([zeros, sin_m[:, half:], zeros, sin_m[:, half:]], axis=1)
    return tuple(jnp.asarray(t) for t in (cos_r, sin_r, cosm, sinma, sinmb))


def kernel(x, norm_g, w_in, conv_dw_w, conv_dw_b, conv_ln_g, conv_ln_b, ret_decay_logit,
           mla_qa_g, mla_w_uq, mla_kva_g, mla_w_ukv, w_out, final_g):
    batch, seq, d_model = x.shape
    depth = norm_g.shape[0]
    assert d_model == D_MODEL and seq % RET_CHUNK == 0
    p = _prep_params(norm_g, w_in, conv_dw_w, conv_dw_b, conv_ln_g, conv_ln_b, ret_decay_logit,
                     mla_qa_g, mla_w_uq, mla_kva_g, mla_w_ukv, w_out, final_g)
    tabs = _rope_inputs(seq)
    x2 = x.reshape(batch * seq, d_model)
    for layer in range(depth):
        hglu, rqkv, mq, mkt, mv, gate = _inproj(x2, p, layer, tabs, seq)
        yc = _conv(hglu, gate, p, layer, batch, seq)
        yr = _retention(rqkv, gate, p, layer, batch, seq)
        ym = _mla(mq, mkt, mv, gate, batch, seq)
        x2 = _outproj(x2, yc, yr, ym, p, layer, seq, final=(layer == depth - 1))
    return x2.reshape(batch, seq, d_model)
```

```python
import functools
import math

import jax
import jax.numpy as jnp
import numpy as np
from jax import lax
from jax.experimental import pallas as pl
from jax.experimental.pallas import tpu as pltpu

D_MODEL = 1024
D_MIX = 2 * D_MODEL
CONV_W = 512
CONV_K = 31
RET_W = 512
RET_HEADS = 4
RET_HD = 128
RET_CHUNK = 128
MLA_W = 1024
MLA_HEADS = 8
MLA_V_HD = 128
MLA_NOPE = 128
MLA_ROPE = 64
MLA_Q_RANK = 384
MLA_KV_RANK = 256
ROPE_BASE = 10000.0
EPS = 1e-6

OFF_RET = 2 * CONV_W
OFF_QLAT = OFF_RET + 3 * RET_W
OFF_KVLAT = OFF_QLAT + MLA_Q_RANK
OFF_KROPE = OFF_KVLAT + MLA_KV_RANK
OFF_GATE = OFF_KROPE + MLA_ROPE
N_IN = OFF_GATE + D_MIX
W_MAIN = OFF_QLAT + MLA_KV_RANK

LANES = 128
SUBLANES = 8
MLA_QK_PAD = 2 * LANES
CONV_HALO = 16
CONV_HOP = 128
CONV_WIN = CONV_HOP + 2 * CONV_HALO
VMEM_LIMIT = 56 * 1024 * 1024

BF16 = jnp.bfloat16
F32 = jnp.float32


def _tiles(seq):
    return min(512, seq), 256, min(2048, seq)


def _rope_tables(seq, dim):
    f32 = np.float32
    inv = f32(1.0) / (f32(ROPE_BASE) ** (np.arange(0, dim, 2, dtype=f32) / f32(dim)))
    ang = np.arange(seq, dtype=f32)[:, None] * inv[None, :]
    ang = np.concatenate([ang, ang], axis=-1).astype(np.float64)
    return np.cos(ang).astype(f32), np.sin(ang).astype(f32)


def _rmsnorm(x, g):
    return x * lax.rsqrt(jnp.mean(x * x, axis=-1, keepdims=True) + EPS) * g


def _layer_block(layer, shape):
    return pl.BlockSpec((None,) + shape, lambda *_: (layer,) + (0,) * len(shape), pipeline_mode=pl.Buffered(1))


def _inproj_kernel(x_ref, g_ref, wmain_ref, wlat_ref, wgate_ref, wuq_ref, wukt_ref, wuv_ref, qag_ref, kvag_ref,
                   cosr_ref, sinr_ref, cosm_ref, sinma_ref, sinmb_ref,
                   hglu_ref, rqkv_ref, mq_ref, mkt_ref, mv_ref, gate_ref, *, sub):
    for r in range(x_ref.shape[0] // sub):
        _inproj_rows(slice(r * sub, (r + 1) * sub), x_ref, g_ref, wmain_ref, wlat_ref, wgate_ref,
                     wuq_ref, wukt_ref, wuv_ref, qag_ref, kvag_ref,
                     cosr_ref, sinr_ref, cosm_ref, sinma_ref, sinmb_ref,
                     hglu_ref, rqkv_ref, mq_ref, mkt_ref, mv_ref, gate_ref)


def _inproj_rows(rows, x_ref, g_ref, wmain_ref, wlat_ref, wgate_ref, wuq_ref, wukt_ref, wuv_ref, qag_ref, kvag_ref,
                 cosr_ref, sinr_ref, cosm_ref, sinma_ref, sinmb_ref,
                 hglu_ref, rqkv_ref, mq_ref, mkt_ref, mv_ref, gate_ref):
    h = _rmsnorm(x_ref[rows, :], g_ref[...]).astype(BF16)

    lat = jnp.dot(h, wlat_ref[...], preferred_element_type=F32)
    q_lat, k_rope = lat[:, :MLA_Q_RANK], lat[:, MLA_Q_RANK:]

    gt = jnp.dot(h, wgate_ref[...], preferred_element_type=F32)
    gate_ref[rows, :] = (gt * jax.nn.sigmoid(gt)).astype(BF16)

    u_main = jnp.dot(h, wmain_ref[...], preferred_element_type=F32)

    hglu_ref[rows, :] = (u_main[:, :CONV_W] * jax.nn.sigmoid(u_main[:, CONV_W:2 * CONV_W])).astype(BF16)

    cosr = cosr_ref[rows, :]
    sinr = sinr_ref[rows, :]
    k_scale = RET_HD ** -0.5
    for part in range(3):
        for hd in range(RET_HEADS):
            col = part * RET_W + hd * RET_HD
            blk = u_main[:, OFF_RET + col:OFF_RET + col + RET_HD]
            if part < 2:
                blk = blk * cosr + pltpu.roll(blk, RET_HD // 2, 1) * sinr
            if part == 1:
                blk = blk * k_scale
            rqkv_ref[rows, col:col + RET_HD] = blk.astype(BF16)

    cosm = cosm_ref[rows, :]
    sinma = sinma_ref[rows, :]
    sinmb = sinmb_ref[rows, :]

    def rope_pair(blk):
        return (blk * cosm + pltpu.roll(blk, LANES - MLA_ROPE // 2, 1) * sinma
                + pltpu.roll(blk, MLA_ROPE // 2, 1) * sinmb)

    qn = _rmsnorm(q_lat, qag_ref[...]).astype(BF16)
    q_scale = (MLA_NOPE + MLA_ROPE) ** -0.5 * math.log2(math.e)
    q_all = jnp.dot(qn, wuq_ref[...], preferred_element_type=F32) * q_scale
    rope_off = MLA_HEADS * MLA_NOPE
    lane = lax.broadcasted_iota(jnp.int32, (q_all.shape[0], LANES), 1)
    for pair in range(MLA_HEADS // 2):
        roped = rope_pair(q_all[:, rope_off + pair * LANES:rope_off + (pair + 1) * LANES])
        for j in range(2):
            hd = 2 * pair + j
            own = (lane < MLA_ROPE) if j == 0 else (lane >= MLA_ROPE)
            mq_ref[rows, hd * MLA_QK_PAD:hd * MLA_QK_PAD + LANES] = q_all[:, hd * MLA_NOPE:(hd + 1) * MLA_NOPE].astype(BF16)
            mq_ref[rows, hd * MLA_QK_PAD + LANES:(hd + 1) * MLA_QK_PAD] = jnp.where(own, roped, 0.0).astype(BF16)

    kvn = _rmsnorm(u_main[:, OFF_QLAT:W_MAIN], kvag_ref[...]).astype(BF16)
    k_rope_t = rope_pair(k_rope).T.astype(BF16)
    k_nope_t = lax.dot_general(wukt_ref[...], kvn, (((1,), (1,)), ((), ())),
                               preferred_element_type=F32).astype(BF16)
    for hd in range(MLA_HEADS):
        mkt_ref[hd * MLA_QK_PAD:hd * MLA_QK_PAD + LANES, rows] = k_nope_t[hd * MLA_NOPE:(hd + 1) * MLA_NOPE, :]
        mkt_ref[hd * MLA_QK_PAD + LANES:(hd + 1) * MLA_QK_PAD, rows] = k_rope_t
    mv_ref[rows, :] = jnp.dot(kvn, wuv_ref[...], preferred_element_type=F32).astype(BF16)


def _inproj(x2, p, layer, tabs, seq):
    tm, sub, _ = _tiles(seq)
    tokens = x2.shape[0]
    nseq = seq // tm
    row = lambda i: (i, 0)
    pos = lambda i: (i % nseq, 0)
    return pl.pallas_call(
        functools.partial(_inproj_kernel, sub=sub),
        grid=(tokens // tm,),
        in_specs=[
            pl.BlockSpec((tm, D_MODEL), row),
            _layer_block(layer, (1, D_MODEL)),
            _layer_block(layer, (D_MODEL, W_MAIN)),
            _layer_block(layer, (D_MODEL, MLA_Q_RANK + LANES)),
            _layer_block(layer, (D_MODEL, D_MIX)),
            _layer_block(layer, (MLA_Q_RANK, MLA_HEADS * (MLA_NOPE + MLA_ROPE))),
            _layer_block(layer, (MLA_HEADS * MLA_NOPE, MLA_KV_RANK)),
            _layer_block(layer, (MLA_KV_RANK, MLA_W)),
            _layer_block(layer, (1, MLA_Q_RANK)),
            _layer_block(layer, (1, MLA_KV_RANK)),
        ] + [pl.BlockSpec((tm, LANES), pos)] * len(tabs),
        out_specs=[
            pl.BlockSpec((tm, CONV_W), row),
            pl.BlockSpec((tm, 3 * RET_W), row),
            pl.BlockSpec((tm, MLA_HEADS * MLA_QK_PAD), row),
            pl.BlockSpec((MLA_HEADS * MLA_QK_PAD, tm), lambda i: (0, i)),
            pl.BlockSpec((tm, MLA_W), row),
            pl.BlockSpec((tm, D_MIX), row),
        ],
        out_shape=[
            jax.ShapeDtypeStruct((tokens, CONV_W), BF16),
            jax.ShapeDtypeStruct((tokens, 3 * RET_W), BF16),
            jax.ShapeDtypeStruct((tokens, MLA_HEADS * MLA_QK_PAD), BF16),
            jax.ShapeDtypeStruct((MLA_HEADS * MLA_QK_PAD, tokens), BF16),
            jax.ShapeDtypeStruct((tokens, MLA_W), BF16),
            jax.ShapeDtypeStruct((tokens, D_MIX), BF16),
        ],
        compiler_params=pltpu.CompilerParams(
            dimension_semantics=("arbitrary",), vmem_limit_bytes=VMEM_LIMIT),
        name="inproj",
    )(x2, p["norm_g"], p["w_main"], p["w_lat"], p["w_gate"], p["w_uq"], p["w_ukt"], p["w_uv"],
      p["qa_g"], p["kva_g"], *tabs)


def _conv_tables():
    n = CONV_WIN
    k = np.arange(n)
    ang = 2.0 * np.pi * (np.outer(k, k) % n) / n
    hart = np.cos(ang) + np.sin(ang)
    both = np.concatenate([hart, hart[(-k) % n]], axis=0)

    def split(m):
        hi = m.astype(BF16)
        return hi, (m - hi.astype(np.float64)).astype(BF16)

    fwd_hi, fwd_lo = split(both)
    inv_hi, inv_lo = split(hart[:CONV_HOP])
    inv = np.concatenate([inv_hi, inv_lo, inv_hi], axis=1)
    taps = np.zeros((2 * n, CONV_K + 1), np.float32)
    taps[:, :CONV_K] = both[:, n - 1 - np.arange(CONV_K)]
    return jnp.asarray(fwd_hi), jnp.asarray(fwd_lo), jnp.asarray(inv), jnp.asarray(taps)


def _conv_kernel(h_ref, gate_ref, w_ref, b_ref, lg_ref, lb_ref, fwd_hi_ref, fwd_lo_ref, inv_ref, taps_ref, o_ref,
                 pad_ref, ge_ref, go_ref, *, seq):
    n = CONV_WIN
    zeros = jnp.zeros((CONV_HALO, CONV_W), BF16)
    pad_ref[0:CONV_HALO, :] = zeros
    pad_ref[CONV_HALO + seq:, :] = zeros
    pad_ref[CONV_HALO:CONV_HALO + seq, :] = h_ref[...]

    @pl.when(pl.program_id(0) == 0)
    def _():
        g = jnp.dot(taps_ref[...], w_ref[...], preferred_element_type=F32, precision=lax.Precision.HIGHEST)
        ge_ref[...] = (g[:n] + g[n:]) * (0.5 / n)
        go_ref[...] = (g[:n] - g[n:]) * (0.5 / n)

    def forward(i):
        win = pad_ref[i * CONV_HOP:i * CONV_HOP + n, :]
        return (jnp.dot(fwd_hi_ref[...], win, preferred_element_type=F32)
                + jnp.dot(fwd_lo_ref[...], win, preferred_element_type=F32))

    n_win = seq // CONV_HOP
    xs_next = forward(0)
    for i in range(n_win):
        xs = xs_next
        if i + 1 < n_win:
            xs_next = forward(i + 1)
        z = xs[:n] * ge_ref[...] + xs[n:] * go_ref[...]
        z_hi = z.astype(BF16)
        z_lo = (z - z_hi.astype(F32)).astype(BF16)
        acc = jnp.dot(inv_ref[...], jnp.concatenate([z_hi, z_hi, z_lo], axis=0), preferred_element_type=F32)
        acc = acc + b_ref[...]
        mu = jnp.mean(acc, axis=-1, keepdims=True)
        cen = acc - mu
        var = jnp.mean(cen * cen, axis=-1, keepdims=True)
        hn = cen * lax.rsqrt(var + EPS) * lg_ref[...] + lb_ref[...]
        out_rows = slice(i * CONV_HOP, (i + 1) * CONV_HOP)
        o_ref[out_rows, :] = (hn * jax.nn.sigmoid(hn) * gate_ref[out_rows, :].astype(F32)).astype(o_ref.dtype)


def _conv(hglu, gate, p, layer, batch, seq):
    assert seq % CONV_HOP == 0
    tables = _conv_tables()
    whole = lambda a: pl.BlockSpec(a.shape, lambda i: (0,) * a.ndim, pipeline_mode=pl.Buffered(1))
    return pl.pallas_call(
        functools.partial(_conv_kernel, seq=seq),
        grid=(batch,),
        in_specs=[
            pl.BlockSpec((seq, CONV_W), lambda i: (i, 0)),
            pl.BlockSpec((seq, CONV_W), lambda i: (i, 0)),
            _layer_block(layer, (CONV_K + 1, CONV_W)),
            _layer_block(layer, (1, CONV_W)),
            _layer_block(layer, (1, CONV_W)),
            _layer_block(layer, (1, CONV_W)),
        ] + [whole(t) for t in tables],
        out_specs=pl.BlockSpec((seq, CONV_W), lambda i: (i, 0)),
        out_shape=jax.ShapeDtypeStruct(hglu.shape, BF16),
        scratch_shapes=[
            pltpu.VMEM((seq + 2 * CONV_HALO, CONV_W), BF16),
            pltpu.VMEM((CONV_WIN, CONV_W), F32),
            pltpu.VMEM((CONV_WIN, CONV_W), F32),
        ],
        compiler_params=pltpu.CompilerParams(dimension_semantics=("arbitrary",)),
        name="conv",
    )(hglu, gate, p["conv_w"], p["conv_b"], p["conv_ln_g"], p["conv_ln_b"], *tables)


def _log_sigmoid(x):
    return jnp.minimum(x, 0.0) - jnp.log1p(jnp.exp(-jnp.abs(x)))


def _ret_kernel(dl_ref, q_ref, k_ref, v_ref, gate_ref, o_ref, decay_ref, sd_ref, kv_ref, st_ref, *, seq, chunk, heads):
    C = chunk
    n_chunks = seq // C
    D = RET_HD
    ri = lax.broadcasted_iota(jnp.int32, (C, LANES), 0)
    idx = ri.astype(F32)
    trans_b = (((1,), (1,)), ((), ()))
    trans_a = (((0,), (0,)), ((), ()))

    def rows(n):
        return pl.ds(n * C, C)

    def lanes(j):
        return slice(j * D, (j + 1) * D)

    tabs = []
    for j in range(heads):
        lg_f = jnp.broadcast_to(_log_sigmoid(dl_ref[0, j])[0:1, :], (C, LANES))
        lg_b = jnp.broadcast_to(_log_sigmoid(dl_ref[1, j])[0:1, :], (C, LANES))
        for c in range(C // LANES):
            diff = (ri - (lax.broadcasted_iota(jnp.int32, (C, LANES), 1) + c * LANES)).astype(F32)
            decay_ref[j, :, c * LANES:(c + 1) * LANES] = jnp.where(
                diff >= 0.0, jnp.exp(lg_f * jnp.maximum(diff, 0.0)), jnp.exp(lg_b * jnp.maximum(-diff, 0.0)))
        tabs.append(dict(
            qdec_f=jnp.exp(lg_f * (idx + 1.0)), kdec_f=jnp.exp(lg_f * (C - 1.0 - idx)),
            qdec_b=jnp.exp(lg_b * (C - idx)), kdec_b=jnp.exp(lg_b * idx),
            cdec_f=jnp.exp(lg_f[:D] * float(C)), cdec_b=jnp.exp(lg_b[:D] * float(C))))

    for n in range(n_chunks):
        for j, t in enumerate(tabs):
            qn, kn, vn = q_ref[rows(n), lanes(j)], k_ref[rows(n), lanes(j)], v_ref[rows(n), lanes(j)]
            s = lax.dot_general(qn, kn, trans_b, preferred_element_type=F32)
            sd_ref[j, n] = (s * decay_ref[j]).astype(BF16)
            knf = kn.astype(F32)
            kv_ref[j, 0, n] = lax.dot_general((knf * t["kdec_f"]).astype(BF16), vn, trans_a, preferred_element_type=F32)
            kv_ref[j, 1, n] = lax.dot_general((knf * t["kdec_b"]).astype(BF16), vn, trans_a, preferred_element_type=F32)

    for j, t in enumerate(tabs):
        state = jnp.zeros((D, D), F32)
        for n in range(n_chunks):
            st_ref[j, 0, n] = state.astype(BF16)
            state = t["cdec_f"] * state + kv_ref[j, 0, n]
        state = jnp.zeros((D, D), F32)
        for n in reversed(range(n_chunks)):
            st_ref[j, 1, n] = state.astype(BF16)
            state = t["cdec_b"] * state + kv_ref[j, 1, n]

    for n in range(n_chunks):
        for j, t in enumerate(tabs):
            qnf = q_ref[rows(n), lanes(j)].astype(F32)
            lhs = jnp.concatenate(
                [sd_ref[j, n], (qnf * t["qdec_f"]).astype(BF16), (qnf * t["qdec_b"]).astype(BF16)], axis=1)
            rhs = jnp.concatenate([v_ref[rows(n), lanes(j)], st_ref[j, 0, n], st_ref[j, 1, n]], axis=0)
            out = jnp.dot(lhs, rhs, preferred_element_type=F32)
            mu = jnp.mean(out, axis=-1, keepdims=True)
            cen = out - mu
            var = jnp.mean(cen * cen, axis=-1, keepdims=True)
            gated = cen * lax.rsqrt(var + EPS) * gate_ref[rows(n), lanes(j)].astype(F32)
            o_ref[rows(n), lanes(j)] = gated.astype(o_ref.dtype)


def _retention(rqkv, gate, p, layer, batch, seq):
    chunk = 256 if seq % 256 == 0 else RET_CHUNK
    n_chunks = seq // chunk
    heads = 2
    groups = RET_HEADS // heads
    width = heads * RET_HD
    blk = lambda first: pl.BlockSpec((seq, width), lambda b, g: (b, first + g))
    return pl.pallas_call(
        functools.partial(_ret_kernel, seq=seq, chunk=chunk, heads=heads),
        grid=(batch, groups),
        in_specs=[
            pl.BlockSpec((None, 2, heads, SUBLANES, LANES), lambda b, g: (layer, 0, g, 0, 0)),
            blk(0), blk(groups), blk(2 * groups),
            blk(CONV_W // width),
        ],
        out_specs=blk(0),
        out_shape=jax.ShapeDtypeStruct((batch * seq, RET_W), BF16),
        scratch_shapes=[
            pltpu.VMEM((heads, chunk, chunk), F32),
            pltpu.VMEM((heads, n_chunks, chunk, chunk), BF16),
            pltpu.VMEM((heads, 2, n_chunks, RET_HD, RET_HD), F32),
            pltpu.VMEM((heads, 2, n_chunks, RET_HD, RET_HD), BF16),
        ],
        compiler_params=pltpu.CompilerParams(dimension_semantics=("arbitrary", "arbitrary")),
        name="retention",
    )(p["ret_decay"], rqkv, rqkv, rqkv, gate)


def _mla_kernel(q_ref, kt_ref, v_ref, gate_ref, o_ref, *, sub, heads):
    ones = jnp.ones((v_ref.shape[0], MLA_V_HD), BF16)
    kts = [kt_ref[j * MLA_QK_PAD:(j + 1) * MLA_QK_PAD, :] for j in range(heads)]
    vs = [jnp.concatenate([v_ref[:, j * MLA_V_HD:(j + 1) * MLA_V_HD], ones], axis=1) for j in range(heads)]
    items = [(j, r) for j in range(heads) for r in range(q_ref.shape[0] // sub)]

    def scores(j, r):
        q = q_ref[r * sub:(r + 1) * sub, j * MLA_QK_PAD:(j + 1) * MLA_QK_PAD]
        return jnp.dot(q, kts[j], preferred_element_type=F32)

    s_next = scores(*items[0])
    for i, (j, r) in enumerate(items):
        rows, cols = slice(r * sub, (r + 1) * sub), slice(j * MLA_V_HD, (j + 1) * MLA_V_HD)
        s = s_next
        if i + 1 < len(items):
            s_next = scores(*items[i + 1])
        m = jnp.max(s, axis=-1, keepdims=True)
        p = jnp.exp2(s - m)
        o = jnp.dot(p.astype(BF16), vs[j], preferred_element_type=F32)
        gated = o[:, :MLA_V_HD] / o[:, MLA_V_HD:] * gate_ref[rows, cols].astype(F32)
        o_ref[rows, cols] = gated.astype(o_ref.dtype)


def _mla(mq, mkt, mv, gate, batch, seq):
    _, sub, tq = _tiles(seq)
    nq = seq // tq
    heads = 2
    qk_w, v_w = heads * MLA_QK_PAD, heads * MLA_V_HD
    return pl.pallas_call(
        functools.partial(_mla_kernel, sub=sub, heads=heads),
        grid=(batch, MLA_HEADS // heads, nq),
        in_specs=[
            pl.BlockSpec((tq, qk_w), lambda b, g, i: (b * nq + i, g)),
            pl.BlockSpec((qk_w, seq), lambda b, g, i: (g, b)),
            pl.BlockSpec((seq, v_w), lambda b, g, i: (b, g)),
            pl.BlockSpec((tq, v_w), lambda b, g, i: (b * nq + i, (CONV_W + RET_W) // v_w + g)),
        ],
        out_specs=pl.BlockSpec((tq, v_w), lambda b, g, i: (b * nq + i, g)),
        out_shape=jax.ShapeDtypeStruct((batch * seq, MLA_W), BF16),
        compiler_params=pltpu.CompilerParams(
            dimension_semantics=("arbitrary", "arbitrary", "arbitrary"), vmem_limit_bytes=VMEM_LIMIT),
        name="mla_attention",
    )(mq, mkt, mv, gate)


def _outproj_kernel(x_ref, yc_ref, yr_ref, ym_ref, w_ref, fg_ref, o_ref, *, final):
    acc = x_ref[...]
    acc = acc + jnp.dot(yc_ref[...], w_ref[0:CONV_W, :], preferred_element_type=F32)
    acc = acc + jnp.dot(yr_ref[...], w_ref[CONV_W:CONV_W + RET_W, :], preferred_element_type=F32)
    acc = acc + jnp.dot(ym_ref[...], w_ref[CONV_W + RET_W:, :], preferred_element_type=F32)
    if final:
        acc = _rmsnorm(acc, fg_ref[...])
    o_ref[...] = acc


def _outproj(x2, yc, yr, ym, p, layer, seq, final):
    tm = min(2 * _tiles(seq)[0], seq)
    tokens = x2.shape[0]
    row = lambda i: (i, 0)
    return pl.pallas_call(
        functools.partial(_outproj_kernel, final=final),
        grid=(tokens // tm,),
        in_specs=[
            pl.BlockSpec((tm, D_MODEL), row),
            pl.BlockSpec((tm, CONV_W), row),
            pl.BlockSpec((tm, RET_W), row),
            pl.BlockSpec((tm, MLA_W), row),
            _layer_block(layer, (D_MIX, D_MODEL)),
            pl.BlockSpec((1, D_MODEL), lambda i: (0, 0)),
        ],
        out_specs=pl.BlockSpec((tm, D_MODEL), row),
        out_shape=jax.ShapeDtypeStruct(x2.shape, F32),
        compiler_params=pltpu.CompilerParams(
            dimension_semantics=("arbitrary",), vmem_limit_bytes=VMEM_LIMIT),
        name="outproj",
    )(x2, yc, yr, ym, p["w_out"], p["final_g"])


def _pad_last(w, width):
    return jnp.pad(w, [(0, 0)] * (w.ndim - 1) + [(0, width - w.shape[-1])])


def _split_w_in_kernel(wt_ref, main_ref, lat_ref, gate_ref):
    wt = wt_ref[...]
    k_rope = wt[OFF_KROPE:OFF_GATE, :]
    main_ref[...] = jnp.concatenate([wt[:OFF_QLAT, :].T, wt[OFF_KVLAT:OFF_KROPE, :].T], axis=1).astype(BF16)
    lat_ref[...] = jnp.concatenate([wt[OFF_QLAT:OFF_KVLAT, :].T,
                                    jnp.concatenate([k_rope, k_rope], axis=0).T], axis=1).astype(BF16)
    gate_ref[...] = wt[OFF_GATE:, :].T.astype(BF16)


def _split_w_in(w_in):
    depth = w_in.shape[0]
    rows = 256
    widths = (W_MAIN, MLA_Q_RANK + LANES, D_MIX)
    return pl.pallas_call(
        _split_w_in_kernel,
        grid=(depth, D_MODEL // rows),
        in_specs=[pl.BlockSpec((None, N_IN, rows), lambda l, i: (l, 0, i))],
        out_specs=[pl.BlockSpec((None, rows, w), lambda l, i: (l, i, 0)) for w in widths],
        out_shape=[jax.ShapeDtypeStruct((depth, D_MODEL, w), BF16) for w in widths],
        compiler_params=pltpu.CompilerParams(dimension_semantics=("arbitrary", "arbitrary")),
        name="split_w_in",
    )(jnp.swapaxes(w_in, 1, 2))


def _prep_params(norm_g, w_in, conv_dw_w, conv_dw_b, conv_ln_g, conv_ln_b, ret_decay_logit,
                 mla_qa_g, mla_w_uq, mla_kva_g, mla_w_ukv, w_out, final_g):
    depth = norm_g.shape[0]
    uq = mla_w_uq.reshape(depth, MLA_Q_RANK, MLA_HEADS, MLA_NOPE + MLA_ROPE)
    uq = jnp.concatenate([uq[..., :MLA_NOPE].reshape(depth, MLA_Q_RANK, MLA_HEADS * MLA_NOPE),
                          uq[..., MLA_NOPE:].reshape(depth, MLA_Q_RANK, MLA_HEADS * MLA_ROPE)], axis=-1)
    ukv = mla_w_ukv.reshape(depth, MLA_KV_RANK, MLA_HEADS, MLA_NOPE + MLA_V_HD)
    ukt = ukv[..., :MLA_NOPE].reshape(depth, MLA_KV_RANK, MLA_HEADS * MLA_NOPE).transpose(0, 2, 1)
    uv = ukv[..., MLA_NOPE:].reshape(depth, MLA_KV_RANK, MLA_W)
    row = lambda a: a[:, None, :]
    w_main, w_lat, w_gate = _split_w_in(w_in)
    return {
        "norm_g": row(norm_g),
        "w_main": w_main,
        "w_lat": w_lat,
        "w_gate": w_gate,
        "w_uq": uq.astype(BF16),
        "w_ukt": ukt.astype(BF16),
        "w_uv": uv.astype(BF16),
        "qa_g": row(mla_qa_g),
        "kva_g": row(mla_kva_g),
        "conv_w": jnp.pad(conv_dw_w, ((0, 0), (0, 1), (0, 0))),
        "conv_b": row(conv_dw_b),
        "conv_ln_g": row(conv_ln_g),
        "conv_ln_b": row(conv_ln_b),
        "ret_decay": jnp.broadcast_to(ret_decay_logit[:, :, :, None, None], (depth, 2, RET_HEADS, SUBLANES, LANES)),
        "w_out": w_out.astype(BF16),
        "final_g": final_g[None, :],
    }


def _rope_inputs(seq):
    cos_r, sin_r = _rope_tables(seq, RET_HD)
    half = RET_HD // 2
    sin_r = np.concatenate([-sin_r[:, :half], sin_r[:, half:]], axis=1)
    cos_m, sin_m = _rope_tables(seq, MLA_ROPE)
    half = MLA_ROPE // 2
    zeros = np.zeros((seq, half), np.float32)
    cosm = np.concatenate([cos_m, cos_m], axis=1)
    sinma = np.concatenate([-sin_m[:, :half], zeros, -sin_m[:, :half], zeros], axis=1)
    sinmb = np.concatenate([zeros, sin_m[:, half:], zeros, sin_m[:, half:]], axis=1)
    return tuple(jnp.asarray(t) for t in (cos_r, sin_r, cosm, sinma, sinmb))


def kernel(x, norm_g, w_in, conv_dw_w, conv_dw_b, conv_ln_g, conv_ln_b, ret_decay_logit,
           mla_qa_g, mla_w_uq, mla_kva_g, mla_w_ukv, w_out, final_g):
    batch, seq, d_model = x.shape
    depth = norm_g.shape[0]
    assert d_model == D_MODEL and seq % RET_CHUNK == 0
    p = _prep_params(norm_g, w_in, conv_dw_w, conv_dw_b, conv_ln_g, conv_ln_b, ret_decay_logit,
                     mla_qa_g, mla_w_uq, mla_kva_g, mla_w_ukv, w_out, final_g)
    tabs = _rope_inputs(seq)
    x2 = x.reshape(batch * seq, d_model)
    for layer in range(depth):
        hglu, rqkv, mq, mkt, mv, gate = _inproj(x2, p, layer, tabs, seq)
        yc = _conv(hglu, gate, p, layer, batch, seq)
        yr = _retention(rqkv, gate, p, layer, batch, seq)
        ym = _mla(mq, mkt, mv, gate, batch, seq)
        x2 = _outproj(x2, yc, yr, ym, p, layer, seq, final=(layer == depth - 1))
    return x2.reshape(batch, seq, d_model)
```

```python
import functools
import math

import jax
import jax.numpy as jnp
import numpy as np
from jax import lax
from jax.experimental import pallas as pl
from jax.experimental.pallas import tpu as pltpu

D_MODEL = 1024
D_MIX = 2 * D_MODEL
CONV_W = 512
CONV_K = 31
RET_W = 512
RET_HEADS = 4
RET_HD = 128
RET_CHUNK = 128
MLA_W = 1024
MLA_HEADS = 8
MLA_V_HD = 128
MLA_NOPE = 128
MLA_ROPE = 64
MLA_Q_RANK = 384
MLA_KV_RANK = 256
ROPE_BASE = 10000.0
EPS = 1e-6

OFF_RET = 2 * CONV_W
OFF_QLAT = OFF_RET + 3 * RET_W
OFF_KVLAT = OFF_QLAT + MLA_Q_RANK
OFF_KROPE = OFF_KVLAT + MLA_KV_RANK
OFF_GATE = OFF_KROPE + MLA_ROPE
N_IN = OFF_GATE + D_MIX
W_MAIN = OFF_QLAT + MLA_KV_RANK

LANES = 128
SUBLANES = 8
MLA_QK_PAD = 2 * LANES
CONV_HALO = 16
CONV_HOP = 128
CONV_WIN = CONV_HOP + 2 * CONV_HALO
VMEM_LIMIT = 56 * 1024 * 1024

BF16 = jnp.bfloat16
F32 = jnp.float32


def _tiles(seq):
    return min(512, seq), 256, min(2048, seq)


def _rope_tables(seq, dim):
    f32 = np.float32
    inv = f32(1.0) / (f32(ROPE_BASE) ** (np.arange(0, dim, 2, dtype=f32) / f32(dim)))
    ang = np.arange(seq, dtype=f32)[:, None] * inv[None, :]
    ang = np.concatenate([ang, ang], axis=-1).astype(np.float64)
    return np.cos(ang).astype(f32), np.sin(ang).astype(f32)


def _rmsnorm(x, g):
    return x * lax.rsqrt(jnp.mean(x * x, axis=-1, keepdims=True) + EPS) * g


VEC_NORM_G, VEC_QA_G, VEC_KVA_G, VEC_CONV_B, VEC_CONV_LN_G, VEC_CONV_LN_B = range(6)


def _vec(vec_ref, row, width):
    return vec_ref[row:row + 1, :width]


def _layer_block(layer, shape):
    return pl.BlockSpec((None,) + shape, lambda *_: (layer,) + (0,) * len(shape), pipeline_mode=pl.Buffered(1))


def _inproj_kernel(x_ref, vec_ref, wmain_ref, wlat_ref, wgate_ref, wuq_ref, wukt_ref, wuv_ref,
                   cosr_ref, sinr_ref, cosm_ref, sinma_ref, sinmb_ref,
                   hglu_ref, rqkv_ref, mq_ref, mkt_ref, mv_ref, gate_ref, *, sub):
    for r in range(x_ref.shape[0] // sub):
        _inproj_rows(slice(r * sub, (r + 1) * sub), x_ref, vec_ref, wmain_ref, wlat_ref, wgate_ref,
                     wuq_ref, wukt_ref, wuv_ref,
                     cosr_ref, sinr_ref, cosm_ref, sinma_ref, sinmb_ref,
                     hglu_ref, rqkv_ref, mq_ref, mkt_ref, mv_ref, gate_ref)


def _inproj_rows(rows, x_ref, vec_ref, wmain_ref, wlat_ref, wgate_ref, wuq_ref, wukt_ref, wuv_ref,
                 cosr_ref, sinr_ref, cosm_ref, sinma_ref, sinmb_ref,
                 hglu_ref, rqkv_ref, mq_ref, mkt_ref, mv_ref, gate_ref):
    h = _rmsnorm(x_ref[rows, :], _vec(vec_ref, VEC_NORM_G, D_MODEL)).astype(BF16)

    lat = jnp.dot(h, wlat_ref[...], preferred_element_type=F32)
    q_lat, k_rope = lat[:, :MLA_Q_RANK], lat[:, MLA_Q_RANK:]

    gt = jnp.dot(h, wgate_ref[...], preferred_element_type=F32)
    gate_ref[rows, :] = (gt * jax.nn.sigmoid(gt)).astype(BF16)

    u_main = jnp.dot(h, wmain_ref[...], preferred_element_type=F32)

    hglu_ref[rows, :] = (u_main[:, :CONV_W] * jax.nn.sigmoid(u_main[:, CONV_W:2 * CONV_W])).astype(BF16)

    cosr = cosr_ref[rows, :]
    sinr = sinr_ref[rows, :]
    k_scale = RET_HD ** -0.5
    for part in range(3):
        for hd in range(RET_HEADS):
            col = part * RET_W + hd * RET_HD
            blk = u_main[:, OFF_RET + col:OFF_RET + col + RET_HD]
            if part < 2:
                blk = blk * cosr + pltpu.roll(blk, RET_HD // 2, 1) * sinr
            if part == 1:
                blk = blk * k_scale
            rqkv_ref[rows, col:col + RET_HD] = blk.astype(BF16)

    cosm = cosm_ref[rows, :]
    sinma = sinma_ref[rows, :]
    sinmb = sinmb_ref[rows, :]

    def rope_pair(blk):
        return (blk * cosm + pltpu.roll(blk, LANES - MLA_ROPE // 2, 1) * sinma
                + pltpu.roll(blk, MLA_ROPE // 2, 1) * sinmb)

    qn = _rmsnorm(q_lat, _vec(vec_ref, VEC_QA_G, MLA_Q_RANK)).astype(BF16)
    q_scale = (MLA_NOPE + MLA_ROPE) ** -0.5 * math.log2(math.e)
    q_all = jnp.dot(qn, wuq_ref[...], preferred_element_type=F32) * q_scale
    rope_off = MLA_HEADS * MLA_NOPE
    lane = lax.broadcasted_iota(jnp.int32, (q_all.shape[0], LANES), 1)
    for pair in range(MLA_HEADS // 2):
        roped = rope_pair(q_all[:, rope_off + pair * LANES:rope_off + (pair + 1) * LANES])
        for j in range(2):
            hd = 2 * pair + j
            own = (lane < MLA_ROPE) if j == 0 else (lane >= MLA_ROPE)
            mq_ref[rows, hd * MLA_QK_PAD:hd * MLA_QK_PAD + LANES] = q_all[:, hd * MLA_NOPE:(hd + 1) * MLA_NOPE].astype(BF16)
            mq_ref[rows, hd * MLA_QK_PAD + LANES:(hd + 1) * MLA_QK_PAD] = jnp.where(own, roped, 0.0).astype(BF16)

    kvn = _rmsnorm(u_main[:, OFF_QLAT:W_MAIN], _vec(vec_ref, VEC_KVA_G, MLA_KV_RANK)).astype(BF16)
    k_rope_t = rope_pair(k_rope).T.astype(BF16)
    k_nope_t = lax.dot_general(wukt_ref[...], kvn, (((1,), (1,)), ((), ())),
                               preferred_element_type=F32).astype(BF16)
    for hd in range(MLA_HEADS):
        mkt_ref[hd * MLA_QK_PAD:hd * MLA_QK_PAD + LANES, rows] = k_nope_t[hd * MLA_NOPE:(hd + 1) * MLA_NOPE, :]
        mkt_ref[hd * MLA_QK_PAD + LANES:(hd + 1) * MLA_QK_PAD, rows] = k_rope_t
    mv_ref[rows, :] = jnp.dot(kvn, wuv_ref[...], preferred_element_type=F32).astype(BF16)


def _inproj(x2, p, layer, tabs, seq):
    tm, sub, _ = _tiles(seq)
    tokens = x2.shape[0]
    nseq = seq // tm
    row = lambda i: (i, 0)
    pos = lambda i: (i % nseq, 0)
    return pl.pallas_call(
        functools.partial(_inproj_kernel, sub=sub),
        grid=(tokens // tm,),
        in_specs=[
            pl.BlockSpec((tm, D_MODEL), row),
            _layer_block(layer, (SUBLANES, D_MODEL)),
            _layer_block(layer, (D_MODEL, W_MAIN)),
            _layer_block(layer, (D_MODEL, MLA_Q_RANK + LANES)),
            _layer_block(layer, (D_MODEL, D_MIX)),
            _layer_block(layer, (MLA_Q_RANK, MLA_HEADS * (MLA_NOPE + MLA_ROPE))),
            _layer_block(layer, (MLA_HEADS * MLA_NOPE, MLA_KV_RANK)),
            _layer_block(layer, (MLA_KV_RANK, MLA_W)),
        ] + [pl.BlockSpec((tm, LANES), pos)] * len(tabs),
        out_specs=[
            pl.BlockSpec((tm, CONV_W), row),
            pl.BlockSpec((tm, 3 * RET_W), row),
            pl.BlockSpec((tm, MLA_HEADS * MLA_QK_PAD), row),
            pl.BlockSpec((MLA_HEADS * MLA_QK_PAD, tm), lambda i: (0, i)),
            pl.BlockSpec((tm, MLA_W), row),
            pl.BlockSpec((tm, D_MIX), row),
        ],
        out_shape=[
            jax.ShapeDtypeStruct((tokens, CONV_W), BF16),
            jax.ShapeDtypeStruct((tokens, 3 * RET_W), BF16),
            jax.ShapeDtypeStruct((tokens, MLA_HEADS * MLA_QK_PAD), BF16),
            jax.ShapeDtypeStruct((MLA_HEADS * MLA_QK_PAD, tokens), BF16),
            jax.ShapeDtypeStruct((tokens, MLA_W), BF16),
            jax.ShapeDtypeStruct((tokens, D_MIX), BF16),
        ],
        compiler_params=pltpu.CompilerParams(
            dimension_semantics=("arbitrary",), vmem_limit_bytes=VMEM_LIMIT),
        name="inproj",
    )(x2, p["vecs"], p["w_main"], p["w_lat"], p["w_gate"], p["w_uq"], p["w_ukt"], p["w_uv"], *tabs)


def _conv_tables():
    n = CONV_WIN
    k = np.arange(n)
    ang = 2.0 * np.pi * (np.outer(k, k) % n) / n
    hart = np.cos(ang) + np.sin(ang)
    both = np.concatenate([hart, hart[(-k) % n]], axis=0)

    def split(m):
        hi = m.astype(BF16)
        return hi, (m - hi.astype(np.float64)).astype(BF16)

    fwd_hi, fwd_lo = split(both)
    inv_hi, inv_lo = split(hart[:CONV_HOP])
    inv = np.concatenate([inv_hi, inv_lo, inv_hi], axis=1)
    taps = np.zeros((2 * n, CONV_K + 1), np.float32)
    taps[:, :CONV_K] = both[:, n - 1 - np.arange(CONV_K)]
    return jnp.asarray(fwd_hi), jnp.asarray(fwd_lo), jnp.asarray(inv), jnp.asarray(taps)


def _conv_kernel(h_ref, gate_ref, w_ref, vec_ref, fwd_hi_ref, fwd_lo_ref, inv_ref, taps_ref, o_ref,
                 pad_ref, ge_ref, go_ref, *, seq):
    n = CONV_WIN
    zeros = jnp.zeros((CONV_HALO, CONV_W), BF16)
    pad_ref[0:CONV_HALO, :] = zeros
    pad_ref[CONV_HALO + seq:, :] = zeros
    pad_ref[CONV_HALO:CONV_HALO + seq, :] = h_ref[...]

    @pl.when(pl.program_id(0) == 0)
    def _():
        g = jnp.dot(taps_ref[...], w_ref[...], preferred_element_type=F32, precision=lax.Precision.HIGHEST)
        ge_ref[...] = (g[:n] + g[n:]) * (0.5 / n)
        go_ref[...] = (g[:n] - g[n:]) * (0.5 / n)

    def forward(i):
        win = pad_ref[i * CONV_HOP:i * CONV_HOP + n, :]
        return (jnp.dot(fwd_hi_ref[...], win, preferred_element_type=F32)
                + jnp.dot(fwd_lo_ref[...], win, preferred_element_type=F32))

    n_win = seq // CONV_HOP
    xs_next = forward(0)
    for i in range(n_win):
        xs = xs_next
        if i + 1 < n_win:
            xs_next = forward(i + 1)
        z = xs[:n] * ge_ref[...] + xs[n:] * go_ref[...]
        z_hi = z.astype(BF16)
        z_lo = (z - z_hi.astype(F32)).astype(BF16)
        acc = jnp.dot(inv_ref[...], jnp.concatenate([z_hi, z_hi, z_lo], axis=0), preferred_element_type=F32)
        acc = acc + _vec(vec_ref, VEC_CONV_B, CONV_W)
        mu = jnp.mean(acc, axis=-1, keepdims=True)
        cen = acc - mu
        var = jnp.mean(cen * cen, axis=-1, keepdims=True)
        hn = cen * lax.rsqrt(var + EPS) * _vec(vec_ref, VEC_CONV_LN_G, CONV_W) + _vec(vec_ref, VEC_CONV_LN_B, CONV_W)
        out_rows = slice(i * CONV_HOP, (i + 1) * CONV_HOP)
        o_ref[out_rows, :] = (hn * jax.nn.sigmoid(hn) * gate_ref[out_rows, :].astype(F32)).astype(o_ref.dtype)


def _conv(hglu, gate, p, layer, batch, seq):
    assert seq % CONV_HOP == 0
    tables = _conv_tables()
    whole = lambda a: pl.BlockSpec(a.shape, lambda i: (0,) * a.ndim, pipeline_mode=pl.Buffered(1))
    return pl.pallas_call(
        functools.partial(_conv_kernel, seq=seq),
        grid=(batch,),
        in_specs=[
            pl.BlockSpec((seq, CONV_W), lambda i: (i, 0)),
            pl.BlockSpec((seq, CONV_W), lambda i: (i, 0)),
            _layer_block(layer, (CONV_K + 1, CONV_W)),
            _layer_block(layer, (SUBLANES, D_MODEL)),
        ] + [whole(t) for t in tables],
        out_specs=pl.BlockSpec((seq, CONV_W), lambda i: (i, 0)),
        out_shape=jax.ShapeDtypeStruct(hglu.shape, BF16),
        scratch_shapes=[
            pltpu.VMEM((seq + 2 * CONV_HALO, CONV_W), BF16),
            pltpu.VMEM((CONV_WIN, CONV_W), F32),
            pltpu.VMEM((CONV_WIN, CONV_W), F32),
        ],
        compiler_params=pltpu.CompilerParams(dimension_semantics=("arbitrary",)),
        name="conv",
    )(hglu, gate, p["conv_w"], p["vecs"], *tables)


def _log_sigmoid(x):
    return jnp.minimum(x, 0.0) - jnp.log1p(jnp.exp(-jnp.abs(x)))


def _ret_kernel(dl_ref, q_ref, k_ref, v_ref, gate_ref, o_ref, decay_ref, sd_ref, kv_ref, st_ref, *, seq, chunk, heads):
    C = chunk
    n_chunks = seq // C
    D = RET_HD
    ri = lax.broadcasted_iota(jnp.int32, (C, LANES), 0)
    idx = ri.astype(F32)
    trans_b = (((1,), (1,)), ((), ()))
    trans_a = (((0,), (0,)), ((), ()))

    def rows(n):
        return pl.ds(n * C, C)

    def lanes(j):
        return slice(j * D, (j + 1) * D)

    tabs = []
    for j in range(heads):
        lg_f = jnp.broadcast_to(_log_sigmoid(dl_ref[0, j])[0:1, :], (C, LANES))
        lg_b = jnp.broadcast_to(_log_sigmoid(dl_ref[1, j])[0:1, :], (C, LANES))
        for c in range(C // LANES):
            diff = (ri - (lax.broadcasted_iota(jnp.int32, (C, LANES), 1) + c * LANES)).astype(F32)
            decay_ref[j, :, c * LANES:(c + 1) * LANES] = jnp.where(
                diff >= 0.0, jnp.exp(lg_f * jnp.maximum(diff, 0.0)), jnp.exp(lg_b * jnp.maximum(-diff, 0.0)))
        tabs.append(dict(
            qdec_f=jnp.exp(lg_f * (idx + 1.0)), kdec_f=jnp.exp(lg_f * (C - 1.0 - idx)),
            qdec_b=jnp.exp(lg_b * (C - idx)), kdec_b=jnp.exp(lg_b * idx),
            cdec_f=jnp.exp(lg_f[:D] * float(C)), cdec_b=jnp.exp(lg_b[:D] * float(C))))

    for n in range(n_chunks):
        for j, t in enumerate(tabs):
            qn, kn, vn = q_ref[rows(n), lanes(j)], k_ref[rows(n), lanes(j)], v_ref[rows(n), lanes(j)]
            s = lax.dot_general(qn, kn, trans_b, preferred_element_type=F32)
            sd_ref[j, n] = (s * decay_ref[j]).astype(BF16)
            knf = kn.astype(F32)
            kv_ref[j, 0, n] = lax.dot_general((knf * t["kdec_f"]).astype(BF16), vn, trans_a, preferred_element_type=F32)
            kv_ref[j, 1, n] = lax.dot_general((knf * t["kdec_b"]).astype(BF16), vn, trans_a, preferred_element_type=F32)

    for j, t in enumerate(tabs):
        state = jnp.zeros((D, D), F32)
        for n in range(n_chunks):
            st_ref[j, 0, n] = state.astype(BF16)
            state = t["cdec_f"] * state + kv_ref[j, 0, n]
        state = jnp.zeros((D, D), F32)
        for n in reversed(range(n_chunks)):
            st_ref[j, 1, n] = state.astype(BF16)
            state = t["cdec_b"] * state + kv_ref[j, 1, n]

    for n in range(n_chunks):
        for j, t in enumerate(tabs):
            qnf = q_ref[rows(n), lanes(j)].astype(F32)
            lhs = jnp.concatenate(
                [sd_ref[j, n], (qnf * t["qdec_f"]).astype(BF16), (qnf * t["qdec_b"]).astype(BF16)], axis=1)
            rhs = jnp.concatenate([v_ref[rows(n), lanes(j)], st_ref[j, 0, n], st_ref[j, 1, n]], axis=0)
            out = jnp.dot(lhs, rhs, preferred_element_type=F32)
            mu = jnp.mean(out, axis=-1, keepdims=True)
            cen = out - mu
            var = jnp.mean(cen * cen, axis=-1, keepdims=True)
            gated = cen * lax.rsqrt(var + EPS) * gate_ref[rows(n), lanes(j)].astype(F32)
            o_ref[rows(n), lanes(j)] = gated.astype(o_ref.dtype)


def _retention(rqkv, gate, p, layer, batch, seq):
    chunk = 256 if seq % 256 == 0 else RET_CHUNK
    n_chunks = seq // chunk
    heads = 2
    groups = RET_HEADS // heads
    width = heads * RET_HD
    blk = lambda first: pl.BlockSpec((seq, width), lambda b, g: (b, first + g))
    return pl.pallas_call(
        functools.partial(_ret_kernel, seq=seq, chunk=chunk, heads=heads),
        grid=(batch, groups),
        in_specs=[
            pl.BlockSpec((None, 2, heads, SUBLANES, LANES), lambda b, g: (layer, 0, g, 0, 0)),
            blk(0), blk(groups), blk(2 * groups),
            blk(CONV_W // width),
        ],
        out_specs=blk(0),
        out_shape=jax.ShapeDtypeStruct((batch * seq, RET_W), BF16),
        scratch_shapes=[
            pltpu.VMEM((heads, chunk, chunk), F32),
            pltpu.VMEM((heads, n_chunks, chunk, chunk), BF16),
            pltpu.VMEM((heads, 2, n_chunks, RET_HD, RET_HD), F32),
            pltpu.VMEM((heads, 2, n_chunks, RET_HD, RET_HD), BF16),
        ],
        compiler_params=pltpu.CompilerParams(dimension_semantics=("arbitrary", "arbitrary")),
        name="retention",
    )(p["ret_decay"], rqkv, rqkv, rqkv, gate)


def _mla_kernel(q_ref, kt_ref, v_ref, gate_ref, o_ref, *, sub, heads):
    ones = jnp.ones((v_ref.shape[0], MLA_V_HD), BF16)
    kts = [kt_ref[j * MLA_QK_PAD:(j + 1) * MLA_QK_PAD, :] for j in range(heads)]
    vs = [jnp.concatenate([v_ref[:, j * MLA_V_HD:(j + 1) * MLA_V_HD], ones], axis=1) for j in range(heads)]
    items = [(j, r) for j in range(heads) for r in range(q_ref.shape[0] // sub)]

    def scores(j, r):
        q = q_ref[r * sub:(r + 1) * sub, j * MLA_QK_PAD:(j + 1) * MLA_QK_PAD]
        return jnp.dot(q, kts[j], preferred_element_type=F32)

    s_next = scores(*items[0])
    for i, (j, r) in enumerate(items):
        rows, cols = slice(r * sub, (r + 1) * sub), slice(j * MLA_V_HD, (j + 1) * MLA_V_HD)
        s = s_next
        if i + 1 < len(items):
            s_next = scores(*items[i + 1])
        m = jnp.max(s, axis=-1, keepdims=True)
        p = jnp.exp2(s - m)
        o = jnp.dot(p.astype(BF16), vs[j], preferred_element_type=F32)
        gated = o[:, :MLA_V_HD] / o[:, MLA_V_HD:] * gate_ref[rows, cols].astype(F32)
        o_ref[rows, cols] = gated.astype(o_ref.dtype)


def _mla(mq, mkt, mv, gate, batch, seq):
    _, sub, tq = _tiles(seq)
    nq = seq // tq
    heads = 2
    qk_w, v_w = heads * MLA_QK_PAD, heads * MLA_V_HD
    return pl.pallas_call(
        functools.partial(_mla_kernel, sub=sub, heads=heads),
        grid=(batch, MLA_HEADS // heads, nq),
        in_specs=[
            pl.BlockSpec((tq, qk_w), lambda b, g, i: (b * nq + i, g)),
            pl.BlockSpec((qk_w, seq), lambda b, g, i: (g, b)),
            pl.BlockSpec((seq, v_w), lambda b, g, i: (b, g)),
            pl.BlockSpec((tq, v_w), lambda b, g, i: (b * nq + i, (CONV_W + RET_W) // v_w + g)),
        ],
        out_specs=pl.BlockSpec((tq, v_w), lambda b, g, i: (b * nq + i, g)),
        out_shape=jax.ShapeDtypeStruct((batch * seq, MLA_W), BF16),
        compiler_params=pltpu.CompilerParams(
            dimension_semantics=("arbitrary", "arbitrary", "arbitrary"), vmem_limit_bytes=VMEM_LIMIT),
        name="mla_attention",
    )(mq, mkt, mv, gate)


def _outproj_kernel(x_ref, yc_ref, yr_ref, ym_ref, w32_ref, fg_ref, o_ref, w_ref, *, final):
    @pl.when(pl.program_id(0) == 0)
    def _():
        w_ref[...] = w32_ref[...].astype(BF16)

    acc = x_ref[...]
    acc = acc + jnp.dot(yc_ref[...], w_ref[0:CONV_W, :], preferred_element_type=F32)
    acc = acc + jnp.dot(yr_ref[...], w_ref[CONV_W:CONV_W + RET_W, :], preferred_element_type=F32)
    acc = acc + jnp.dot(ym_ref[...], w_ref[CONV_W + RET_W:, :], preferred_element_type=F32)
    if final:
        acc = _rmsnorm(acc, fg_ref[...])
    o_ref[...] = acc


def _outproj(x2, yc, yr, ym, p, layer, seq, final):
    tm = min(2 * _tiles(seq)[0], seq)
    tokens = x2.shape[0]
    row = lambda i: (i, 0)
    return pl.pallas_call(
        functools.partial(_outproj_kernel, final=final),
        grid=(tokens // tm,),
        in_specs=[
            pl.BlockSpec((tm, D_MODEL), row),
            pl.BlockSpec((tm, CONV_W), row),
            pl.BlockSpec((tm, RET_W), row),
            pl.BlockSpec((tm, MLA_W), row),
            _layer_block(layer, (D_MIX, D_MODEL)),
            pl.BlockSpec((1, D_MODEL), lambda i: (0, 0)),
        ],
        out_specs=pl.BlockSpec((tm, D_MODEL), row),
        out_shape=jax.ShapeDtypeStruct(x2.shape, F32),
        scratch_shapes=[pltpu.VMEM((D_MIX, D_MODEL), BF16)],
        compiler_params=pltpu.CompilerParams(
            dimension_semantics=("arbitrary",), vmem_limit_bytes=VMEM_LIMIT),
        name="outproj",
    )(x2, yc, yr, ym, p["w_out"], p["final_g"])


def _pad_last(w, width):
    return jnp.pad(w, [(0, 0)] * (w.ndim - 1) + [(0, width - w.shape[-1])])


def _split_w_in_kernel(wt_ref, main_ref, lat_ref, gate_ref):
    wt = wt_ref[...]
    k_rope = wt[OFF_KROPE:OFF_GATE, :]
    main_ref[...] = jnp.concatenate([wt[:OFF_QLAT, :].T, wt[OFF_KVLAT:OFF_KROPE, :].T], axis=1).astype(BF16)
    lat_ref[...] = jnp.concatenate([wt[OFF_QLAT:OFF_KVLAT, :].T,
                                    jnp.concatenate([k_rope, k_rope], axis=0).T], axis=1).astype(BF16)
    gate_ref[...] = wt[OFF_GATE:, :].T.astype(BF16)


def _split_w_in(w_in):
    depth = w_in.shape[0]
    rows = 256
    widths = (W_MAIN, MLA_Q_RANK + LANES, D_MIX)
    return pl.pallas_call(
        _split_w_in_kernel,
        grid=(depth, D_MODEL // rows),
        in_specs=[pl.BlockSpec((None, N_IN, rows), lambda l, i: (l, 0, i))],
        out_specs=[pl.BlockSpec((None, rows, w), lambda l, i: (l, i, 0)) for w in widths],
        out_shape=[jax.ShapeDtypeStruct((depth, D_MODEL, w), BF16) for w in widths],
        compiler_params=pltpu.CompilerParams(dimension_semantics=("arbitrary", "arbitrary")),
        name="split_w_in",
    )(jnp.swapaxes(w_in, 1, 2))


def _prep_params(norm_g, w_in, conv_dw_w, conv_dw_b, conv_ln_g, conv_ln_b, ret_decay_logit,
                 mla_qa_g, mla_w_uq, mla_kva_g, mla_w_ukv, w_out, final_g):
    depth = norm_g.shape[0]
    uq = mla_w_uq.reshape(depth, MLA_Q_RANK, MLA_HEADS, MLA_NOPE + MLA_ROPE)
    uq = jnp.concatenate([uq[..., :MLA_NOPE].reshape(depth, MLA_Q_RANK, MLA_HEADS * MLA_NOPE),
                          uq[..., MLA_NOPE:].reshape(depth, MLA_Q_RANK, MLA_HEADS * MLA_ROPE)], axis=-1)
    ukv = mla_w_ukv.reshape(depth, MLA_KV_RANK, MLA_HEADS, MLA_NOPE + MLA_V_HD)
    ukt = ukv[..., :MLA_NOPE].reshape(depth, MLA_KV_RANK, MLA_HEADS * MLA_NOPE).transpose(0, 2, 1)
    uv = ukv[..., MLA_NOPE:].reshape(depth, MLA_KV_RANK, MLA_W)
    vecs = [norm_g, mla_qa_g, mla_kva_g, conv_dw_b, conv_ln_g, conv_ln_b]
    vecs = jnp.stack([_pad_last(v, D_MODEL) for v in vecs] + [jnp.zeros_like(norm_g)] * (SUBLANES - len(vecs)), axis=1)
    w_main, w_lat, w_gate = _split_w_in(w_in)
    return {
        "vecs": vecs,
        "w_main": w_main,
        "w_lat": w_lat,
        "w_gate": w_gate,
        "w_uq": uq.astype(BF16),
        "w_ukt": ukt.astype(BF16),
        "w_uv": uv.astype(BF16),
        "conv_w": jnp.pad(conv_dw_w, ((0, 0), (0, 1), (0, 0))),
        "ret_decay": jnp.broadcast_to(ret_decay_logit[:, :, :, None, None], (depth, 2, RET_HEADS, SUBLANES, LANES)),
        "w_out": w_out,
        "final_g": final_g[None, :],
    }


def _rope_inputs(seq):
    cos_r, sin_r = _rope_tables(seq, RET_HD)
    half = RET_HD // 2
    sin_r = np.concatenate([-sin_r[:, :half], sin_r[:, half:]], axis=1)
    cos_m, sin_m = _rope_tables(seq, MLA_ROPE)
    half = MLA_ROPE // 2
    zeros = np.zeros((seq, half), np.float32)
    cosm = np.concatenate([cos_m, cos_m], axis=1)
    sinma = np.concatenate([-sin_m[:, :half], zeros, -sin_m[:, :half], zeros], axis=1)
    sinmb = np.concatenate([zeros, sin_m[:, half:], zeros, sin_m[:, half:]], axis=1)
    return tuple(jnp.asarray(t) for t in (cos_r, sin_r, cosm, sinma, sinmb))


def kernel(x, norm_g, w_in, conv_dw_w, conv_dw_b, conv_ln_g, conv_ln_b, ret_decay_logit,
           mla_qa_g, mla_w_uq, mla_kva_g, mla_w_ukv, w_out, final_g):
    batch, seq, d_model = x.shape
    depth = norm_g.shape[0]
    assert d_model == D_MODEL and seq % RET_CHUNK == 0
    p = _prep_params(norm_g, w_in, conv_dw_w, conv_dw_b, conv_ln_g, conv_ln_b, ret_decay_logit,
                     mla_qa_g, mla_w_uq, mla_kva_g, mla_w_ukv, w_out, final_g)
    tabs = _rope_inputs(seq)
    x2 = x.reshape(batch * seq, d_model)
    for layer in range(depth):
        hglu, rqkv, mq, mkt, mv, gate = _inproj(x2, p, layer, tabs, seq)
        yc = _conv(hglu, gate, p, layer, batch, seq)
        yr = _retention(rqkv, gate, p, layer, batch, seq)
        ym = _mla(mq, mkt, mv, gate, batch, seq)
        x2 = _outproj(x2, yc, yr, ym, p, layer, seq, final=(layer == depth - 1))
    return x2.reshape(batch, seq, d_model)
```

```python
import functools
import math

import jax
import jax.numpy as jnp
import numpy as np
from jax import lax
from jax.experimental import pallas as pl
from jax.experimental.pallas import tpu as pltpu

D_MODEL = 1024
D_MIX = 2 * D_MODEL
CONV_W = 512
CONV_K = 31
RET_W = 512
RET_HEADS = 4
RET_HD = 128
RET_CHUNK = 128
MLA_W = 1024
MLA_HEADS = 8
MLA_V_HD = 128
MLA_NOPE = 128
MLA_ROPE = 64
MLA_Q_RANK = 384
MLA_KV_RANK = 256
ROPE_BASE = 10000.0
EPS = 1e-6

OFF_RET = 2 * CONV_W
OFF_QLAT = OFF_RET + 3 * RET_W
OFF_KVLAT = OFF_QLAT + MLA_Q_RANK
OFF_KROPE = OFF_KVLAT + MLA_KV_RANK
OFF_GATE = OFF_KROPE + MLA_ROPE
N_IN = OFF_GATE + D_MIX
W_MAIN = OFF_QLAT + MLA_KV_RANK

LANES = 128
SUBLANES = 8
MLA_QK_PAD = 2 * LANES
CONV_HALO = 16
CONV_HOP = 128
CONV_WIN = CONV_HOP + 2 * CONV_HALO
VMEM_LIMIT = 56 * 1024 * 1024

BF16 = jnp.bfloat16
F32 = jnp.float32


def _tiles(seq):
    return min(512, seq), 256, min(2048, seq)


def _rope_tables(seq, dim):
    f32 = np.float32
    inv = f32(1.0) / (f32(ROPE_BASE) ** (np.arange(0, dim, 2, dtype=f32) / f32(dim)))
    ang = np.arange(seq, dtype=f32)[:, None] * inv[None, :]
    ang = np.concatenate([ang, ang], axis=-1).astype(np.float64)
    return np.cos(ang).astype(f32), np.sin(ang).astype(f32)


def _rmsnorm(x, g):
    return x * lax.rsqrt(jnp.mean(x * x, axis=-1, keepdims=True) + EPS) * g


VEC_NORM_G, VEC_QA_G, VEC_KVA_G, VEC_CONV_B, VEC_CONV_LN_G, VEC_CONV_LN_B = range(6)


def _vec(vec_ref, row, width):
    return vec_ref[row:row + 1, :width]


def _layer_block(layer, shape):
    return pl.BlockSpec((None,) + shape, lambda *_: (layer,) + (0,) * len(shape), pipeline_mode=pl.Buffered(1))


def _inproj_kernel(x_ref, vec_ref, wmain_ref, wlat_ref, wgate_ref, wuq_ref, wukt_ref, wuv_ref,
                   cosr_ref, sinr_ref, cosm_ref, sinma_ref, sinmb_ref,
                   hglu_ref, rqkv_ref, mq_ref, mkt_ref, mv_ref, gate_ref, *, sub):
    blocks = [_inproj_rows(slice(r * sub, (r + 1) * sub), x_ref, vec_ref, wmain_ref, wlat_ref, wgate_ref,
                           wuq_ref, wukt_ref, wuv_ref,
                           cosr_ref, sinr_ref, cosm_ref, sinma_ref, sinmb_ref,
                           hglu_ref, rqkv_ref, mq_ref, mkt_ref, mv_ref, gate_ref)
              for r in range(x_ref.shape[0] // sub)]
    for _ in zip(*blocks):
        pass


def _inproj_rows(rows, x_ref, vec_ref, wmain_ref, wlat_ref, wgate_ref, wuq_ref, wukt_ref, wuv_ref,
                 cosr_ref, sinr_ref, cosm_ref, sinma_ref, sinmb_ref,
                 hglu_ref, rqkv_ref, mq_ref, mkt_ref, mv_ref, gate_ref):
    h = _rmsnorm(x_ref[rows, :], _vec(vec_ref, VEC_NORM_G, D_MODEL)).astype(BF16)

    lat = jnp.dot(h, wlat_ref[...], preferred_element_type=F32)
    q_lat, k_rope = lat[:, :MLA_Q_RANK], lat[:, MLA_Q_RANK:]
    yield

    gt = jnp.dot(h, wgate_ref[...], preferred_element_type=F32)
    gate_ref[rows, :] = (gt * jax.nn.sigmoid(gt)).astype(BF16)
    yield

    u_main = jnp.dot(h, wmain_ref[...], preferred_element_type=F32)

    hglu_ref[rows, :] = (u_main[:, :CONV_W] * jax.nn.sigmoid(u_main[:, CONV_W:2 * CONV_W])).astype(BF16)

    cosr = cosr_ref[rows, :]
    sinr = sinr_ref[rows, :]
    k_scale = RET_HD ** -0.5
    for part in range(3):
        for hd in range(RET_HEADS):
            col = part * RET_W + hd * RET_HD
            blk = u_main[:, OFF_RET + col:OFF_RET + col + RET_HD]
            if part < 2:
                blk = blk * cosr + pltpu.roll(blk, RET_HD // 2, 1) * sinr
            if part == 1:
                blk = blk * k_scale
            rqkv_ref[rows, col:col + RET_HD] = blk.astype(BF16)
    yield

    cosm = cosm_ref[rows, :]
    sinma = sinma_ref[rows, :]
    sinmb = sinmb_ref[rows, :]

    def rope_pair(blk):
        return (blk * cosm + pltpu.roll(blk, LANES - MLA_ROPE // 2, 1) * sinma
                + pltpu.roll(blk, MLA_ROPE // 2, 1) * sinmb)

    qn = _rmsnorm(q_lat, _vec(vec_ref, VEC_QA_G, MLA_Q_RANK)).astype(BF16)
    q_scale = (MLA_NOPE + MLA_ROPE) ** -0.5 * math.log2(math.e)
    q_all = jnp.dot(qn, wuq_ref[...], preferred_element_type=F32) * q_scale
    rope_off = MLA_HEADS * MLA_NOPE
    lane = lax.broadcasted_iota(jnp.int32, (q_all.shape[0], LANES), 1)
    for pair in range(MLA_HEADS // 2):
        roped = rope_pair(q_all[:, rope_off + pair * LANES:rope_off + (pair + 1) * LANES])
        for j in range(2):
            hd = 2 * pair + j
            own = (lane < MLA_ROPE) if j == 0 else (lane >= MLA_ROPE)
            mq_ref[rows, hd * MLA_QK_PAD:hd * MLA_QK_PAD + LANES] = q_all[:, hd * MLA_NOPE:(hd + 1) * MLA_NOPE].astype(BF16)
            mq_ref[rows, hd * MLA_QK_PAD + LANES:(hd + 1) * MLA_QK_PAD] = jnp.where(own, roped, 0.0).astype(BF16)

    yield
    kvn = _rmsnorm(u_main[:, OFF_QLAT:W_MAIN], _vec(vec_ref, VEC_KVA_G, MLA_KV_RANK)).astype(BF16)
    k_rope_t = rope_pair(k_rope).T.astype(BF16)
    k_nope_t = lax.dot_general(wukt_ref[...], kvn, (((1,), (1,)), ((), ())),
                               preferred_element_type=F32).astype(BF16)
    for hd in range(MLA_HEADS):
        mkt_ref[hd * MLA_QK_PAD:hd * MLA_QK_PAD + LANES, rows] = k_nope_t[hd * MLA_NOPE:(hd + 1) * MLA_NOPE, :]
        mkt_ref[hd * MLA_QK_PAD + LANES:(hd + 1) * MLA_QK_PAD, rows] = k_rope_t
    mv_ref[rows, :] = jnp.dot(kvn, wuv_ref[...], preferred_element_type=F32).astype(BF16)
    yield


def _inproj(x2, p, layer, tabs, seq):
    tm, sub, _ = _tiles(seq)
    tokens = x2.shape[0]
    nseq = seq // tm
    row = lambda i: (i, 0)
    pos = lambda i: (i % nseq, 0)
    return pl.pallas_call(
        functools.partial(_inproj_kernel, sub=sub),
        grid=(tokens // tm,),
        in_specs=[
            pl.BlockSpec((tm, D_MODEL), row),
            _layer_block(layer, (SUBLANES, D_MODEL)),
            _layer_block(layer, (D_MODEL, W_MAIN)),
            _layer_block(layer, (D_MODEL, MLA_Q_RANK + LANES)),
            _layer_block(layer, (D_MODEL, D_MIX)),
            _layer_block(layer, (MLA_Q_RANK, MLA_HEADS * (MLA_NOPE + MLA_ROPE))),
            _layer_block(layer, (MLA_HEADS * MLA_NOPE, MLA_KV_RANK)),
            _layer_block(layer, (MLA_KV_RANK, MLA_W)),
        ] + [pl.BlockSpec((tm, LANES), pos)] * len(tabs),
        out_specs=[
            pl.BlockSpec((tm, CONV_W), row),
            pl.BlockSpec((tm, 3 * RET_W), row),
            pl.BlockSpec((tm, MLA_HEADS * MLA_QK_PAD), row),
            pl.BlockSpec((MLA_HEADS * MLA_QK_PAD, tm), lambda i: (0, i)),
            pl.BlockSpec((tm, MLA_W), row),
            pl.BlockSpec((tm, D_MIX), row),
        ],
        out_shape=[
            jax.ShapeDtypeStruct((tokens, CONV_W), BF16),
            jax.ShapeDtypeStruct((tokens, 3 * RET_W), BF16),
            jax.ShapeDtypeStruct((tokens, MLA_HEADS * MLA_QK_PAD), BF16),
            jax.ShapeDtypeStruct((MLA_HEADS * MLA_QK_PAD, tokens), BF16),
            jax.ShapeDtypeStruct((tokens, MLA_W), BF16),
            jax.ShapeDtypeStruct((tokens, D_MIX), BF16),
        ],
        compiler_params=pltpu.CompilerParams(
            dimension_semantics=("arbitrary",), vmem_limit_bytes=VMEM_LIMIT),
        name="inproj",
    )(x2, p["vecs"], p["w_main"], p["w_lat"], p["w_gate"], p["w_uq"], p["w_ukt"], p["w_uv"], *tabs)


def _conv_tables():
    n = CONV_WIN
    k = np.arange(n)
    ang = 2.0 * np.pi * (np.outer(k, k) % n) / n
    hart = np.cos(ang) + np.sin(ang)
    both = np.concatenate([hart, hart[(-k) % n]], axis=0)

    def split(m):
        hi = m.astype(BF16)
        return hi, (m - hi.astype(np.float64)).astype(BF16)

    fwd_hi, fwd_lo = split(both)
    inv_hi, inv_lo = split(hart[:CONV_HOP])
    inv = np.concatenate([inv_hi, inv_lo, inv_hi], axis=1)
    taps = np.zeros((2 * n, CONV_K + 1), np.float32)
    taps[:, :CONV_K] = both[:, n - 1 - np.arange(CONV_K)]
    return jnp.asarray(fwd_hi), jnp.asarray(fwd_lo), jnp.asarray(inv), jnp.asarray(taps)


def _conv_kernel(h_ref, gate_ref, w_ref, vec_ref, fwd_hi_ref, fwd_lo_ref, inv_ref, taps_ref, o_ref,
                 pad_ref, ge_ref, go_ref, *, seq):
    n = CONV_WIN
    zeros = jnp.zeros((CONV_HALO, CONV_W), BF16)
    pad_ref[0:CONV_HALO, :] = zeros
    pad_ref[CONV_HALO + seq:, :] = zeros
    pad_ref[CONV_HALO:CONV_HALO + seq, :] = h_ref[...]

    @pl.when(pl.program_id(0) == 0)
    def _():
        g = jnp.dot(taps_ref[...], w_ref[...], preferred_element_type=F32, precision=lax.Precision.HIGHEST)
        ge_ref[...] = (g[:n] + g[n:]) * (0.5 / n)
        go_ref[...] = (g[:n] - g[n:]) * (0.5 / n)

    def forward(i):
        win = pad_ref[i * CONV_HOP:i * CONV_HOP + n, :]
        return (jnp.dot(fwd_hi_ref[...], win, preferred_element_type=F32)
                + jnp.dot(fwd_lo_ref[...], win, preferred_element_type=F32))

    n_win = seq // CONV_HOP
    xs_next = forward(0)
    for i in range(n_win):
        xs = xs_next
        if i + 1 < n_win:
            xs_next = forward(i + 1)
        z = xs[:n] * ge_ref[...] + xs[n:] * go_ref[...]
        z_hi = z.astype(BF16)
        z_lo = (z - z_hi.astype(F32)).astype(BF16)
        acc = jnp.dot(inv_ref[...], jnp.concatenate([z_hi, z_hi, z_lo], axis=0), preferred_element_type=F32)
        acc = acc + _vec(vec_ref, VEC_CONV_B, CONV_W)
        mu = jnp.mean(acc, axis=-1, keepdims=True)
        cen = acc - mu
        var = jnp.mean(cen * cen, axis=-1, keepdims=True)
        hn = cen * lax.rsqrt(var + EPS) * _vec(vec_ref, VEC_CONV_LN_G, CONV_W) + _vec(vec_ref, VEC_CONV_LN_B, CONV_W)
        out_rows = slice(i * CONV_HOP, (i + 1) * CONV_HOP)
        o_ref[out_rows, :] = (hn * jax.nn.sigmoid(hn) * gate_ref[out_rows, :].astype(F32)).astype(o_ref.dtype)


def _conv(hglu, gate, p, layer, batch, seq):
    assert seq % CONV_HOP == 0
    tables = _conv_tables()
    whole = lambda a: pl.BlockSpec(a.shape, lambda i: (0,) * a.ndim, pipeline_mode=pl.Buffered(1))
    return pl.pallas_call(
        functools.partial(_conv_kernel, seq=seq),
        grid=(batch,),
        in_specs=[
            pl.BlockSpec((seq, CONV_W), lambda i: (i, 0)),
            pl.BlockSpec((seq, CONV_W), lambda i: (i, 0)),
            _layer_block(layer, (CONV_K + 1, CONV_W)),
            _layer_block(layer, (SUBLANES, D_MODEL)),
        ] + [whole(t) for t in tables],
        out_specs=pl.BlockSpec((seq, CONV_W), lambda i: (i, 0)),
        out_shape=jax.ShapeDtypeStruct(hglu.shape, BF16),
        scratch_shapes=[
            pltpu.VMEM((seq + 2 * CONV_HALO, CONV_W), BF16),
            pltpu.VMEM((CONV_WIN, CONV_W), F32),
            pltpu.VMEM((CONV_WIN, CONV_W), F32),
        ],
        compiler_params=pltpu.CompilerParams(dimension_semantics=("arbitrary",)),
        name="conv",
    )(hglu, gate, p["conv_w"], p["vecs"], *tables)


def _log_sigmoid(x):
    return jnp.minimum(x, 0.0) - jnp.log1p(jnp.exp(-jnp.abs(x)))


def _ret_kernel(dl_ref, q_ref, k_ref, v_ref, gate_ref, o_ref, decay_ref, sd_ref, kv_ref, st_ref, *, seq, chunk, heads):
    C = chunk
    n_chunks = seq // C
    D = RET_HD
    ri = lax.broadcasted_iota(jnp.int32, (C, LANES), 0)
    idx = ri.astype(F32)
    trans_b = (((1,), (1,)), ((), ()))
    trans_a = (((0,), (0,)), ((), ()))

    def rows(n):
        return pl.ds(n * C, C)

    def lanes(j):
        return slice(j * D, (j + 1) * D)

    tabs = []
    for j in range(heads):
        lg_f = jnp.broadcast_to(_log_sigmoid(dl_ref[0, j])[0:1, :], (C, LANES))
        lg_b = jnp.broadcast_to(_log_sigmoid(dl_ref[1, j])[0:1, :], (C, LANES))
        for c in range(C // LANES):
            diff = (ri - (lax.broadcasted_iota(jnp.int32, (C, LANES), 1) + c * LANES)).astype(F32)
            decay_ref[j, :, c * LANES:(c + 1) * LANES] = jnp.where(
                diff >= 0.0, jnp.exp(lg_f * jnp.maximum(diff, 0.0)), jnp.exp(lg_b * jnp.maximum(-diff, 0.0)))
        tabs.append(dict(
            qdec_f=jnp.exp(lg_f * (idx + 1.0)), kdec_f=jnp.exp(lg_f * (C - 1.0 - idx)),
            qdec_b=jnp.exp(lg_b * (C - idx)), kdec_b=jnp.exp(lg_b * idx),
            cdec_f=jnp.exp(lg_f[:D] * float(C)), cdec_b=jnp.exp(lg_b[:D] * float(C))))

    for n in range(n_chunks):
        for j, t in enumerate(tabs):
            qn, kn, vn = q_ref[rows(n), lanes(j)], k_ref[rows(n), lanes(j)], v_ref[rows(n), lanes(j)]
            s = lax.dot_general(qn, kn, trans_b, preferred_element_type=F32)
            sd_ref[j, n] = (s * decay_ref[j]).astype(BF16)
            knf = kn.astype(F32)
            kv_ref[j, 0, n] = lax.dot_general((knf * t["kdec_f"]).astype(BF16), vn, trans_a, preferred_element_type=F32)
            kv_ref[j, 1, n] = lax.dot_general((knf * t["kdec_b"]).astype(BF16), vn, trans_a, preferred_element_type=F32)

    for j, t in enumerate(tabs):
        state = jnp.zeros((D, D), F32)
        for n in range(n_chunks):
            st_ref[j, 0, n] = state.astype(BF16)
            state = t["cdec_f"] * state + kv_ref[j, 0, n]
        state = jnp.zeros((D, D), F32)
        for n in reversed(range(n_chunks)):
            st_ref[j, 1, n] = state.astype(BF16)
            state = t["cdec_b"] * state + kv_ref[j, 1, n]

    for n in range(n_chunks):
        for j, t in enumerate(tabs):
            qnf = q_ref[rows(n), lanes(j)].astype(F32)
            lhs = jnp.concatenate(
                [sd_ref[j, n], (qnf * t["qdec_f"]).astype(BF16), (qnf * t["qdec_b"]).astype(BF16)], axis=1)
            rhs = jnp.concatenate([v_ref[rows(n), lanes(j)], st_ref[j, 0, n], st_ref[j, 1, n]], axis=0)
            out = jnp.dot(lhs, rhs, preferred_element_type=F32)
            mu = jnp.mean(out, axis=-1, keepdims=True)
            cen = out - mu
            var = jnp.mean(cen * cen, axis=-1, keepdims=True)
            gated = cen * lax.rsqrt(var + EPS) * gate_ref[rows(n), lanes(j)].astype(F32)
            o_ref[rows(n), lanes(j)] = gated.astype(o_ref.dtype)


def _retention(rqkv, gate, p, layer, batch, seq):
    chunk = 256 if seq % 256 == 0 else RET_CHUNK
    n_chunks = seq // chunk
    heads = 2
    groups = RET_HEADS // heads
    width = heads * RET_HD
    blk = lambda first: pl.BlockSpec((seq, width), lambda b, g: (b, first + g))
    return pl.pallas_call(
        functools.partial(_ret_kernel, seq=seq, chunk=chunk, heads=heads),
        grid=(batch, groups),
        in_specs=[
            pl.BlockSpec((None, 2, heads, SUBLANES, LANES), lambda b, g: (layer, 0, g, 0, 0)),
            blk(0), blk(groups), blk(2 * groups),
            blk(CONV_W // width),
        ],
        out_specs=blk(0),
        out_shape=jax.ShapeDtypeStruct((batch * seq, RET_W), BF16),
        scratch_shapes=[
            pltpu.VMEM((heads, chunk, chunk), F32),
            pltpu.VMEM((heads, n_chunks, chunk, chunk), BF16),
            pltpu.VMEM((heads, 2, n_chunks, RET_HD, RET_HD), F32),
            pltpu.VMEM((heads, 2, n_chunks, RET_HD, RET_HD), BF16),
        ],
        compiler_params=pltpu.CompilerParams(dimension_semantics=("arbitrary", "arbitrary")),
        name="retention",
    )(p["ret_decay"], rqkv, rqkv, rqkv, gate)


def _mla_kernel(q_ref, kt_ref, v_ref, gate_ref, o_ref, *, sub, heads):
    ones = jnp.ones((v_ref.shape[0], MLA_V_HD), BF16)
    kts = [kt_ref[j * MLA_QK_PAD:(j + 1) * MLA_QK_PAD, :] for j in range(heads)]
    vs = [jnp.concatenate([v_ref[:, j * MLA_V_HD:(j + 1) * MLA_V_HD], ones], axis=1) for j in range(heads)]
    items = [(j, r) for j in range(heads) for r in range(q_ref.shape[0] // sub)]

    def scores(j, r):
        q = q_ref[r * sub:(r + 1) * sub, j * MLA_QK_PAD:(j + 1) * MLA_QK_PAD]
        return jnp.dot(q, kts[j], preferred_element_type=F32)

    s_next = scores(*items[0])
    for i, (j, r) in enumerate(items):
        rows, cols = slice(r * sub, (r + 1) * sub), slice(j * MLA_V_HD, (j + 1) * MLA_V_HD)
        s = s_next
        if i + 1 < len(items):
            s_next = scores(*items[i + 1])
        m = jnp.max(s, axis=-1, keepdims=True)
        p = jnp.exp2(s - m)
        o = jnp.dot(p.astype(BF16), vs[j], preferred_element_type=F32)
        gated = o[:, :MLA_V_HD] / o[:, MLA_V_HD:] * gate_ref[rows, cols].astype(F32)
        o_ref[rows, cols] = gated.astype(o_ref.dtype)


def _mla(mq, mkt, mv, gate, batch, seq):
    _, sub, tq = _tiles(seq)
    nq = seq // tq
    heads = 2
    qk_w, v_w = heads * MLA_QK_PAD, heads * MLA_V_HD
    return pl.pallas_call(
        functools.partial(_mla_kernel, sub=sub, heads=heads),
        grid=(batch, MLA_HEADS // heads, nq),
        in_specs=[
            pl.BlockSpec((tq, qk_w), lambda b, g, i: (b * nq + i, g)),
            pl.BlockSpec((qk_w, seq), lambda b, g, i: (g, b)),
            pl.BlockSpec((seq, v_w), lambda b, g, i: (b, g)),
            pl.BlockSpec((tq, v_w), lambda b, g, i: (b * nq + i, (CONV_W + RET_W) // v_w + g)),
        ],
        out_specs=pl.BlockSpec((tq, v_w), lambda b, g, i: (b * nq + i, g)),
        out_shape=jax.ShapeDtypeStruct((batch * seq, MLA_W), BF16),
        compiler_params=pltpu.CompilerParams(
            dimension_semantics=("arbitrary", "arbitrary", "arbitrary"), vmem_limit_bytes=VMEM_LIMIT),
        name="mla_attention",
    )(mq, mkt, mv, gate)


def _outproj_kernel(x_ref, yc_ref, yr_ref, ym_ref, w32_ref, fg_ref, o_ref, w_ref, *, final):
    @pl.when(pl.program_id(0) == 0)
    def _():
        w_ref[...] = w32_ref[...].astype(BF16)

    acc = x_ref[...]
    acc = acc + jnp.dot(yc_ref[...], w_ref[0:CONV_W, :], preferred_element_type=F32)
    acc = acc + jnp.dot(yr_ref[...], w_ref[CONV_W:CONV_W + RET_W, :], preferred_element_type=F32)
    acc = acc + jnp.dot(ym_ref[...], w_ref[CONV_W + RET_W:, :], preferred_element_type=F32)
    if final:
        acc = _rmsnorm(acc, fg_ref[...])
    o_ref[...] = acc


def _outproj(x2, yc, yr, ym, p, layer, seq, final):
    tm = min(2 * _tiles(seq)[0], seq)
    tokens = x2.shape[0]
    row = lambda i: (i, 0)
    return pl.pallas_call(
        functools.partial(_outproj_kernel, final=final),
        grid=(tokens // tm,),
        in_specs=[
            pl.BlockSpec((tm, D_MODEL), row),
            pl.BlockSpec((tm, CONV_W), row),
            pl.BlockSpec((tm, RET_W), row),
            pl.BlockSpec((tm, MLA_W), row),
            _layer_block(layer, (D_MIX, D_MODEL)),
            pl.BlockSpec((1, D_MODEL), lambda i: (0, 0)),
        ],
        out_specs=pl.BlockSpec((tm, D_MODEL), row),
        out_shape=jax.ShapeDtypeStruct(x2.shape, F32),
        scratch_shapes=[pltpu.VMEM((D_MIX, D_MODEL), BF16)],
        compiler_params=pltpu.CompilerParams(
            dimension_semantics=("arbitrary",), vmem_limit_bytes=VMEM_LIMIT),
        name="outproj",
    )(x2, yc, yr, ym, p["w_out"], p["final_g"])


def _pad_last(w, width):
    return jnp.pad(w, [(0, 0)] * (w.ndim - 1) + [(0, width - w.shape[-1])])


def _split_w_in_kernel(wt_ref, main_ref, lat_ref, gate_ref):
    wt = wt_ref[...]
    k_rope = wt[OFF_KROPE:OFF_GATE, :]
    main_ref[...] = jnp.concatenate([wt[:OFF_QLAT, :].T, wt[OFF_KVLAT:OFF_KROPE, :].T], axis=1).astype(BF16)
    lat_ref[...] = jnp.concatenate([wt[OFF_QLAT:OFF_KVLAT, :].T,
                                    jnp.concatenate([k_rope, k_rope], axis=0).T], axis=1).astype(BF16)
    gate_ref[...] = wt[OFF_GATE:, :].T.astype(BF16)


def _split_w_in(w_in):
    depth = w_in.shape[0]
    rows = 256
    widths = (W_MAIN, MLA_Q_RANK + LANES, D_MIX)
    return pl.pallas_call(
        _split_w_in_kernel,
        grid=(depth, D_MODEL // rows),
        in_specs=[pl.BlockSpec((None, N_IN, rows), lambda l, i: (l, 0, i))],
        out_specs=[pl.BlockSpec((None, rows, w), lambda l, i: (l, i, 0)) for w in widths],
        out_shape=[jax.ShapeDtypeStruct((depth, D_MODEL, w), BF16) for w in widths],
        compiler_params=pltpu.CompilerParams(dimension_semantics=("arbitrary", "arbitrary")),
        name="split_w_in",
    )(jnp.swapaxes(w_in, 1, 2))


def _prep_params(norm_g, w_in, conv_dw_w, conv_dw_b, conv_ln_g, conv_ln_b, ret_decay_logit,
                 mla_qa_g, mla_w_uq, mla_kva_g, mla_w_ukv, w_out, final_g):
    depth = norm_g.shape[0]
    uq = mla_w_uq.reshape(depth, MLA_Q_RANK, MLA_HEADS, MLA_NOPE + MLA_ROPE)
    uq = jnp.concatenate([uq[..., :MLA_NOPE].reshape(depth, MLA_Q_RANK, MLA_HEADS * MLA_NOPE),
                          uq[..., MLA_NOPE:].reshape(depth, MLA_Q_RANK, MLA_HEADS * MLA_ROPE)], axis=-1)
    ukv = mla_w_ukv.reshape(depth, MLA_KV_RANK, MLA_HEADS, MLA_NOPE + MLA_V_HD)
    ukt = ukv[..., :MLA_NOPE].reshape(depth, MLA_KV_RANK, MLA_HEADS * MLA_NOPE).transpose(0, 2, 1)
    uv = ukv[..., MLA_NOPE:].reshape(depth, MLA_KV_RANK, MLA_W)
    vecs = [norm_g, mla_qa_g, mla_kva_g, conv_dw_b, conv_ln_g, conv_ln_b]
    vecs = jnp.stack([_pad_last(v, D_MODEL) for v in vecs] + [jnp.zeros_like(norm_g)] * (SUBLANES - len(vecs)), axis=1)
    w_main, w_lat, w_gate = _split_w_in(w_in)
    return {
        "vecs": vecs,
        "w_main": w_main,
        "w_lat": w_lat,
        "w_gate": w_gate,
        "w_uq": uq.astype(BF16),
        "w_ukt": ukt.astype(BF16),
        "w_uv": uv.astype(BF16),
        "conv_w": jnp.pad(conv_dw_w, ((0, 0), (0, 1), (0, 0))),
        "ret_decay": jnp.broadcast_to(ret_decay_logit[:, :, :, None, None], (depth, 2, RET_HEADS, SUBLANES, LANES)),
        "w_out": w_out,
        "final_g": final_g[None, :],
    }


def _rope_inputs(seq):
    cos_r, sin_r = _rope_tables(seq, RET_HD)
    half = RET_HD // 2
    sin_r = np.concatenate([-sin_r[:, :half], sin_r[:, half:]], axis=1)
    cos_m, sin_m = _rope_tables(seq, MLA_ROPE)
    half = MLA_ROPE // 2
    zeros = np.zeros((seq, half), np.float32)
    cosm = np.concatenate([cos_m, cos_m], axis=1)
    sinma = np.concatenate([-sin_m[:, :half], zeros, -sin_m[:, :half], zeros], axis=1)
    sinmb = np.concatenate([zeros, sin_m[:, half:], zeros, sin_m[:, half:]], axis=1)
    return tuple(jnp.asarray(t) for t in (cos_r, sin_r, cosm, sinma, sinmb))


def kernel(x, norm_g, w_in, conv_dw_w, conv_dw_b, conv_ln_g, conv_ln_b, ret_decay_logit,
           mla_qa_g, mla_w_uq, mla_kva_g, mla_w_ukv, w_out, final_g):
    batch, seq, d_model = x.shape
    depth = norm_g.shape[0]
    assert d_model == D_MODEL and seq % RET_CHUNK == 0
    p = _prep_params(norm_g, w_in, conv_dw_w, conv_dw_b, conv_ln_g, conv_ln_b, ret_decay_logit,
                     mla_qa_g, mla_w_uq, mla_kva_g, mla_w_ukv, w_out, final_g)
    tabs = _rope_inputs(seq)
    x2 = x.reshape(batch * seq, d_model)
    for layer in range(depth):
        hglu, rqkv, mq, mkt, mv, gate = _inproj(x2, p, layer, tabs, seq)
        yc = _conv(hglu, gate, p, layer, batch, seq)
        yr = _retention(rqkv, gate, p, layer, batch, seq)
        ym = _mla(mq, mkt, mv, gate, batch, seq)
        x2 = _outproj(x2, yc, yr, ym, p, layer, seq, final=(layer == depth - 1))
    return x2.reshape(batch, seq, d_model)
```

```python
import functools
import math

import jax
import jax.numpy as jnp
import numpy as np
from jax import lax
from jax.experimental import pallas as pl
from jax.experimental.pallas import tpu as pltpu

D_MODEL = 1024
D_MIX = 2 * D_MODEL
CONV_W = 512
CONV_K = 31
RET_W = 512
RET_HEADS = 4
RET_HD = 128
RET_CHUNK = 128
MLA_W = 1024
MLA_HEADS = 8
MLA_V_HD = 128
MLA_NOPE = 128
MLA_ROPE = 64
MLA_Q_RANK = 384
MLA_KV_RANK = 256
ROPE_BASE = 10000.0
EPS = 1e-6

OFF_RET = 2 * CONV_W
OFF_QLAT = OFF_RET + 3 * RET_W
OFF_KVLAT = OFF_QLAT + MLA_Q_RANK
OFF_KROPE = OFF_KVLAT + MLA_KV_RANK
OFF_GATE = OFF_KROPE + MLA_ROPE
N_IN = OFF_GATE + D_MIX
W_MAIN = OFF_QLAT + MLA_KV_RANK

LANES = 128
SUBLANES = 8
MLA_QK_PAD = 2 * LANES
CONV_HALO = 16
CONV_HOP = 128
CONV_WIN = CONV_HOP + 2 * CONV_HALO
VMEM_LIMIT = 56 * 1024 * 1024

BF16 = jnp.bfloat16
F32 = jnp.float32


def _tiles(seq):
    return min(512, seq), 256, min(2048, seq)


def _rope_tables(seq, dim):
    f32 = np.float32
    inv = f32(1.0) / (f32(ROPE_BASE) ** (np.arange(0, dim, 2, dtype=f32) / f32(dim)))
    ang = np.arange(seq, dtype=f32)[:, None] * inv[None, :]
    ang = np.concatenate([ang, ang], axis=-1).astype(np.float64)
    return np.cos(ang).astype(f32), np.sin(ang).astype(f32)


def _rmsnorm(x, g):
    return x * lax.rsqrt(jnp.mean(x * x, axis=-1, keepdims=True) + EPS) * g


VEC_NORM_G, VEC_QA_G, VEC_KVA_G, VEC_CONV_B, VEC_CONV_LN_G, VEC_CONV_LN_B = range(6)


def _vec(vec_ref, row, width):
    return vec_ref[row:row + 1, :width]


def _layer_block(layer, shape):
    return pl.BlockSpec((None,) + shape, lambda *_: (layer,) + (0,) * len(shape), pipeline_mode=pl.Buffered(1))


def _inproj_kernel(x_ref, vec_ref, wmain_ref, wlat_ref, wgate_ref, wuq_ref, wukt_ref, wuv_ref,
                   cosr_ref, sinr_ref, cosm_ref, sinma_ref, sinmb_ref,
                   hglu_ref, rqkv_ref, mq_ref, mkt_ref, mv_ref, gate_ref, *, sub):
    blocks = [_inproj_rows(slice(r * sub, (r + 1) * sub), x_ref, vec_ref, wmain_ref, wlat_ref, wgate_ref,
                           wuq_ref, wukt_ref, wuv_ref,
                           cosr_ref, sinr_ref, cosm_ref, sinma_ref, sinmb_ref,
                           hglu_ref, rqkv_ref, mq_ref, mkt_ref, mv_ref, gate_ref)
              for r in range(x_ref.shape[0] // sub)]
    for _ in zip(*blocks):
        pass


def _inproj_rows(rows, x_ref, vec_ref, wmain_ref, wlat_ref, wgate_ref, wuq_ref, wukt_ref, wuv_ref,
                 cosr_ref, sinr_ref, cosm_ref, sinma_ref, sinmb_ref,
                 hglu_ref, rqkv_ref, mq_ref, mkt_ref, mv_ref, gate_ref):
    h = _rmsnorm(x_ref[rows, :], _vec(vec_ref, VEC_NORM_G, D_MODEL)).astype(BF16)

    lat = jnp.dot(h, wlat_ref[...], preferred_element_type=F32)
    q_lat, k_rope = lat[:, :MLA_Q_RANK], lat[:, MLA_Q_RANK:]
    yield

    gt = jnp.dot(h, wgate_ref[...], preferred_element_type=F32)
    gate_ref[rows, :] = (gt * jax.nn.sigmoid(gt)).astype(BF16)
    yield

    u_main = jnp.dot(h, wmain_ref[...], preferred_element_type=F32)

    hglu_ref[rows, :] = (u_main[:, :CONV_W] * jax.nn.sigmoid(u_main[:, CONV_W:2 * CONV_W])).astype(BF16)

    cosr = cosr_ref[rows, :]
    sinr = sinr_ref[rows, :]
    k_scale = RET_HD ** -0.5
    for part in range(3):
        for hd in range(RET_HEADS):
            col = part * RET_W + hd * RET_HD
            blk = u_main[:, OFF_RET + col:OFF_RET + col + RET_HD]
            if part < 2:
                blk = blk * cosr + pltpu.roll(blk, RET_HD // 2, 1) * sinr
            if part == 1:
                blk = blk * k_scale
            rqkv_ref[rows, col:col + RET_HD] = blk.astype(BF16)
    yield

    cosm = cosm_ref[rows, :]
    sinma = sinma_ref[rows, :]
    sinmb = sinmb_ref[rows, :]

    def rope_pair(blk):
        return (blk * cosm + pltpu.roll(blk, LANES - MLA_ROPE // 2, 1) * sinma
                + pltpu.roll(blk, MLA_ROPE // 2, 1) * sinmb)

    qn = _rmsnorm(q_lat, _vec(vec_ref, VEC_QA_G, MLA_Q_RANK)).astype(BF16)
    q_scale = (MLA_NOPE + MLA_ROPE) ** -0.5 * math.log2(math.e)
    q_all = jnp.dot(qn, wuq_ref[...], preferred_element_type=F32) * q_scale
    rope_off = MLA_HEADS * MLA_NOPE
    lane = lax.broadcasted_iota(jnp.int32, (q_all.shape[0], LANES), 1)
    for pair in range(MLA_HEADS // 2):
        roped = rope_pair(q_all[:, rope_off + pair * LANES:rope_off + (pair + 1) * LANES])
        for j in range(2):
            hd = 2 * pair + j
            own = (lane < MLA_ROPE) if j == 0 else (lane >= MLA_ROPE)
            mq_ref[rows, hd * MLA_QK_PAD:hd * MLA_QK_PAD + LANES] = q_all[:, hd * MLA_NOPE:(hd + 1) * MLA_NOPE].astype(BF16)
            mq_ref[rows, hd * MLA_QK_PAD + LANES:(hd + 1) * MLA_QK_PAD] = jnp.where(own, roped, 0.0).astype(BF16)

    yield
    kvn = _rmsnorm(u_main[:, OFF_QLAT:W_MAIN], _vec(vec_ref, VEC_KVA_G, MLA_KV_RANK)).astype(BF16)
    k_rope_t = rope_pair(k_rope).T.astype(BF16)
    k_nope_t = lax.dot_general(wukt_ref[...], kvn, (((1,), (1,)), ((), ())),
                               preferred_element_type=F32).astype(BF16)
    for hd in range(MLA_HEADS):
        mkt_ref[hd * MLA_QK_PAD:hd * MLA_QK_PAD + LANES, rows] = k_nope_t[hd * MLA_NOPE:(hd + 1) * MLA_NOPE, :]
        mkt_ref[hd * MLA_QK_PAD + LANES:(hd + 1) * MLA_QK_PAD, rows] = k_rope_t
    mv_ref[rows, :] = jnp.dot(kvn, wuv_ref[...], preferred_element_type=F32).astype(BF16)
    yield


def _inproj(x2, p, layer, tabs, seq):
    tm, sub, _ = _tiles(seq)
    tokens = x2.shape[0]
    nseq = seq // tm
    row = lambda i: (i, 0)
    pos = lambda i: (i % nseq, 0)
    return pl.pallas_call(
        functools.partial(_inproj_kernel, sub=sub),
        grid=(tokens // tm,),
        in_specs=[
            pl.BlockSpec((tm, D_MODEL), row),
            _layer_block(layer, (SUBLANES, D_MODEL)),
            _layer_block(layer, (D_MODEL, W_MAIN)),
            _layer_block(layer, (D_MODEL, MLA_Q_RANK + LANES)),
            _layer_block(layer, (D_MODEL, D_MIX)),
            _layer_block(layer, (MLA_Q_RANK, MLA_HEADS * (MLA_NOPE + MLA_ROPE))),
            _layer_block(layer, (MLA_HEADS * MLA_NOPE, MLA_KV_RANK)),
            _layer_block(layer, (MLA_KV_RANK, MLA_W)),
        ] + [pl.BlockSpec((tm, LANES), pos)] * len(tabs),
        out_specs=[
            pl.BlockSpec((tm, CONV_W), row),
            pl.BlockSpec((tm, 3 * RET_W), row),
            pl.BlockSpec((tm, MLA_HEADS * MLA_QK_PAD), row),
            pl.BlockSpec((MLA_HEADS * MLA_QK_PAD, tm), lambda i: (0, i)),
            pl.BlockSpec((tm, MLA_W), row),
            pl.BlockSpec((tm, D_MIX), row),
        ],
        out_shape=[
            jax.ShapeDtypeStruct((tokens, CONV_W), BF16),
            jax.ShapeDtypeStruct((tokens, 3 * RET_W), BF16),
            jax.ShapeDtypeStruct((tokens, MLA_HEADS * MLA_QK_PAD), BF16),
            jax.ShapeDtypeStruct((MLA_HEADS * MLA_QK_PAD, tokens), BF16),
            jax.ShapeDtypeStruct((tokens, MLA_W), BF16),
            jax.ShapeDtypeStruct((tokens, D_MIX), BF16),
        ],
        compiler_params=pltpu.CompilerParams(
            dimension_semantics=("arbitrary",), vmem_limit_bytes=VMEM_LIMIT),
        name="inproj",
    )(x2, p["vecs"], p["w_main"], p["w_lat"], p["w_gate"], p["w_uq"], p["w_ukt"], p["w_uv"], *tabs)


def _conv_tables():
    n, half = CONV_WIN, CONV_WIN // 2

    def trig_rows(pos):
        ang = 2.0 * np.pi * ((np.arange(half)[:, None] * pos[None, :]) % n) / n
        top, bot = np.cos(ang), np.sin(ang)
        bot[0] = np.cos(2.0 * np.pi * ((half * pos) % n) / n)
        return np.concatenate([top, bot], axis=0)

    def split(m):
        hi = m.astype(BF16)
        return hi, (m - hi.astype(np.float64)).astype(BF16)

    fwd_hi, fwd_lo = split(trig_rows(np.arange(n)))
    inv = trig_rows(np.arange(CONV_HOP)).T * (2.0 / n)
    inv[:, 0] *= 0.5
    inv[:, half] *= 0.5
    inv_hi, inv_lo = split(inv)
    inv = np.concatenate([inv_hi, inv_lo, inv_hi], axis=1)
    taps = np.zeros((n, CONV_K + 1), np.float32)
    taps[:, :CONV_K] = trig_rows(n - 1 - np.arange(CONV_K))
    return jnp.asarray(fwd_hi), jnp.asarray(fwd_lo), jnp.asarray(inv), jnp.asarray(taps)


def _conv_kernel(h_ref, gate_ref, w_ref, vec_ref, fwd_hi_ref, fwd_lo_ref, inv_ref, taps_ref, o_ref,
                 pad_ref, g_ref, *, seq, seqs):
    n, half = CONV_WIN, CONV_WIN // 2
    zeros = jnp.zeros((CONV_HALO, CONV_W), BF16)
    for q in range(seqs):
        pad_ref[q, 0:CONV_HALO, :] = zeros
        pad_ref[q, CONV_HALO + seq:, :] = zeros
        pad_ref[q, CONV_HALO:CONV_HALO + seq, :] = h_ref[q * seq:(q + 1) * seq, :]

    @pl.when(pl.program_id(0) == 0)
    def _():
        g = jnp.dot(taps_ref[...], w_ref[...], preferred_element_type=F32, precision=lax.Precision.HIGHEST)
        g_cos, g_mix = g[:half], g[half:]
        first = lax.broadcasted_iota(jnp.int32, g_cos.shape, 0) == 0
        g_sin = jnp.where(first, 0.0, g_mix)
        g_ref[0] = g_cos
        g_ref[1] = -g_sin
        g_ref[2] = g_sin
        g_ref[3] = jnp.where(first, g_mix, g_cos)

    def forward(q, i):
        win = pad_ref[q, i * CONV_HOP:i * CONV_HOP + n, :]
        return (jnp.dot(fwd_hi_ref[...], win, preferred_element_type=F32)
                + jnp.dot(fwd_lo_ref[...], win, preferred_element_type=F32))

    windows = [(q, i) for q in range(seqs) for i in range(seq // CONV_HOP)]
    xs_next = forward(*windows[0])
    for w, (q, i) in enumerate(windows):
        xs = xs_next
        if w + 1 < len(windows):
            xs_next = forward(*windows[w + 1])
        a, b = xs[:half], xs[half:]
        z = jnp.concatenate([a * g_ref[0] + b * g_ref[1], a * g_ref[2] + b * g_ref[3]], axis=0)
        z_hi = z.astype(BF16)
        z_lo = (z - z_hi.astype(F32)).astype(BF16)
        acc = jnp.dot(inv_ref[...], jnp.concatenate([z_hi, z_hi, z_lo], axis=0), preferred_element_type=F32)
        acc = acc + _vec(vec_ref, VEC_CONV_B, CONV_W)
        mu = jnp.mean(acc, axis=-1, keepdims=True)
        cen = acc - mu
        var = jnp.mean(cen * cen, axis=-1, keepdims=True)
        hn = cen * lax.rsqrt(var + EPS) * _vec(vec_ref, VEC_CONV_LN_G, CONV_W) + _vec(vec_ref, VEC_CONV_LN_B, CONV_W)
        out_rows = slice(q * seq + i * CONV_HOP, q * seq + (i + 1) * CONV_HOP)
        o_ref[out_rows, :] = (hn * jax.nn.sigmoid(hn) * gate_ref[out_rows, :].astype(F32)).astype(o_ref.dtype)


def _conv(hglu, gate, p, layer, batch, seq):
    assert seq % CONV_HOP == 0
    seqs = 2 if batch % 2 == 0 else 1
    rows = seqs * seq
    tables = _conv_tables()
    whole = lambda a: pl.BlockSpec(a.shape, lambda i: (0,) * a.ndim, pipeline_mode=pl.Buffered(1))
    return pl.pallas_call(
        functools.partial(_conv_kernel, seq=seq, seqs=seqs),
        grid=(batch // seqs,),
        in_specs=[
            pl.BlockSpec((rows, CONV_W), lambda i: (i, 0)),
            pl.BlockSpec((rows, CONV_W), lambda i: (i, 0)),
            _layer_block(layer, (CONV_K + 1, CONV_W)),
            _layer_block(layer, (SUBLANES, D_MODEL)),
        ] + [whole(t) for t in tables],
        out_specs=pl.BlockSpec((rows, CONV_W), lambda i: (i, 0)),
        out_shape=jax.ShapeDtypeStruct(hglu.shape, BF16),
        scratch_shapes=[
            pltpu.VMEM((seqs, seq + 2 * CONV_HALO, CONV_W), BF16),
            pltpu.VMEM((4, CONV_WIN // 2, CONV_W), F32),
        ],
        compiler_params=pltpu.CompilerParams(dimension_semantics=("arbitrary",)),
        name="conv",
    )(hglu, gate, p["conv_w"], p["vecs"], *tables)


def _log_sigmoid(x):
    return jnp.minimum(x, 0.0) - jnp.log1p(jnp.exp(-jnp.abs(x)))


def _ret_kernel(dl_ref, q_ref, k_ref, v_ref, gate_ref, o_ref, decay_ref, tab_ref, cdec_ref, sd_ref, kv_ref, st_ref, *,
                seq, chunk, heads):
    C = chunk
    n_chunks = seq // C
    D = RET_HD
    ri = lax.broadcasted_iota(jnp.int32, (C, LANES), 0)
    idx = ri.astype(F32)
    trans_b = (((1,), (1,)), ((), ()))
    trans_a = (((0,), (0,)), ((), ()))

    def rows(n):
        return pl.ds(n * C, C)

    def lanes(j):
        return slice(j * D, (j + 1) * D)

    qdec_f, kdec_f, qdec_b, kdec_b = range(4)

    @pl.when(pl.program_id(1) == 0)
    def _():
        for j in range(heads):
            lg_f = jnp.broadcast_to(_log_sigmoid(dl_ref[0, j])[0:1, :], (C, LANES))
            lg_b = jnp.broadcast_to(_log_sigmoid(dl_ref[1, j])[0:1, :], (C, LANES))
            for c in range(C // LANES):
                diff = (ri - (lax.broadcasted_iota(jnp.int32, (C, LANES), 1) + c * LANES)).astype(F32)
                decay_ref[j, :, c * LANES:(c + 1) * LANES] = jnp.where(
                    diff >= 0.0, jnp.exp(lg_f * jnp.maximum(diff, 0.0)), jnp.exp(lg_b * jnp.maximum(-diff, 0.0)))
            tab_ref[j, qdec_f] = jnp.exp(lg_f * (idx + 1.0))
            tab_ref[j, kdec_f] = jnp.exp(lg_f * (C - 1.0 - idx))
            tab_ref[j, qdec_b] = jnp.exp(lg_b * (C - idx))
            tab_ref[j, kdec_b] = jnp.exp(lg_b * idx)
            cdec_ref[j, 0] = jnp.exp(lg_f[:D] * float(C))
            cdec_ref[j, 1] = jnp.exp(lg_b[:D] * float(C))

    for n in range(n_chunks):
        for j in range(heads):
            qn, kn, vn = q_ref[rows(n), lanes(j)], k_ref[rows(n), lanes(j)], v_ref[rows(n), lanes(j)]
            s = lax.dot_general(qn, kn, trans_b, preferred_element_type=F32)
            sd_ref[j, n] = (s * decay_ref[j]).astype(BF16)
            knf = kn.astype(F32)
            kv_ref[j, 0, n] = lax.dot_general((knf * tab_ref[j, kdec_f]).astype(BF16), vn, trans_a, preferred_element_type=F32)
            kv_ref[j, 1, n] = lax.dot_general((knf * tab_ref[j, kdec_b]).astype(BF16), vn, trans_a, preferred_element_type=F32)

    for j in range(heads):
        state = jnp.zeros((D, D), F32)
        for n in range(n_chunks):
            st_ref[j, 0, n] = state.astype(BF16)
            state = cdec_ref[j, 0] * state + kv_ref[j, 0, n]
        state = jnp.zeros((D, D), F32)
        for n in reversed(range(n_chunks)):
            st_ref[j, 1, n] = state.astype(BF16)
            state = cdec_ref[j, 1] * state + kv_ref[j, 1, n]

    for n in range(n_chunks):
        for j in range(heads):
            qnf = q_ref[rows(n), lanes(j)].astype(F32)
            lhs = jnp.concatenate(
                [sd_ref[j, n], (qnf * tab_ref[j, qdec_f]).astype(BF16), (qnf * tab_ref[j, qdec_b]).astype(BF16)], axis=1)
            rhs = jnp.concatenate([v_ref[rows(n), lanes(j)], st_ref[j, 0, n], st_ref[j, 1, n]], axis=0)
            out = jnp.dot(lhs, rhs, preferred_element_type=F32)
            mu = jnp.mean(out, axis=-1, keepdims=True)
            cen = out - mu
            var = jnp.mean(cen * cen, axis=-1, keepdims=True)
            gated = cen * lax.rsqrt(var + EPS) * gate_ref[rows(n), lanes(j)].astype(F32)
            o_ref[rows(n), lanes(j)] = gated.astype(o_ref.dtype)


def _retention(rqkv, gate, p, layer, batch, seq):
    chunk = 256 if seq % 256 == 0 else RET_CHUNK
    n_chunks = seq // chunk
    heads = 2
    groups = RET_HEADS // heads
    width = heads * RET_HD
    blk = lambda first: pl.BlockSpec((seq, width), lambda g, b: (b, first + g))
    return pl.pallas_call(
        functools.partial(_ret_kernel, seq=seq, chunk=chunk, heads=heads),
        grid=(groups, batch),
        in_specs=[
            pl.BlockSpec((None, 2, heads, SUBLANES, LANES), lambda g, b: (layer, 0, g, 0, 0)),
            blk(0), blk(groups), blk(2 * groups),
            blk(CONV_W // width),
        ],
        out_specs=blk(0),
        out_shape=jax.ShapeDtypeStruct((batch * seq, RET_W), BF16),
        scratch_shapes=[
            pltpu.VMEM((heads, chunk, chunk), F32),
            pltpu.VMEM((heads, 4, chunk, LANES), F32),
            pltpu.VMEM((heads, 2, RET_HD, LANES), F32),
            pltpu.VMEM((heads, n_chunks, chunk, chunk), BF16),
            pltpu.VMEM((heads, 2, n_chunks, RET_HD, RET_HD), F32),
            pltpu.VMEM((heads, 2, n_chunks, RET_HD, RET_HD), BF16),
        ],
        compiler_params=pltpu.CompilerParams(dimension_semantics=("arbitrary", "arbitrary")),
        name="retention",
    )(p["ret_decay"], rqkv, rqkv, rqkv, gate)


def _mla_kernel(q_ref, kt_ref, v_ref, gate_ref, o_ref, *, sub, heads):
    ones = jnp.ones((v_ref.shape[0], MLA_V_HD), BF16)
    kts = [kt_ref[j * MLA_QK_PAD:(j + 1) * MLA_QK_PAD, :] for j in range(heads)]
    vs = [jnp.concatenate([v_ref[:, j * MLA_V_HD:(j + 1) * MLA_V_HD], ones], axis=1) for j in range(heads)]
    items = [(j, r) for j in range(heads) for r in range(q_ref.shape[0] // sub)]

    def scores(j, r):
        q = q_ref[r * sub:(r + 1) * sub, j * MLA_QK_PAD:(j + 1) * MLA_QK_PAD]
        return jnp.dot(q, kts[j], preferred_element_type=F32)

    s_next = scores(*items[0])
    for i, (j, r) in enumerate(items):
        rows, cols = slice(r * sub, (r + 1) * sub), slice(j * MLA_V_HD, (j + 1) * MLA_V_HD)
        s = s_next
        if i + 1 < len(items):
            s_next = scores(*items[i + 1])
        m = jnp.max(s, axis=-1, keepdims=True)
        p = jnp.exp2(s - m)
        o = jnp.dot(p.astype(BF16), vs[j], preferred_element_type=F32)
        gated = o[:, :MLA_V_HD] / o[:, MLA_V_HD:] * gate_ref[rows, cols].astype(F32)
        o_ref[rows, cols] = gated.astype(o_ref.dtype)


def _mla(mq, mkt, mv, gate, batch, seq):
    _, sub, tq = _tiles(seq)
    nq = seq // tq
    heads = 2
    qk_w, v_w = heads * MLA_QK_PAD, heads * MLA_V_HD
    return pl.pallas_call(
        functools.partial(_mla_kernel, sub=sub, heads=heads),
        grid=(batch, MLA_HEADS // heads, nq),
        in_specs=[
            pl.BlockSpec((tq, qk_w), lambda b, g, i: (b * nq + i, g)),
            pl.BlockSpec((qk_w, seq), lambda b, g, i: (g, b)),
            pl.BlockSpec((seq, v_w), lambda b, g, i: (b, g)),
            pl.BlockSpec((tq, v_w), lambda b, g, i: (b * nq + i, (CONV_W + RET_W) // v_w + g)),
        ],
        out_specs=pl.BlockSpec((tq, v_w), lambda b, g, i: (b * nq + i, g)),
        out_shape=jax.ShapeDtypeStruct((batch * seq, MLA_W), BF16),
        compiler_params=pltpu.CompilerParams(
            dimension_semantics=("arbitrary", "arbitrary", "arbitrary"), vmem_limit_bytes=VMEM_LIMIT),
        name="mla_attention",
    )(mq, mkt, mv, gate)


def _outproj_kernel(x_ref, yc_ref, yr_ref, ym_ref, w32_ref, fg_ref, o_ref, w_ref, *, final):
    @pl.when(pl.program_id(0) == 0)
    def _():
        w_ref[...] = w32_ref[...].astype(BF16)

    acc = x_ref[...]
    acc = acc + jnp.dot(yc_ref[...], w_ref[0:CONV_W, :], preferred_element_type=F32)
    acc = acc + jnp.dot(yr_ref[...], w_ref[CONV_W:CONV_W + RET_W, :], preferred_element_type=F32)
    acc = acc + jnp.dot(ym_ref[...], w_ref[CONV_W + RET_W:, :], preferred_element_type=F32)
    if final:
        acc = _rmsnorm(acc, fg_ref[...])
    o_ref[...] = acc


def _outproj(x2, yc, yr, ym, p, layer, seq, final):
    tm = min(2 * _tiles(seq)[0], seq)
    tokens = x2.shape[0]
    row = lambda i: (i, 0)
    return pl.pallas_call(
        functools.partial(_outproj_kernel, final=final),
        grid=(tokens // tm,),
        in_specs=[
            pl.BlockSpec((tm, D_MODEL), row),
            pl.BlockSpec((tm, CONV_W), row),
            pl.BlockSpec((tm, RET_W), row),
            pl.BlockSpec((tm, MLA_W), row),
            _layer_block(layer, (D_MIX, D_MODEL)),
            pl.BlockSpec((1, D_MODEL), lambda i: (0, 0)),
        ],
        out_specs=pl.BlockSpec((tm, D_MODEL), row),
        out_shape=jax.ShapeDtypeStruct(x2.shape, F32),
        scratch_shapes=[pltpu.VMEM((D_MIX, D_MODEL), BF16)],
        compiler_params=pltpu.CompilerParams(
            dimension_semantics=("arbitrary",), vmem_limit_bytes=VMEM_LIMIT),
        name="outproj",
    )(x2, yc, yr, ym, p["w_out"], p["final_g"])


def _pad_last(w, width):
    return jnp.pad(w, [(0, 0)] * (w.ndim - 1) + [(0, width - w.shape[-1])])


def _split_w_in_kernel(wt_ref, main_ref, lat_ref, gate_ref):
    wt = wt_ref[...]
    k_rope = wt[OFF_KROPE:OFF_GATE, :]
    main_ref[...] = jnp.concatenate([wt[:OFF_QLAT, :].T, wt[OFF_KVLAT:OFF_KROPE, :].T], axis=1).astype(BF16)
    lat_ref[...] = jnp.concatenate([wt[OFF_QLAT:OFF_KVLAT, :].T,
                                    jnp.concatenate([k_rope, k_rope], axis=0).T], axis=1).astype(BF16)
    gate_ref[...] = wt[OFF_GATE:, :].T.astype(BF16)


def _split_w_in(w_in):
    depth = w_in.shape[0]
    rows = 256
    widths = (W_MAIN, MLA_Q_RANK + LANES, D_MIX)
    return pl.pallas_call(
        _split_w_in_kernel,
        grid=(depth, D_MODEL // rows),
        in_specs=[pl.BlockSpec((None, N_IN, rows), lambda l, i: (l, 0, i))],
        out_specs=[pl.BlockSpec((None, rows, w), lambda l, i: (l, i, 0)) for w in widths],
        out_shape=[jax.ShapeDtypeStruct((depth, D_MODEL, w), BF16) for w in widths],
        compiler_params=pltpu.CompilerParams(dimension_semantics=("arbitrary", "arbitrary")),
        name="split_w_in",
    )(jnp.swapaxes(w_in, 1, 2))


def _prep_params(norm_g, w_in, conv_dw_w, conv_dw_b, conv_ln_g, conv_ln_b, ret_decay_logit,
                 mla_qa_g, mla_w_uq, mla_kva_g, mla_w_ukv, w_out, final_g):
    depth = norm_g.shape[0]
    uq = mla_w_uq.reshape(depth, MLA_Q_RANK, MLA_HEADS, MLA_NOPE + MLA_ROPE)
    uq = jnp.concatenate([uq[..., :MLA_NOPE].reshape(depth, MLA_Q_RANK, MLA_HEADS * MLA_NOPE),
                          uq[..., MLA_NOPE:].reshape(depth, MLA_Q_RANK, MLA_HEADS * MLA_ROPE)], axis=-1)
    ukv = mla_w_ukv.reshape(depth, MLA_KV_RANK, MLA_HEADS, MLA_NOPE + MLA_V_HD)
    ukt = ukv[..., :MLA_NOPE].reshape(depth, MLA_KV_RANK, MLA_HEADS * MLA_NOPE).transpose(0, 2, 1)
    uv = ukv[..., MLA_NOPE:].reshape(depth, MLA_KV_RANK, MLA_W)
    vecs = [norm_g, mla_qa_g, mla_kva_g, conv_dw_b, conv_ln_g, conv_ln_b]
    vecs = jnp.stack([_pad_last(v, D_MODEL) for v in vecs] + [jnp.zeros_like(norm_g)] * (SUBLANES - len(vecs)), axis=1)
    w_main, w_lat, w_gate = _split_w_in(w_in)
    return {
        "vecs": vecs,
        "w_main": w_main,
        "w_lat": w_lat,
        "w_gate": w_gate,
        "w_uq": uq.astype(BF16),
        "w_ukt": ukt.astype(BF16),
        "w_uv": uv.astype(BF16),
        "conv_w": jnp.pad(conv_dw_w, ((0, 0), (0, 1), (0, 0))),
        "ret_decay": jnp.broadcast_to(ret_decay_logit[:, :, :, None, None], (depth, 2, RET_HEADS, SUBLANES, LANES)),
        "w_out": w_out,
        "final_g": final_g[None, :],
    }


def _rope_inputs(seq):
    cos_r, sin_r = _rope_tables(seq, RET_HD)
    half = RET_HD // 2
    sin_r = np.concatenate([-sin_r[:, :half], sin_r[:, half:]], axis=1)
    cos_m, sin_m = _rope_tables(seq, MLA_ROPE)
    half = MLA_ROPE // 2
    zeros = np.zeros((seq, half), np.float32)
    cosm = np.concatenate([cos_m, cos_m], axis=1)
    sinma = np.concatenate([-sin_m[:, :half], zeros, -sin_m[:, :half], zeros], axis=1)
    sinmb = np.concatenate([zeros, sin_m[:, half:], zeros, sin_m[:, half:]], axis=1)
    return tuple(jnp.asarray(t) for t in (cos_r, sin_r, cosm, sinma, sinmb))


def kernel(x, norm_g, w_in, conv_dw_w, conv_dw_b, conv_ln_g, conv_ln_b, ret_decay_logit,
           mla_qa_g, mla_w_uq, mla_kva_g, mla_w_ukv, w_out, final_g):
    batch, seq, d_model = x.shape
    depth = norm_g.shape[0]
    assert d_model == D_MODEL and seq % RET_CHUNK == 0
    p = _prep_params(norm_g, w_in, conv_dw_w, conv_dw_b, conv_ln_g, conv_ln_b, ret_decay_logit,
                     mla_qa_g, mla_w_uq, mla_kva_g, mla_w_ukv, w_out, final_g)
    tabs = _rope_inputs(seq)
    x2 = x.reshape(batch * seq, d_model)
    for layer in range(depth):
        hglu, rqkv, mq, mkt, mv, gate = _inproj(x2, p, layer, tabs, seq)
        yc = _conv(hglu, gate, p, layer, batch, seq)
        yr = _retention(rqkv, gate, p, layer, batch, seq)
        ym = _mla(mq, mkt, mv, gate, batch, seq)
        x2 = _outproj(x2, yc, yr, ym, p, layer, seq, final=(layer == depth - 1))
    return x2.reshape(batch, seq, d_model)
```

```python
import functools
import math

import jax
import jax.numpy as jnp
import numpy as np
from jax import lax
from jax.experimental import pallas as pl
from jax.experimental.pallas import tpu as pltpu

D_MODEL = 1024
D_MIX = 2 * D_MODEL
CONV_W = 512
CONV_K = 31
RET_W = 512
RET_HEADS = 4
RET_HD = 128
RET_CHUNK = 128
MLA_W = 1024
MLA_HEADS = 8
MLA_V_HD = 128
MLA_NOPE = 128
MLA_ROPE = 64
MLA_Q_RANK = 384
MLA_KV_RANK = 256
ROPE_BASE = 10000.0
EPS = 1e-6

OFF_RET = 2 * CONV_W
OFF_QLAT = OFF_RET + 3 * RET_W
OFF_KVLAT = OFF_QLAT + MLA_Q_RANK
OFF_KROPE = OFF_KVLAT + MLA_KV_RANK
OFF_GATE = OFF_KROPE + MLA_ROPE
N_IN = OFF_GATE + D_MIX
W_MAIN = OFF_QLAT + MLA_KV_RANK

LANES = 128
SUBLANES = 8
MLA_QK_PAD = 2 * LANES
CONV_HALO = 16
CONV_HOP = 128
CONV_WIN = CONV_HOP + 2 * CONV_HALO
VMEM_LIMIT = 56 * 1024 * 1024

BF16 = jnp.bfloat16
F32 = jnp.float32


def _tiles(seq):
    return min(512, seq), 256, min(2048, seq)


def _rope_tables(seq, dim):
    f32 = np.float32
    inv = f32(1.0) / (f32(ROPE_BASE) ** (np.arange(0, dim, 2, dtype=f32) / f32(dim)))
    ang = np.arange(seq, dtype=f32)[:, None] * inv[None, :]
    ang = np.concatenate([ang, ang], axis=-1).astype(np.float64)
    return np.cos(ang).astype(f32), np.sin(ang).astype(f32)


def _rmsnorm(x, g):
    return x * lax.rsqrt(jnp.mean(x * x, axis=-1, keepdims=True) + EPS) * g


VEC_NORM_G, VEC_QA_G, VEC_KVA_G, VEC_CONV_B, VEC_CONV_LN_G, VEC_CONV_LN_B = range(6)


def _vec(vec_ref, row, width):
    return vec_ref[row:row + 1, :width]


def _layer_block(layer, shape):
    return pl.BlockSpec((None,) + shape, lambda *_: (layer,) + (0,) * len(shape), pipeline_mode=pl.Buffered(1))


def _inproj_kernel(x_ref, vec_ref, wmain_ref, wlat_ref, wgate_ref, wuq_ref, wukt_ref, wuv_ref,
                   cosr_ref, sinr_ref, cosm_ref, sinma_ref, sinmb_ref,
                   hglu_ref, rqkv_ref, mq_ref, mkt_ref, mv_ref, gate_ref, *, sub):
    blocks = [_inproj_rows(slice(r * sub, (r + 1) * sub), x_ref, vec_ref, wmain_ref, wlat_ref, wgate_ref,
                           wuq_ref, wukt_ref, wuv_ref,
                           cosr_ref, sinr_ref, cosm_ref, sinma_ref, sinmb_ref,
                           hglu_ref, rqkv_ref, mq_ref, mkt_ref, mv_ref, gate_ref)
              for r in range(x_ref.shape[0] // sub)]
    for _ in zip(*blocks):
        pass


def _inproj_rows(rows, x_ref, vec_ref, wmain_ref, wlat_ref, wgate_ref, wuq_ref, wukt_ref, wuv_ref,
                 cosr_ref, sinr_ref, cosm_ref, sinma_ref, sinmb_ref,
                 hglu_ref, rqkv_ref, mq_ref, mkt_ref, mv_ref, gate_ref):
    h = _rmsnorm(x_ref[rows, :], _vec(vec_ref, VEC_NORM_G, D_MODEL)).astype(BF16)

    lat = jnp.dot(h, wlat_ref[...], preferred_element_type=F32)
    q_lat, k_rope = lat[:, :MLA_Q_RANK], lat[:, MLA_Q_RANK:]
    yield

    gt = jnp.dot(h, wgate_ref[...], preferred_element_type=F32)
    gate_ref[rows, :] = (gt * jax.nn.sigmoid(gt)).astype(BF16)
    yield

    u_main = jnp.dot(h, wmain_ref[...], preferred_element_type=F32)

    hglu_ref[rows, :] = (u_main[:, :CONV_W] * jax.nn.sigmoid(u_main[:, CONV_W:2 * CONV_W])).astype(BF16)

    cosr = cosr_ref[rows, :]
    sinr = sinr_ref[rows, :]
    k_scale = RET_HD ** -0.5
    for part in range(3):
        for hd in range(RET_HEADS):
            col = part * RET_W + hd * RET_HD
            blk = u_main[:, OFF_RET + col:OFF_RET + col + RET_HD]
            if part < 2:
                blk = blk * cosr + pltpu.roll(blk, RET_HD // 2, 1) * sinr
            if part == 1:
                blk = blk * k_scale
            rqkv_ref[rows, col:col + RET_HD] = blk.astype(BF16)
    yield

    cosm = cosm_ref[rows, :]
    sinma = sinma_ref[rows, :]
    sinmb = sinmb_ref[rows, :]

    def rope_pair(blk):
        return (blk * cosm + pltpu.roll(blk, LANES - MLA_ROPE // 2, 1) * sinma
                + pltpu.roll(blk, MLA_ROPE // 2, 1) * sinmb)

    qn = _rmsnorm(q_lat, _vec(vec_ref, VEC_QA_G, MLA_Q_RANK)).astype(BF16)
    q_scale = (MLA_NOPE + MLA_ROPE) ** -0.5 * math.log2(math.e)
    q_all = jnp.dot(qn, wuq_ref[...], preferred_element_type=F32) * q_scale
    rope_off = MLA_HEADS * MLA_NOPE
    lane = lax.broadcasted_iota(jnp.int32, (q_all.shape[0], LANES), 1)
    for pair in range(MLA_HEADS // 2):
        roped = rope_pair(q_all[:, rope_off + pair * LANES:rope_off + (pair + 1) * LANES])
        for j in range(2):
            hd = 2 * pair + j
            own = (lane < MLA_ROPE) if j == 0 else (lane >= MLA_ROPE)
            mq_ref[rows, hd * MLA_QK_PAD:hd * MLA_QK_PAD + LANES] = q_all[:, hd * MLA_NOPE:(hd + 1) * MLA_NOPE].astype(BF16)
            mq_ref[rows, hd * MLA_QK_PAD + LANES:(hd + 1) * MLA_QK_PAD] = jnp.where(own, roped, 0.0).astype(BF16)

    yield
    kvn = _rmsnorm(u_main[:, OFF_QLAT:W_MAIN], _vec(vec_ref, VEC_KVA_G, MLA_KV_RANK)).astype(BF16)
    k_rope_t = rope_pair(k_rope).T.astype(BF16)
    k_nope_t = lax.dot_general(wukt_ref[...], kvn, (((1,), (1,)), ((), ())),
                               preferred_element_type=F32).astype(BF16)
    for hd in range(MLA_HEADS):
        mkt_ref[hd * MLA_QK_PAD:hd * MLA_QK_PAD + LANES, rows] = k_nope_t[hd * MLA_NOPE:(hd + 1) * MLA_NOPE, :]
        mkt_ref[hd * MLA_QK_PAD + LANES:(hd + 1) * MLA_QK_PAD, rows] = k_rope_t
    mv_ref[rows, :] = jnp.dot(kvn, wuv_ref[...], preferred_element_type=F32).astype(BF16)
    yield


def _inproj(x2, p, layer, tabs, seq):
    tm, sub, _ = _tiles(seq)
    tokens = x2.shape[0]
    nseq = seq // tm
    row = lambda i: (i, 0)
    pos = lambda i: (i % nseq, 0)
    return pl.pallas_call(
        functools.partial(_inproj_kernel, sub=sub),
        grid=(tokens // tm,),
        in_specs=[
            pl.BlockSpec((tm, D_MODEL), row),
            _layer_block(layer, (SUBLANES, D_MODEL)),
            _layer_block(layer, (D_MODEL, W_MAIN)),
            _layer_block(layer, (D_MODEL, MLA_Q_RANK + LANES)),
            _layer_block(layer, (D_MODEL, D_MIX)),
            _layer_block(layer, (MLA_Q_RANK, MLA_HEADS * (MLA_NOPE + MLA_ROPE))),
            _layer_block(layer, (MLA_HEADS * MLA_NOPE, MLA_KV_RANK)),
            _layer_block(layer, (MLA_KV_RANK, MLA_W)),
        ] + [pl.BlockSpec((tm, LANES), pos)] * len(tabs),
        out_specs=[
            pl.BlockSpec((tm, CONV_W), row),
            pl.BlockSpec((tm, 3 * RET_W), row),
            pl.BlockSpec((tm, MLA_HEADS * MLA_QK_PAD), row),
            pl.BlockSpec((MLA_HEADS * MLA_QK_PAD, tm), lambda i: (0, i)),
            pl.BlockSpec((tm, MLA_W), row),
            pl.BlockSpec((tm, D_MIX), row),
        ],
        out_shape=[
            jax.ShapeDtypeStruct((tokens, CONV_W), BF16),
            jax.ShapeDtypeStruct((tokens, 3 * RET_W), BF16),
            jax.ShapeDtypeStruct((tokens, MLA_HEADS * MLA_QK_PAD), BF16),
            jax.ShapeDtypeStruct((MLA_HEADS * MLA_QK_PAD, tokens), BF16),
            jax.ShapeDtypeStruct((tokens, MLA_W), BF16),
            jax.ShapeDtypeStruct((tokens, D_MIX), BF16),
        ],
        compiler_params=pltpu.CompilerParams(
            dimension_semantics=("arbitrary",), vmem_limit_bytes=VMEM_LIMIT),
        name="inproj",
    )(x2, p["vecs"], p["w_main"], p["w_lat"], p["w_gate"], p["w_uq"], p["w_ukt"], p["w_uv"], *tabs)


def _conv_tables():
    n, half = CONV_WIN, CONV_WIN // 2

    def trig_rows(pos):
        ang = 2.0 * np.pi * ((np.arange(half)[:, None] * pos[None, :]) % n) / n
        top, bot = np.cos(ang), np.sin(ang)
        bot[0] = np.cos(2.0 * np.pi * ((half * pos) % n) / n)
        return np.concatenate([top, bot], axis=0)

    def split(m):
        hi = m.astype(BF16)
        return hi, (m - hi.astype(np.float64)).astype(BF16)

    fwd_hi, fwd_lo = split(trig_rows(np.arange(n)))
    inv = trig_rows(np.arange(CONV_HOP)).T * (2.0 / n)
    inv[:, 0] *= 0.5
    inv[:, half] *= 0.5
    inv_hi, inv_lo = split(inv)
    inv = np.concatenate([inv_hi, inv_lo, inv_hi], axis=1)
    taps = np.zeros((n, CONV_K + 1), np.float32)
    taps[:, :CONV_K] = trig_rows(n - 1 - np.arange(CONV_K))
    return jnp.asarray(fwd_hi), jnp.asarray(fwd_lo), jnp.asarray(inv), jnp.asarray(taps)


def _conv_kernel(h_ref, gate_ref, w_ref, vec_ref, fwd_hi_ref, fwd_lo_ref, inv_ref, taps_ref, o_ref,
                 pad_ref, g_ref, *, seq, seqs):
    n, half = CONV_WIN, CONV_WIN // 2
    zeros = jnp.zeros((CONV_HALO, CONV_W), BF16)
    for q in range(seqs):
        pad_ref[q, 0:CONV_HALO, :] = zeros
        pad_ref[q, CONV_HALO + seq:, :] = zeros
        pad_ref[q, CONV_HALO:CONV_HALO + seq, :] = h_ref[q * seq:(q + 1) * seq, :]

    @pl.when(pl.program_id(0) == 0)
    def _():
        g = jnp.dot(taps_ref[...], w_ref[...], preferred_element_type=F32, precision=lax.Precision.HIGHEST)
        g_cos, g_mix = g[:half], g[half:]
        first = lax.broadcasted_iota(jnp.int32, g_cos.shape, 0) == 0
        g_sin = jnp.where(first, 0.0, g_mix)
        g_ref[0] = g_cos
        g_ref[1] = -g_sin
        g_ref[2] = g_sin
        g_ref[3] = jnp.where(first, g_mix, g_cos)

    def forward(q, i):
        win = pad_ref[q, i * CONV_HOP:i * CONV_HOP + n, :]
        return (jnp.dot(fwd_hi_ref[...], win, preferred_element_type=F32)
                + jnp.dot(fwd_lo_ref[...], win, preferred_element_type=F32))

    windows = [(q, i) for q in range(seqs) for i in range(seq // CONV_HOP)]
    xs_next = forward(*windows[0])
    for w, (q, i) in enumerate(windows):
        xs = xs_next
        if w + 1 < len(windows):
            xs_next = forward(*windows[w + 1])
        a, b = xs[:half], xs[half:]
        z = jnp.concatenate([a * g_ref[0] + b * g_ref[1], a * g_ref[2] + b * g_ref[3]], axis=0)
        z_hi = z.astype(BF16)
        z_lo = (z - z_hi.astype(F32)).astype(BF16)
        acc = jnp.dot(inv_ref[...], jnp.concatenate([z_hi, z_hi, z_lo], axis=0), preferred_element_type=F32)
        acc = acc + _vec(vec_ref, VEC_CONV_B, CONV_W)
        mu = jnp.mean(acc, axis=-1, keepdims=True)
        cen = acc - mu
        var = jnp.mean(cen * cen, axis=-1, keepdims=True)
        hn = cen * lax.rsqrt(var + EPS) * _vec(vec_ref, VEC_CONV_LN_G, CONV_W) + _vec(vec_ref, VEC_CONV_LN_B, CONV_W)
        out_rows = slice(q * seq + i * CONV_HOP, q * seq + (i + 1) * CONV_HOP)
        o_ref[out_rows, :] = (hn * jax.nn.sigmoid(hn) * gate_ref[out_rows, :].astype(F32)).astype(o_ref.dtype)


def _conv(hglu, gate, p, layer, batch, seq):
    assert seq % CONV_HOP == 0
    seqs = 2 if batch % 2 == 0 else 1
    rows = seqs * seq
    tables = _conv_tables()
    whole = lambda a: pl.BlockSpec(a.shape, lambda i: (0,) * a.ndim, pipeline_mode=pl.Buffered(1))
    return pl.pallas_call(
        functools.partial(_conv_kernel, seq=seq, seqs=seqs),
        grid=(batch // seqs,),
        in_specs=[
            pl.BlockSpec((rows, CONV_W), lambda i: (i, 0)),
            pl.BlockSpec((rows, CONV_W), lambda i: (i, 0)),
            _layer_block(layer, (CONV_K + 1, CONV_W)),
            _layer_block(layer, (SUBLANES, D_MODEL)),
        ] + [whole(t) for t in tables],
        out_specs=pl.BlockSpec((rows, CONV_W), lambda i: (i, 0)),
        out_shape=jax.ShapeDtypeStruct(hglu.shape, BF16),
        scratch_shapes=[
            pltpu.VMEM((seqs, seq + 2 * CONV_HALO, CONV_W), BF16),
            pltpu.VMEM((4, CONV_WIN // 2, CONV_W), F32),
        ],
        compiler_params=pltpu.CompilerParams(dimension_semantics=("arbitrary",)),
        name="conv",
    )(hglu, gate, p["conv_w"], p["vecs"], *tables)


def _log_sigmoid(x):
    return jnp.minimum(x, 0.0) - jnp.log1p(jnp.exp(-jnp.abs(x)))


def _ret_kernel(dl_ref, q_ref, k_ref, v_ref, gate_ref, o_ref, decay_ref, tab_ref, cdec_ref, sd_ref, kv_ref, st_ref, *,
                seq, chunk, heads):
    C = chunk
    n_chunks = seq // C
    D = RET_HD
    ri = lax.broadcasted_iota(jnp.int32, (C, LANES), 0)
    idx = ri.astype(F32)
    trans_b = (((1,), (1,)), ((), ()))
    trans_a = (((0,), (0,)), ((), ()))

    def rows(n):
        return pl.ds(n * C, C)

    def lanes(j):
        return slice(j * D, (j + 1) * D)

    qdec_f, kdec_f, qdec_b, kdec_b = range(4)

    @pl.when(pl.program_id(1) == 0)
    def _():
        for j in range(heads):
            lg_f = jnp.broadcast_to(_log_sigmoid(dl_ref[0, j])[0:1, :], (C, LANES))
            lg_b = jnp.broadcast_to(_log_sigmoid(dl_ref[1, j])[0:1, :], (C, LANES))
            for c in range(C // LANES):
                diff = (ri - (lax.broadcasted_iota(jnp.int32, (C, LANES), 1) + c * LANES)).astype(F32)
                decay_ref[j, :, c * LANES:(c + 1) * LANES] = jnp.where(
                    diff >= 0.0, jnp.exp(lg_f * jnp.maximum(diff, 0.0)), jnp.exp(lg_b * jnp.maximum(-diff, 0.0)))
            tab_ref[j, qdec_f] = jnp.exp(lg_f * (idx + 1.0))
            tab_ref[j, kdec_f] = jnp.exp(lg_f * (C - 1.0 - idx))
            tab_ref[j, qdec_b] = jnp.exp(lg_b * (C - idx))
            tab_ref[j, kdec_b] = jnp.exp(lg_b * idx)
            cdec_ref[j, 0] = jnp.exp(lg_f[:D] * float(C))
            cdec_ref[j, 1] = jnp.exp(lg_b[:D] * float(C))

    for n in range(n_chunks):
        for j in range(heads):
            qn, kn, vn = q_ref[rows(n), lanes(j)], k_ref[rows(n), lanes(j)], v_ref[rows(n), lanes(j)]
            s = lax.dot_general(qn, kn, trans_b, preferred_element_type=F32)
            sd_ref[j, n] = (s * decay_ref[j]).astype(BF16)
            knf = kn.astype(F32)
            kv_ref[j, 0, n] = lax.dot_general((knf * tab_ref[j, kdec_f]).astype(BF16), vn, trans_a, preferred_element_type=F32)
            kv_ref[j, 1, n] = lax.dot_general((knf * tab_ref[j, kdec_b]).astype(BF16), vn, trans_a, preferred_element_type=F32)

    for j in range(heads):
        state = jnp.zeros((D, D), F32)
        for n in range(n_chunks):
            st_ref[j, 0, n] = state.astype(BF16)
            state = cdec_ref[j, 0] * state + kv_ref[j, 0, n]
        state = jnp.zeros((D, D), F32)
        for n in reversed(range(n_chunks)):
            st_ref[j, 1, n] = state.astype(BF16)
            state = cdec_ref[j, 1] * state + kv_ref[j, 1, n]

    for n in range(n_chunks):
        for j in range(heads):
            qnf = q_ref[rows(n), lanes(j)].astype(F32)
            lhs = jnp.concatenate(
                [sd_ref[j, n], (qnf * tab_ref[j, qdec_f]).astype(BF16), (qnf * tab_ref[j, qdec_b]).astype(BF16)], axis=1)
            rhs = jnp.concatenate([v_ref[rows(n), lanes(j)], st_ref[j, 0, n], st_ref[j, 1, n]], axis=0)
            out = jnp.dot(lhs, rhs, preferred_element_type=F32)
            mu = jnp.mean(out, axis=-1, keepdims=True)
            cen = out - mu
            var = jnp.mean(cen * cen, axis=-1, keepdims=True)
            gated = cen * lax.rsqrt(var + EPS) * gate_ref[rows(n), lanes(j)].astype(F32)
            o_ref[rows(n), lanes(j)] = gated.astype(o_ref.dtype)


def _retention(rqkv, gate, p, layer, batch, seq):
    chunk = 256 if seq % 256 == 0 else RET_CHUNK
    n_chunks = seq // chunk
    heads = 2
    groups = RET_HEADS // heads
    width = heads * RET_HD
    blk = lambda first: pl.BlockSpec((seq, width), lambda g, b: (b, first + g))
    return pl.pallas_call(
        functools.partial(_ret_kernel, seq=seq, chunk=chunk, heads=heads),
        grid=(groups, batch),
        in_specs=[
            pl.BlockSpec((None, 2, heads, SUBLANES, LANES), lambda g, b: (layer, 0, g, 0, 0)),
            blk(0), blk(groups), blk(2 * groups),
            blk(CONV_W // width),
        ],
        out_specs=blk(0),
        out_shape=jax.ShapeDtypeStruct((batch * seq, RET_W), BF16),
        scratch_shapes=[
            pltpu.VMEM((heads, chunk, chunk), F32),
            pltpu.VMEM((heads, 4, chunk, LANES), F32),
            pltpu.VMEM((heads, 2, RET_HD, LANES), F32),
            pltpu.VMEM((heads, n_chunks, chunk, chunk), BF16),
            pltpu.VMEM((heads, 2, n_chunks, RET_HD, RET_HD), F32),
            pltpu.VMEM((heads, 2, n_chunks, RET_HD, RET_HD), BF16),
        ],
        compiler_params=pltpu.CompilerParams(dimension_semantics=("arbitrary", "arbitrary")),
        name="retention",
    )(p["ret_decay"], rqkv, rqkv, rqkv, gate)


def _mla_kernel(q_ref, kt_ref, v_ref, gate_ref, o_ref, *, sub, heads):
    ones = jnp.ones((v_ref.shape[0], MLA_V_HD), BF16)
    kts = [kt_ref[j * MLA_QK_PAD:(j + 1) * MLA_QK_PAD, :] for j in range(heads)]
    vs = [jnp.concatenate([v_ref[:, j * MLA_V_HD:(j + 1) * MLA_V_HD], ones], axis=1) for j in range(heads)]
    items = [(j, r) for j in range(heads) for r in range(q_ref.shape[0] // sub)]

    def scores(j, r):
        q = q_ref[r * sub:(r + 1) * sub, j * MLA_QK_PAD:(j + 1) * MLA_QK_PAD]
        return jnp.dot(q, kts[j], preferred_element_type=F32)

    s_next = scores(*items[0])
    for i, (j, r) in enumerate(items):
        rows, cols = slice(r * sub, (r + 1) * sub), slice(j * MLA_V_HD, (j + 1) * MLA_V_HD)
        s = s_next
        if i + 1 < len(items):
            s_next = scores(*items[i + 1])
        m = jnp.max(s, axis=-1, keepdims=True)
        p = jnp.exp2(s - m)
        o = jnp.dot(p.astype(BF16), vs[j], preferred_element_type=F32)
        gated = o[:, :MLA_V_HD] / o[:, MLA_V_HD:] * gate_ref[rows, cols].astype(F32)
        o_ref[rows, cols] = gated.astype(o_ref.dtype)


def _mla(mq, mkt, mv, gate, batch, seq):
    _, sub, tq = _tiles(seq)
    nq = seq // tq
    heads = 4
    qk_w, v_w = heads * MLA_QK_PAD, heads * MLA_V_HD
    return pl.pallas_call(
        functools.partial(_mla_kernel, sub=sub, heads=heads),
        grid=(batch, MLA_HEADS // heads, nq),
        in_specs=[
            pl.BlockSpec((tq, qk_w), lambda b, g, i: (b * nq + i, g)),
            pl.BlockSpec((qk_w, seq), lambda b, g, i: (g, b)),
            pl.BlockSpec((seq, v_w), lambda b, g, i: (b, g)),
            pl.BlockSpec((tq, v_w), lambda b, g, i: (b * nq + i, (CONV_W + RET_W) // v_w + g)),
        ],
        out_specs=pl.BlockSpec((tq, v_w), lambda b, g, i: (b * nq + i, g)),
        out_shape=jax.ShapeDtypeStruct((batch * seq, MLA_W), BF16),
        compiler_params=pltpu.CompilerParams(
            dimension_semantics=("arbitrary", "arbitrary", "arbitrary"), vmem_limit_bytes=VMEM_LIMIT),
        name="mla_attention",
    )(mq, mkt, mv, gate)


def _outproj_kernel(x_ref, yc_ref, yr_ref, ym_ref, w32_ref, fg_ref, o_ref, w_ref, *, final):
    @pl.when(pl.program_id(0) == 0)
    def _():
        w_ref[...] = w32_ref[...].astype(BF16)

    acc = x_ref[...]
    acc = acc + jnp.dot(yc_ref[...], w_ref[0:CONV_W, :], preferred_element_type=F32)
    acc = acc + jnp.dot(yr_ref[...], w_ref[CONV_W:CONV_W + RET_W, :], preferred_element_type=F32)
    acc = acc + jnp.dot(ym_ref[...], w_ref[CONV_W + RET_W:, :], preferred_element_type=F32)
    if final:
        acc = _rmsnorm(acc, fg_ref[...])
    o_ref[...] = acc


def _outproj(x2, yc, yr, ym, p, layer, seq, final):
    tm = min(2 * _tiles(seq)[0], seq)
    tokens = x2.shape[0]
    row = lambda i: (i, 0)
    return pl.pallas_call(
        functools.partial(_outproj_kernel, final=final),
        grid=(tokens // tm,),
        in_specs=[
            pl.BlockSpec((tm, D_MODEL), row),
            pl.BlockSpec((tm, CONV_W), row),
            pl.BlockSpec((tm, RET_W), row),
            pl.BlockSpec((tm, MLA_W), row),
            _layer_block(layer, (D_MIX, D_MODEL)),
            pl.BlockSpec((1, D_MODEL), lambda i: (0, 0)),
        ],
        out_specs=pl.BlockSpec((tm, D_MODEL), row),
        out_shape=jax.ShapeDtypeStruct(x2.shape, F32),
        scratch_shapes=[pltpu.VMEM((D_MIX, D_MODEL), BF16)],
        compiler_params=pltpu.CompilerParams(
            dimension_semantics=("arbitrary",), vmem_limit_bytes=VMEM_LIMIT),
        name="outproj",
    )(x2, yc, yr, ym, p["w_out"], p["final_g"])


def _pad_last(w, width):
    return jnp.pad(w, [(0, 0)] * (w.ndim - 1) + [(0, width - w.shape[-1])])


def _split_w_in_kernel(wt_ref, main_ref, lat_ref, gate_ref):
    wt = wt_ref[...]
    k_rope = wt[OFF_KROPE:OFF_GATE, :]
    main_ref[...] = jnp.concatenate([wt[:OFF_QLAT, :].T, wt[OFF_KVLAT:OFF_KROPE, :].T], axis=1).astype(BF16)
    lat_ref[...] = jnp.concatenate([wt[OFF_QLAT:OFF_KVLAT, :].T,
                                    jnp.concatenate([k_rope, k_rope], axis=0).T], axis=1).astype(BF16)
    gate_ref[...] = wt[OFF_GATE:, :].T.astype(BF16)


def _split_w_in(w_in):
    depth = w_in.shape[0]
    rows = 256
    widths = (W_MAIN, MLA_Q_RANK + LANES, D_MIX)
    return pl.pallas_call(
        _split_w_in_kernel,
        grid=(depth, D_MODEL // rows),
        in_specs=[pl.BlockSpec((None, N_IN, rows), lambda l, i: (l, 0, i))],
        out_specs=[pl.BlockSpec((None, rows, w), lambda l, i: (l, i, 0)) for w in widths],
        out_shape=[jax.ShapeDtypeStruct((depth, D_MODEL, w), BF16) for w in widths],
        compiler_params=pltpu.CompilerParams(dimension_semantics=("arbitrary", "arbitrary")),
        name="split_w_in",
    )(jnp.swapaxes(w_in, 1, 2))


def _prep_params(norm_g, w_in, conv_dw_w, conv_dw_b, conv_ln_g, conv_ln_b, ret_decay_logit,
                 mla_qa_g, mla_w_uq, mla_kva_g, mla_w_ukv, w_out, final_g):
    depth = norm_g.shape[0]
    uq = mla_w_uq.reshape(depth, MLA_Q_RANK, MLA_HEADS, MLA_NOPE + MLA_ROPE)
    uq = jnp.concatenate([uq[..., :MLA_NOPE].reshape(depth, MLA_Q_RANK, MLA_HEADS * MLA_NOPE),
                          uq[..., MLA_NOPE:].reshape(depth, MLA_Q_RANK, MLA_HEADS * MLA_ROPE)], axis=-1)
    ukv = mla_w_ukv.reshape(depth, MLA_KV_RANK, MLA_HEADS, MLA_NOPE + MLA_V_HD)
    ukt = ukv[..., :MLA_NOPE].reshape(depth, MLA_KV_RANK, MLA_HEADS * MLA_NOPE).transpose(0, 2, 1)
    uv = ukv[..., MLA_NOPE:].reshape(depth, MLA_KV_RANK, MLA_W)
    vecs = [norm_g, mla_qa_g, mla_kva_g, conv_dw_b, conv_ln_g, conv_ln_b]
    vecs = jnp.stack([_pad_last(v, D_MODEL) for v in vecs] + [jnp.zeros_like(norm_g)] * (SUBLANES - len(vecs)), axis=1)
    w_main, w_lat, w_gate = _split_w_in(w_in)
    return {
        "vecs": vecs,
        "w_main": w_main,
        "w_lat": w_lat,
        "w_gate": w_gate,
        "w_uq": uq.astype(BF16),
        "w_ukt": ukt.astype(BF16),
        "w_uv": uv.astype(BF16),
        "conv_w": jnp.pad(conv_dw_w, ((0, 0), (0, 1), (0, 0))),
        "ret_decay": jnp.broadcast_to(ret_decay_logit[:, :, :, None, None], (depth, 2, RET_HEADS, SUBLANES, LANES)),
        "w_out": w_out,
        "final_g": final_g[None, :],
    }


def _rope_inputs(seq):
    cos_r, sin_r = _rope_tables(seq, RET_HD)
    half = RET_HD // 2
    sin_r = np.concatenate([-sin_r[:, :half], sin_r[:, half:]], axis=1)
    cos_m, sin_m = _rope_tables(seq, MLA_ROPE)
    half = MLA_ROPE // 2
    zeros = np.zeros((seq, half), np.float32)
    cosm = np.concatenate([cos_m, cos_m], axis=1)
    sinma = np.concatenate([-sin_m[:, :half], zeros, -sin_m[:, :half], zeros], axis=1)
    sinmb = np.concatenate([zeros, sin_m[:, half:], zeros, sin_m[:, half:]], axis=1)
    return tuple(jnp.asarray(t) for t in (cos_r, sin_r, cosm, sinma, sinmb))


def kernel(x, norm_g, w_in, conv_dw_w, conv_dw_b, conv_ln_g, conv_ln_b, ret_decay_logit,
           mla_qa_g, mla_w_uq, mla_kva_g, mla_w_ukv, w_out, final_g):
    batch, seq, d_model = x.shape
    depth = norm_g.shape[0]
    assert d_model == D_MODEL and seq % RET_CHUNK == 0
    p = _prep_params(norm_g, w_in, conv_dw_w, conv_dw_b, conv_ln_g, conv_ln_b, ret_decay_logit,
                     mla_qa_g, mla_w_uq, mla_kva_g, mla_w_ukv, w_out, final_g)
    tabs = _rope_inputs(seq)
    x2 = x.reshape(batch * seq, d_model)
    for layer in range(depth):
        hglu, rqkv, mq, mkt, mv, gate = _inproj(x2, p, layer, tabs, seq)
        yc = _conv(hglu, gate, p, layer, batch, seq)
        yr = _retention(rqkv, gate, p, layer, batch, seq)
        ym = _mla(mq, mkt, mv, gate, batch, seq)
        x2 = _outproj(x2, yc, yr, ym, p, layer, seq, final=(layer == depth - 1))
    return x2.reshape(batch, seq, d_model)
```

```python
import functools
import itertools
import math

import jax
import jax.numpy as jnp
import numpy as np
from jax import lax
from jax.experimental import pallas as pl
from jax.experimental.pallas import tpu as pltpu

D_MODEL = 1024
D_MIX = 2 * D_MODEL
CONV_W = 512
CONV_K = 31
RET_W = 512
RET_HEADS = 4
RET_HD = 128
RET_CHUNK = 128
MLA_W = 1024
MLA_HEADS = 8
MLA_V_HD = 128
MLA_NOPE = 128
MLA_ROPE = 64
MLA_Q_RANK = 384
MLA_KV_RANK = 256
ROPE_BASE = 10000.0
EPS = 1e-6

OFF_RET = 2 * CONV_W
OFF_QLAT = OFF_RET + 3 * RET_W
OFF_KVLAT = OFF_QLAT + MLA_Q_RANK
OFF_KROPE = OFF_KVLAT + MLA_KV_RANK
OFF_GATE = OFF_KROPE + MLA_ROPE
N_IN = OFF_GATE + D_MIX
W_MAIN = OFF_QLAT + MLA_KV_RANK

LANES = 128
SUBLANES = 8
MLA_QK_PAD = 2 * LANES
CONV_HALO = 16
CONV_HOP = 128
CONV_WIN = CONV_HOP + 2 * CONV_HALO
VMEM_LIMIT = 56 * 1024 * 1024

BF16 = jnp.bfloat16
F32 = jnp.float32


def _tiles(seq):
    return min(512, seq), 256, min(2048, seq)


def _rope_tables(seq, dim):
    f32 = np.float32
    inv = f32(1.0) / (f32(ROPE_BASE) ** (np.arange(0, dim, 2, dtype=f32) / f32(dim)))
    ang = np.arange(seq, dtype=f32)[:, None] * inv[None, :]
    ang = np.concatenate([ang, ang], axis=-1).astype(np.float64)
    return np.cos(ang).astype(f32), np.sin(ang).astype(f32)


def _rmsnorm(x, g):
    return x * lax.rsqrt(jnp.mean(x * x, axis=-1, keepdims=True) + EPS) * g


VEC_NORM_G, VEC_QA_G, VEC_KVA_G, VEC_CONV_B, VEC_CONV_LN_G, VEC_CONV_LN_B = range(6)


def _vec(vec_ref, row, width):
    return vec_ref[row:row + 1, :width]


def _layer_block(layer, shape):
    return pl.BlockSpec((None,) + shape, lambda *_: (layer,) + (0,) * len(shape), pipeline_mode=pl.Buffered(1))


def _inproj_kernel(x_ref, vec_ref, wmain_ref, wlat_ref, wgate_ref, wuq_ref, wukt_ref, wuv_ref,
                   cosr_ref, sinr_ref, cosm_ref, sinma_ref, sinmb_ref,
                   hglu_ref, rqkv_ref, mq_ref, mkt_ref, mv_ref, gate_ref, *, sub):
    blocks = [_inproj_rows(slice(r * sub, (r + 1) * sub), x_ref, vec_ref, wmain_ref, wlat_ref, wgate_ref,
                           wuq_ref, wukt_ref, wuv_ref,
                           cosr_ref, sinr_ref, cosm_ref, sinma_ref, sinmb_ref,
                           hglu_ref, rqkv_ref, mq_ref, mkt_ref, mv_ref, gate_ref)
              for r in range(x_ref.shape[0] // sub)]
    for _ in zip(*blocks):
        pass


def _inproj_rows(rows, x_ref, vec_ref, wmain_ref, wlat_ref, wgate_ref, wuq_ref, wukt_ref, wuv_ref,
                 cosr_ref, sinr_ref, cosm_ref, sinma_ref, sinmb_ref,
                 hglu_ref, rqkv_ref, mq_ref, mkt_ref, mv_ref, gate_ref):
    h = _rmsnorm(x_ref[rows, :], _vec(vec_ref, VEC_NORM_G, D_MODEL)).astype(BF16)

    lat = jnp.dot(h, wlat_ref[...], preferred_element_type=F32)
    q_lat, k_rope = lat[:, :MLA_Q_RANK], lat[:, MLA_Q_RANK:]
    yield

    gt = jnp.dot(h, wgate_ref[...], preferred_element_type=F32)
    gate_ref[rows, :] = (gt * jax.nn.sigmoid(gt)).astype(BF16)
    yield

    u_main = jnp.dot(h, wmain_ref[...], preferred_element_type=F32)

    hglu_ref[rows, :] = (u_main[:, :CONV_W] * jax.nn.sigmoid(u_main[:, CONV_W:2 * CONV_W])).astype(BF16)

    cosr = cosr_ref[rows, :]
    sinr = sinr_ref[rows, :]
    k_scale = RET_HD ** -0.5
    for part in range(3):
        for hd in range(RET_HEADS):
            col = part * RET_W + hd * RET_HD
            blk = u_main[:, OFF_RET + col:OFF_RET + col + RET_HD]
            if part < 2:
                blk = blk * cosr + pltpu.roll(blk, RET_HD // 2, 1) * sinr
            if part == 1:
                blk = blk * k_scale
            rqkv_ref[rows, col:col + RET_HD] = blk.astype(BF16)
    yield

    cosm = cosm_ref[rows, :]
    sinma = sinma_ref[rows, :]
    sinmb = sinmb_ref[rows, :]

    def rope_pair(blk):
        return (blk * cosm + pltpu.roll(blk, LANES - MLA_ROPE // 2, 1) * sinma
                + pltpu.roll(blk, MLA_ROPE // 2, 1) * sinmb)

    qn = _rmsnorm(q_lat, _vec(vec_ref, VEC_QA_G, MLA_Q_RANK)).astype(BF16)
    q_scale = (MLA_NOPE + MLA_ROPE) ** -0.5 * math.log2(math.e)
    q_all = jnp.dot(qn, wuq_ref[...], preferred_element_type=F32) * q_scale
    rope_off = MLA_HEADS * MLA_NOPE
    lane = lax.broadcasted_iota(jnp.int32, (q_all.shape[0], LANES), 1)
    for pair in range(MLA_HEADS // 2):
        roped = rope_pair(q_all[:, rope_off + pair * LANES:rope_off + (pair + 1) * LANES])
        for j in range(2):
            hd = 2 * pair + j
            own = (lane < MLA_ROPE) if j == 0 else (lane >= MLA_ROPE)
            mq_ref[rows, hd * MLA_QK_PAD:hd * MLA_QK_PAD + LANES] = q_all[:, hd * MLA_NOPE:(hd + 1) * MLA_NOPE].astype(BF16)
            mq_ref[rows, hd * MLA_QK_PAD + LANES:(hd + 1) * MLA_QK_PAD] = jnp.where(own, roped, 0.0).astype(BF16)

    yield
    kvn = _rmsnorm(u_main[:, OFF_QLAT:W_MAIN], _vec(vec_ref, VEC_KVA_G, MLA_KV_RANK)).astype(BF16)
    k_rope_t = rope_pair(k_rope).T.astype(BF16)
    k_nope_t = lax.dot_general(wukt_ref[...], kvn, (((1,), (1,)), ((), ())),
                               preferred_element_type=F32).astype(BF16)
    for hd in range(MLA_HEADS):
        mkt_ref[hd * MLA_QK_PAD:hd * MLA_QK_PAD + LANES, rows] = k_nope_t[hd * MLA_NOPE:(hd + 1) * MLA_NOPE, :]
        mkt_ref[hd * MLA_QK_PAD + LANES:(hd + 1) * MLA_QK_PAD, rows] = k_rope_t
    mv_ref[rows, :] = jnp.dot(kvn, wuv_ref[...], preferred_element_type=F32).astype(BF16)
    yield


def _inproj(x2, p, layer, tabs, seq):
    tm, sub, _ = _tiles(seq)
    tokens = x2.shape[0]
    nseq = seq // tm
    row = lambda i: (i, 0)
    pos = lambda i: (i % nseq, 0)
    return pl.pallas_call(
        functools.partial(_inproj_kernel, sub=sub),
        grid=(tokens // tm,),
        in_specs=[
            pl.BlockSpec((tm, D_MODEL), row),
            _layer_block(layer, (SUBLANES, D_MODEL)),
            _layer_block(layer, (D_MODEL, W_MAIN)),
            _layer_block(layer, (D_MODEL, MLA_Q_RANK + LANES)),
            _layer_block(layer, (D_MODEL, D_MIX)),
            _layer_block(layer, (MLA_Q_RANK, MLA_HEADS * (MLA_NOPE + MLA_ROPE))),
            _layer_block(layer, (MLA_HEADS * MLA_NOPE, MLA_KV_RANK)),
            _layer_block(layer, (MLA_KV_RANK, MLA_W)),
        ] + [pl.BlockSpec((tm, LANES), pos)] * len(tabs),
        out_specs=[
            pl.BlockSpec((tm, CONV_W), row),
            pl.BlockSpec((tm, 3 * RET_W), row),
            pl.BlockSpec((tm, MLA_HEADS * MLA_QK_PAD), row),
            pl.BlockSpec((MLA_HEADS * MLA_QK_PAD, tm), lambda i: (0, i)),
            pl.BlockSpec((tm, MLA_W), row),
            pl.BlockSpec((tm, D_MIX), row),
        ],
        out_shape=[
            jax.ShapeDtypeStruct((tokens, CONV_W), BF16),
            jax.ShapeDtypeStruct((tokens, 3 * RET_W), BF16),
            jax.ShapeDtypeStruct((tokens, MLA_HEADS * MLA_QK_PAD), BF16),
            jax.ShapeDtypeStruct((MLA_HEADS * MLA_QK_PAD, tokens), BF16),
            jax.ShapeDtypeStruct((tokens, MLA_W), BF16),
            jax.ShapeDtypeStruct((tokens, D_MIX), BF16),
        ],
        compiler_params=pltpu.CompilerParams(
            dimension_semantics=("arbitrary",), vmem_limit_bytes=VMEM_LIMIT),
        name="inproj",
    )(x2, p["vecs"], p["w_main"], p["w_lat"], p["w_gate"], p["w_uq"], p["w_ukt"], p["w_uv"], *tabs)


def _conv_tables():
    n, half = CONV_WIN, CONV_WIN // 2

    def trig_rows(pos):
        ang = 2.0 * np.pi * ((np.arange(half)[:, None] * pos[None, :]) % n) / n
        top, bot = np.cos(ang), np.sin(ang)
        bot[0] = np.cos(2.0 * np.pi * ((half * pos) % n) / n)
        return np.concatenate([top, bot], axis=0)

    def split(m):
        hi = m.astype(BF16)
        return hi, (m - hi.astype(np.float64)).astype(BF16)

    fwd_hi, fwd_lo = split(trig_rows(np.arange(n)))
    inv = trig_rows(np.arange(CONV_HOP)).T * (2.0 / n)
    inv[:, 0] *= 0.5
    inv[:, half] *= 0.5
    inv_hi, inv_lo = split(inv)
    inv = np.concatenate([inv_hi, inv_lo, inv_hi], axis=1)
    taps = np.zeros((n, CONV_K + 1), np.float32)
    taps[:, :CONV_K] = trig_rows(n - 1 - np.arange(CONV_K))
    return jnp.asarray(fwd_hi), jnp.asarray(fwd_lo), jnp.asarray(inv), jnp.asarray(taps)


def _conv_body(h_ref, gate_ref, w_ref, vec_ref, fwd_hi_ref, fwd_lo_ref, inv_ref, taps_ref, o_ref,
               pad_ref, g_ref, *, seq, seqs, first):
    n, half = CONV_WIN, CONV_WIN // 2
    zeros = jnp.zeros((CONV_HALO, CONV_W), BF16)
    for q in range(seqs):
        pad_ref[q, 0:CONV_HALO, :] = zeros
        pad_ref[q, CONV_HALO + seq:, :] = zeros
        pad_ref[q, CONV_HALO:CONV_HALO + seq, :] = h_ref[q * seq:(q + 1) * seq, :]

    @pl.when(first)
    def _():
        g = jnp.dot(taps_ref[...], w_ref[...], preferred_element_type=F32, precision=lax.Precision.HIGHEST)
        g_cos, g_mix = g[:half], g[half:]
        first = lax.broadcasted_iota(jnp.int32, g_cos.shape, 0) == 0
        g_sin = jnp.where(first, 0.0, g_mix)
        g_ref[0] = g_cos
        g_ref[1] = -g_sin
        g_ref[2] = g_sin
        g_ref[3] = jnp.where(first, g_mix, g_cos)

    def forward(q, i):
        win = pad_ref[q, i * CONV_HOP:i * CONV_HOP + n, :]
        return (jnp.dot(fwd_hi_ref[...], win, preferred_element_type=F32)
                + jnp.dot(fwd_lo_ref[...], win, preferred_element_type=F32))

    windows = [(q, i) for q in range(seqs) for i in range(seq // CONV_HOP)]
    xs_next = forward(*windows[0])
    for w, (q, i) in enumerate(windows):
        xs = xs_next
        if w + 1 < len(windows):
            xs_next = forward(*windows[w + 1])
        a, b = xs[:half], xs[half:]
        z = jnp.concatenate([a * g_ref[0] + b * g_ref[1], a * g_ref[2] + b * g_ref[3]], axis=0)
        z_hi = z.astype(BF16)
        z_lo = (z - z_hi.astype(F32)).astype(BF16)
        acc = jnp.dot(inv_ref[...], jnp.concatenate([z_hi, z_hi, z_lo], axis=0), preferred_element_type=F32)
        acc = acc + _vec(vec_ref, VEC_CONV_B, CONV_W)
        mu = jnp.mean(acc, axis=-1, keepdims=True)
        cen = acc - mu
        var = jnp.mean(cen * cen, axis=-1, keepdims=True)
        hn = cen * lax.rsqrt(var + EPS) * _vec(vec_ref, VEC_CONV_LN_G, CONV_W) + _vec(vec_ref, VEC_CONV_LN_B, CONV_W)
        out_rows = slice(q * seq + i * CONV_HOP, q * seq + (i + 1) * CONV_HOP)
        o_ref[out_rows, :] = (hn * jax.nn.sigmoid(hn) * gate_ref[out_rows, :].astype(F32)).astype(o_ref.dtype)
        yield


def _log_sigmoid(x):
    return jnp.minimum(x, 0.0) - jnp.log1p(jnp.exp(-jnp.abs(x)))


def _ret_body(dl_ref, q_ref, k_ref, v_ref, gate_ref, o_ref, decay_ref, tab_ref, cdec_ref, sd_ref, kv_ref, st_ref, *,
              seq, chunk, heads, first):
    C = chunk
    n_chunks = seq // C
    D = RET_HD
    ri = lax.broadcasted_iota(jnp.int32, (C, LANES), 0)
    idx = ri.astype(F32)
    trans_b = (((1,), (1,)), ((), ()))
    trans_a = (((0,), (0,)), ((), ()))

    def rows(n):
        return pl.ds(n * C, C)

    def lanes(j):
        return slice(j * D, (j + 1) * D)

    qdec_f, kdec_f, qdec_b, kdec_b = range(4)

    @pl.when(first)
    def _():
        for j in range(heads):
            lg_f = jnp.broadcast_to(_log_sigmoid(dl_ref[0, j])[0:1, :], (C, LANES))
            lg_b = jnp.broadcast_to(_log_sigmoid(dl_ref[1, j])[0:1, :], (C, LANES))
            for c in range(C // LANES):
                diff = (ri - (lax.broadcasted_iota(jnp.int32, (C, LANES), 1) + c * LANES)).astype(F32)
                decay_ref[j, :, c * LANES:(c + 1) * LANES] = jnp.where(
                    diff >= 0.0, jnp.exp(lg_f * jnp.maximum(diff, 0.0)), jnp.exp(lg_b * jnp.maximum(-diff, 0.0)))
            tab_ref[j, qdec_f] = jnp.exp(lg_f * (idx + 1.0))
            tab_ref[j, kdec_f] = jnp.exp(lg_f * (C - 1.0 - idx))
            tab_ref[j, qdec_b] = jnp.exp(lg_b * (C - idx))
            tab_ref[j, kdec_b] = jnp.exp(lg_b * idx)
            cdec_ref[j, 0] = jnp.exp(lg_f[:D] * float(C))
            cdec_ref[j, 1] = jnp.exp(lg_b[:D] * float(C))

    for n in range(n_chunks):
        for j in range(heads):
            qn, kn, vn = q_ref[rows(n), lanes(j)], k_ref[rows(n), lanes(j)], v_ref[rows(n), lanes(j)]
            s = lax.dot_general(qn, kn, trans_b, preferred_element_type=F32)
            sd_ref[j, n] = (s * decay_ref[j]).astype(BF16)
            knf = kn.astype(F32)
            kv_ref[j, 0, n] = lax.dot_general((knf * tab_ref[j, kdec_f]).astype(BF16), vn, trans_a, preferred_element_type=F32)
            kv_ref[j, 1, n] = lax.dot_general((knf * tab_ref[j, kdec_b]).astype(BF16), vn, trans_a, preferred_element_type=F32)
        yield

    for j in range(heads):
        state = jnp.zeros((D, D), F32)
        for n in range(n_chunks):
            st_ref[j, 0, n] = state.astype(BF16)
            state = cdec_ref[j, 0] * state + kv_ref[j, 0, n]
        state = jnp.zeros((D, D), F32)
        for n in reversed(range(n_chunks)):
            st_ref[j, 1, n] = state.astype(BF16)
            state = cdec_ref[j, 1] * state + kv_ref[j, 1, n]
    yield

    for n in range(n_chunks):
        for j in range(heads):
            qnf = q_ref[rows(n), lanes(j)].astype(F32)
            lhs = jnp.concatenate(
                [sd_ref[j, n], (qnf * tab_ref[j, qdec_f]).astype(BF16), (qnf * tab_ref[j, qdec_b]).astype(BF16)], axis=1)
            rhs = jnp.concatenate([v_ref[rows(n), lanes(j)], st_ref[j, 0, n], st_ref[j, 1, n]], axis=0)
            out = jnp.dot(lhs, rhs, preferred_element_type=F32)
            mu = jnp.mean(out, axis=-1, keepdims=True)
            cen = out - mu
            var = jnp.mean(cen * cen, axis=-1, keepdims=True)
            gated = cen * lax.rsqrt(var + EPS) * gate_ref[rows(n), lanes(j)].astype(F32)
            o_ref[rows(n), lanes(j)] = gated.astype(o_ref.dtype)
        yield


def _branches_kernel(hglu_ref, rqkv_ref, gate_ref, w_ref, vec_ref, dl_ref, fwd_hi_ref, fwd_lo_ref, inv_ref, taps_ref,
                     yc_ref, yr_ref, pad_ref, g_ref, decay_ref, tab_ref, cdec_ref, sd_ref, kv_ref, st_ref, *,
                     seq, chunk, heads):
    first = pl.program_id(0) == 0
    width = heads * RET_HD

    def cols(start, g):
        return pl.ds(start + g * width, width)

    bodies = [_conv_body(hglu_ref, gate_ref.at[:, pl.ds(0, CONV_W)], w_ref, vec_ref, fwd_hi_ref, fwd_lo_ref, inv_ref,
                         taps_ref, yc_ref, pad_ref, g_ref, seq=seq, seqs=1, first=first)]
    for g in range(RET_HEADS // heads):
        bodies.append(_ret_body(
            dl_ref.at[:, pl.ds(g * heads, heads)], rqkv_ref.at[:, cols(0, g)], rqkv_ref.at[:, cols(RET_W, g)],
            rqkv_ref.at[:, cols(2 * RET_W, g)], gate_ref.at[:, cols(CONV_W, g)], yr_ref.at[:, cols(0, g)],
            decay_ref.at[g], tab_ref.at[g], cdec_ref.at[g], sd_ref.at[g], kv_ref.at[g], st_ref.at[g],
            seq=seq, chunk=chunk, heads=heads, first=first))
    for _ in itertools.zip_longest(*bodies):
        pass


def _branches(hglu, rqkv, gate, p, layer, batch, seq):
    assert seq % CONV_HOP == 0
    chunk = 256 if seq % 256 == 0 else RET_CHUNK
    n_chunks = seq // chunk
    heads = 2
    groups = RET_HEADS // heads
    tables = _conv_tables()
    whole = lambda a: pl.BlockSpec(a.shape, lambda i: (0,) * a.ndim, pipeline_mode=pl.Buffered(1))
    row = lambda width: pl.BlockSpec((seq, width), lambda i: (i, 0))
    return pl.pallas_call(
        functools.partial(_branches_kernel, seq=seq, chunk=chunk, heads=heads),
        grid=(batch,),
        in_specs=[
            row(CONV_W),
            row(3 * RET_W),
            row(CONV_W + RET_W),
            _layer_block(layer, (CONV_K + 1, CONV_W)),
            _layer_block(layer, (SUBLANES, D_MODEL)),
            _layer_block(layer, (2, RET_HEADS, SUBLANES, LANES)),
        ] + [whole(t) for t in tables],
        out_specs=[row(CONV_W), row(RET_W)],
        out_shape=[jax.ShapeDtypeStruct((batch * seq, CONV_W), BF16), jax.ShapeDtypeStruct((batch * seq, RET_W), BF16)],
        scratch_shapes=[
            pltpu.VMEM((1, seq + 2 * CONV_HALO, CONV_W), BF16),
            pltpu.VMEM((4, CONV_WIN // 2, CONV_W), F32),
            pltpu.VMEM((groups, heads, chunk, chunk), F32),
            pltpu.VMEM((groups, heads, 4, chunk, LANES), F32),
            pltpu.VMEM((groups, heads, 2, RET_HD, LANES), F32),
            pltpu.VMEM((groups, heads, n_chunks, chunk, chunk), BF16),
            pltpu.VMEM((groups, heads, 2, n_chunks, RET_HD, RET_HD), F32),
            pltpu.VMEM((groups, heads, 2, n_chunks, RET_HD, RET_HD), BF16),
        ],
        compiler_params=pltpu.CompilerParams(dimension_semantics=("arbitrary",), vmem_limit_bytes=VMEM_LIMIT),
        name="conv_retention",
    )(hglu, rqkv, gate, p["conv_w"], p["vecs"], p["ret_decay"], *tables)


def _mla_kernel(q_ref, kt_ref, v_ref, gate_ref, o_ref, *, sub, heads):
    ones = jnp.ones((v_ref.shape[0], MLA_V_HD), BF16)
    kts = [kt_ref[j * MLA_QK_PAD:(j + 1) * MLA_QK_PAD, :] for j in range(heads)]
    vs = [jnp.concatenate([v_ref[:, j * MLA_V_HD:(j + 1) * MLA_V_HD], ones], axis=1) for j in range(heads)]
    items = [(j, r) for j in range(heads) for r in range(q_ref.shape[0] // sub)]

    def scores(j, r):
        q = q_ref[r * sub:(r + 1) * sub, j * MLA_QK_PAD:(j + 1) * MLA_QK_PAD]
        return jnp.dot(q, kts[j], preferred_element_type=F32)

    s_next = scores(*items[0])
    for i, (j, r) in enumerate(items):
        rows, cols = slice(r * sub, (r + 1) * sub), slice(j * MLA_V_HD, (j + 1) * MLA_V_HD)
        s = s_next
        if i + 1 < len(items):
            s_next = scores(*items[i + 1])
        m = jnp.max(s, axis=-1, keepdims=True)
        p = jnp.exp2(s - m)
        o = jnp.dot(p.astype(BF16), vs[j], preferred_element_type=F32)
        gated = o[:, :MLA_V_HD] / o[:, MLA_V_HD:] * gate_ref[rows, cols].astype(F32)
        o_ref[rows, cols] = gated.astype(o_ref.dtype)


def _mla(mq, mkt, mv, gate, batch, seq):
    _, sub, tq = _tiles(seq)
    nq = seq // tq
    heads = 2
    qk_w, v_w = heads * MLA_QK_PAD, heads * MLA_V_HD
    return pl.pallas_call(
        functools.partial(_mla_kernel, sub=sub, heads=heads),
        grid=(batch, MLA_HEADS // heads, nq),
        in_specs=[
            pl.BlockSpec((tq, qk_w), lambda b, g, i: (b * nq + i, g)),
            pl.BlockSpec((qk_w, seq), lambda b, g, i: (g, b)),
            pl.BlockSpec((seq, v_w), lambda b, g, i: (b, g)),
            pl.BlockSpec((tq, v_w), lambda b, g, i: (b * nq + i, (CONV_W + RET_W) // v_w + g)),
        ],
        out_specs=pl.BlockSpec((tq, v_w), lambda b, g, i: (b * nq + i, g)),
        out_shape=jax.ShapeDtypeStruct((batch * seq, MLA_W), BF16),
        compiler_params=pltpu.CompilerParams(
            dimension_semantics=("arbitrary", "arbitrary", "arbitrary"), vmem_limit_bytes=VMEM_LIMIT),
        name="mla_attention",
    )(mq, mkt, mv, gate)


def _outproj_kernel(x_ref, yc_ref, yr_ref, ym_ref, w32_ref, fg_ref, o_ref, w_ref, *, final):
    @pl.when(pl.program_id(0) == 0)
    def _():
        w_ref[...] = w32_ref[...].astype(BF16)

    acc = x_ref[...]
    acc = acc + jnp.dot(yc_ref[...], w_ref[0:CONV_W, :], preferred_element_type=F32)
    acc = acc + jnp.dot(yr_ref[...], w_ref[CONV_W:CONV_W + RET_W, :], preferred_element_type=F32)
    acc = acc + jnp.dot(ym_ref[...], w_ref[CONV_W + RET_W:, :], preferred_element_type=F32)
    if final:
        acc = _rmsnorm(acc, fg_ref[...])
    o_ref[...] = acc


def _outproj(x2, yc, yr, ym, p, layer, seq, final):
    tm = min(2 * _tiles(seq)[0], seq)
    tokens = x2.shape[0]
    row = lambda i: (i, 0)
    return pl.pallas_call(
        functools.partial(_outproj_kernel, final=final),
        grid=(tokens // tm,),
        in_specs=[
            pl.BlockSpec((tm, D_MODEL), row),
            pl.BlockSpec((tm, CONV_W), row),
            pl.BlockSpec((tm, RET_W), row),
            pl.BlockSpec((tm, MLA_W), row),
            _layer_block(layer, (D_MIX, D_MODEL)),
            pl.BlockSpec((1, D_MODEL), lambda i: (0, 0)),
        ],
        out_specs=pl.BlockSpec((tm, D_MODEL), row),
        out_shape=jax.ShapeDtypeStruct(x2.shape, F32),
        scratch_shapes=[pltpu.VMEM((D_MIX, D_MODEL), BF16)],
        compiler_params=pltpu.CompilerParams(
            dimension_semantics=("arbitrary",), vmem_limit_bytes=VMEM_LIMIT),
        name="outproj",
    )(x2, yc, yr, ym, p["w_out"], p["final_g"])


def _pad_last(w, width):
    return jnp.pad(w, [(0, 0)] * (w.ndim - 1) + [(0, width - w.shape[-1])])


def _split_w_in_kernel(wt_ref, main_ref, lat_ref, gate_ref):
    wt = wt_ref[...]
    k_rope = wt[OFF_KROPE:OFF_GATE, :]
    main_ref[...] = jnp.concatenate([wt[:OFF_QLAT, :].T, wt[OFF_KVLAT:OFF_KROPE, :].T], axis=1).astype(BF16)
    lat_ref[...] = jnp.concatenate([wt[OFF_QLAT:OFF_KVLAT, :].T,
                                    jnp.concatenate([k_rope, k_rope], axis=0).T], axis=1).astype(BF16)
    gate_ref[...] = wt[OFF_GATE:, :].T.astype(BF16)


def _split_w_in(w_in):
    depth = w_in.shape[0]
    rows = 256
    widths = (W_MAIN, MLA_Q_RANK + LANES, D_MIX)
    return pl.pallas_call(
        _split_w_in_kernel,
        grid=(depth, D_MODEL // rows),
        in_specs=[pl.BlockSpec((None, N_IN, rows), lambda l, i: (l, 0, i))],
        out_specs=[pl.BlockSpec((None, rows, w), lambda l, i: (l, i, 0)) for w in widths],
        out_shape=[jax.ShapeDtypeStruct((depth, D_MODEL, w), BF16) for w in widths],
        compiler_params=pltpu.CompilerParams(dimension_semantics=("arbitrary", "arbitrary")),
        name="split_w_in",
    )(jnp.swapaxes(w_in, 1, 2))


def _prep_params(norm_g, w_in, conv_dw_w, conv_dw_b, conv_ln_g, conv_ln_b, ret_decay_logit,
                 mla_qa_g, mla_w_uq, mla_kva_g, mla_w_ukv, w_out, final_g):
    depth = norm_g.shape[0]
    uq = mla_w_uq.reshape(depth, MLA_Q_RANK, MLA_HEADS, MLA_NOPE + MLA_ROPE)
    uq = jnp.concatenate([uq[..., :MLA_NOPE].reshape(depth, MLA_Q_RANK, MLA_HEADS * MLA_NOPE),
                          uq[..., MLA_NOPE:].reshape(depth, MLA_Q_RANK, MLA_HEADS * MLA_ROPE)], axis=-1)
    ukv = mla_w_ukv.reshape(depth, MLA_KV_RANK, MLA_HEADS, MLA_NOPE + MLA_V_HD)
    ukt = ukv[..., :MLA_NOPE].reshape(depth, MLA_KV_RANK, MLA_HEADS * MLA_NOPE).transpose(0, 2, 1)
    uv = ukv[..., MLA_NOPE:].reshape(depth, MLA_KV_RANK, MLA_W)
    vecs = [norm_g, mla_qa_g, mla_kva_g, conv_dw_b, conv_ln_g, conv_ln_b]
    vecs = jnp.stack([_pad_last(v, D_MODEL) for v in vecs] + [jnp.zeros_like(norm_g)] * (SUBLANES - len(vecs)), axis=1)
    w_main, w_lat, w_gate = _split_w_in(w_in)
    return {
        "vecs": vecs,
        "w_main": w_main,
        "w_lat": w_lat,
        "w_gate": w_gate,
        "w_uq": uq.astype(BF16),
        "w_ukt": ukt.astype(BF16),
        "w_uv": uv.astype(BF16),
        "conv_w": jnp.pad(conv_dw_w, ((0, 0), (0, 1), (0, 0))),
        "ret_decay": jnp.broadcast_to(ret_decay_logit[:, :, :, None, None], (depth, 2, RET_HEADS, SUBLANES, LANES)),
        "w_out": w_out,
        "final_g": final_g[None, :],
    }


def _rope_inputs(seq):
    cos_r, sin_r = _rope_tables(seq, RET_HD)
    half = RET_HD // 2
    sin_r = np.concatenate([-sin_r[:, :half], sin_r[:, half:]], axis=1)
    cos_m, sin_m = _rope_tables(seq, MLA_ROPE)
    half = MLA_ROPE // 2
    zeros = np.zeros((seq, half), np.float32)
    cosm = np.concatenate([cos_m, cos_m], axis=1)
    sinma = np.concatenate([-sin_m[:, :half], zeros, -sin_m[:, :half], zeros], axis=1)
    sinmb = np.concatenate([zeros, sin_m[:, half:], zeros, sin_m[:, half:]], axis=1)
    return tuple(jnp.asarray(t) for t in (cos_r, sin_r, cosm, sinma, sinmb))


def kernel(x, norm_g, w_in, conv_dw_w, conv_dw_b, conv_ln_g, conv_ln_b, ret_decay_logit,
           mla_qa_g, mla_w_uq, mla_kva_g, mla_w_ukv, w_out, final_g):
    batch, seq, d_model = x.shape
    depth = norm_g.shape[0]
    assert d_model == D_MODEL and seq % RET_CHUNK == 0
    p = _prep_params(norm_g, w_in, conv_dw_w, conv_dw_b, conv_ln_g, conv_ln_b, ret_decay_logit,
                     mla_qa_g, mla_w_uq, mla_kva_g, mla_w_ukv, w_out, final_g)
    tabs = _rope_inputs(seq)
    x2 = x.reshape(batch * seq, d_model)
    for layer in range(depth):
        hglu, rqkv, mq, mkt, mv, gate = _inproj(x2, p, layer, tabs, seq)
        yc, yr = _branches(hglu, rqkv, gate, p, layer, batch, seq)
        ym = _mla(mq, mkt, mv, gate, batch, seq)
        x2 = _outproj(x2, yc, yr, ym, p, layer, seq, final=(layer == depth - 1))
    return x2.reshape(batch, seq, d_model)
```

```python
import functools
import itertools
import math

import jax
import jax.numpy as jnp
import numpy as np
from jax import lax
from jax.experimental import pallas as pl
from jax.experimental.pallas import tpu as pltpu

D_MODEL = 1024
D_MIX = 2 * D_MODEL
CONV_W = 512
CONV_K = 31
RET_W = 512
RET_HEADS = 4
RET_HD = 128
RET_CHUNK = 128
MLA_W = 1024
MLA_HEADS = 8
MLA_V_HD = 128
MLA_NOPE = 128
MLA_ROPE = 64
MLA_Q_RANK = 384
MLA_KV_RANK = 256
ROPE_BASE = 10000.0
EPS = 1e-6

OFF_RET = 2 * CONV_W
OFF_QLAT = OFF_RET + 3 * RET_W
OFF_KVLAT = OFF_QLAT + MLA_Q_RANK
OFF_KROPE = OFF_KVLAT + MLA_KV_RANK
OFF_GATE = OFF_KROPE + MLA_ROPE
N_IN = OFF_GATE + D_MIX
W_MAIN = OFF_QLAT + MLA_KV_RANK

LANES = 128
SUBLANES = 8
MLA_QK_PAD = 2 * LANES
CONV_HALO = 16
CONV_HOP = 128
CONV_WIN = CONV_HOP + 2 * CONV_HALO
VMEM_LIMIT = 56 * 1024 * 1024

BF16 = jnp.bfloat16
F32 = jnp.float32


def _tiles(seq):
    return min(512, seq), 256, min(2048, seq)


def _rope_tables(seq, dim):
    f32 = np.float32
    inv = f32(1.0) / (f32(ROPE_BASE) ** (np.arange(0, dim, 2, dtype=f32) / f32(dim)))
    ang = np.arange(seq, dtype=f32)[:, None] * inv[None, :]
    ang = np.concatenate([ang, ang], axis=-1).astype(np.float64)
    return np.cos(ang).astype(f32), np.sin(ang).astype(f32)


def _rmsnorm(x, g):
    return x * lax.rsqrt(jnp.mean(x * x, axis=-1, keepdims=True) + EPS) * g


VEC_NORM_G, VEC_QA_G, VEC_KVA_G, VEC_CONV_B, VEC_CONV_LN_G, VEC_CONV_LN_B = range(6)


def _vec(vec_ref, row, width):
    return vec_ref[row:row + 1, :width]


def _layer_block(layer, shape):
    return pl.BlockSpec((None,) + shape, lambda *_: (layer,) + (0,) * len(shape), pipeline_mode=pl.Buffered(1))


def _inproj_kernel(x_ref, vec_ref, wmain_ref, wlat_ref, wgate_ref, wuq_ref, wukt_ref, wuv_ref,
                   cosr_ref, sinr_ref, cosm_ref, sinma_ref, sinmb_ref,
                   hglu_ref, rqkv_ref, mq_ref, mkt_ref, mv_ref, gate_ref, *, sub):
    blocks = [_inproj_rows(slice(r * sub, (r + 1) * sub), x_ref, vec_ref, wmain_ref, wlat_ref, wgate_ref,
                           wuq_ref, wukt_ref, wuv_ref,
                           cosr_ref, sinr_ref, cosm_ref, sinma_ref, sinmb_ref,
                           hglu_ref, rqkv_ref, mq_ref, mkt_ref, mv_ref, gate_ref)
              for r in range(x_ref.shape[0] // sub)]
    for _ in zip(*blocks):
        pass


def _inproj_rows(rows, x_ref, vec_ref, wmain_ref, wlat_ref, wgate_ref, wuq_ref, wukt_ref, wuv_ref,
                 cosr_ref, sinr_ref, cosm_ref, sinma_ref, sinmb_ref,
                 hglu_ref, rqkv_ref, mq_ref, mkt_ref, mv_ref, gate_ref):
    h = _rmsnorm(x_ref[rows, :], _vec(vec_ref, VEC_NORM_G, D_MODEL)).astype(BF16)

    lat = jnp.dot(h, wlat_ref[...], preferred_element_type=F32)
    q_lat, k_rope = lat[:, :MLA_Q_RANK], lat[:, MLA_Q_RANK:]
    yield

    gt = jnp.dot(h, wgate_ref[...], preferred_element_type=F32)
    gate_ref[rows, :] = (gt * jax.nn.sigmoid(gt)).astype(BF16)
    yield

    u_main = jnp.dot(h, wmain_ref[...], preferred_element_type=F32)

    hglu_ref[rows, :] = (u_main[:, :CONV_W] * jax.nn.sigmoid(u_main[:, CONV_W:2 * CONV_W])).astype(BF16)

    cosr = cosr_ref[rows, :]
    sinr = sinr_ref[rows, :]
    k_scale = RET_HD ** -0.5
    for part in range(3):
        for hd in range(RET_HEADS):
            col = part * RET_W + hd * RET_HD
            blk = u_main[:, OFF_RET + col:OFF_RET + col + RET_HD]
            if part < 2:
                blk = blk * cosr + pltpu.roll(blk, RET_HD // 2, 1) * sinr
            if part == 1:
                blk = blk * k_scale
            rqkv_ref[rows, col:col + RET_HD] = blk.astype(BF16)
    yield

    cosm = cosm_ref[rows, :]
    sinma = sinma_ref[rows, :]
    sinmb = sinmb_ref[rows, :]

    def rope_pair(blk):
        return (blk * cosm + pltpu.roll(blk, LANES - MLA_ROPE // 2, 1) * sinma
                + pltpu.roll(blk, MLA_ROPE // 2, 1) * sinmb)

    qn = _rmsnorm(q_lat, _vec(vec_ref, VEC_QA_G, MLA_Q_RANK)).astype(BF16)
    q_scale = (MLA_NOPE + MLA_ROPE) ** -0.5 * math.log2(math.e)
    q_all = jnp.dot(qn, wuq_ref[...], preferred_element_type=F32) * q_scale
    rope_off = MLA_HEADS * MLA_NOPE
    lane = lax.broadcasted_iota(jnp.int32, (q_all.shape[0], LANES), 1)
    for pair in range(MLA_HEADS // 2):
        roped = rope_pair(q_all[:, rope_off + pair * LANES:rope_off + (pair + 1) * LANES])
        for j in range(2):
            hd = 2 * pair + j
            own = (lane < MLA_ROPE) if j == 0 else (lane >= MLA_ROPE)
            mq_ref[rows, hd * MLA_QK_PAD:hd * MLA_QK_PAD + LANES] = q_all[:, hd * MLA_NOPE:(hd + 1) * MLA_NOPE].astype(BF16)
            mq_ref[rows, hd * MLA_QK_PAD + LANES:(hd + 1) * MLA_QK_PAD] = jnp.where(own, roped, 0.0).astype(BF16)

    yield
    kvn = _rmsnorm(u_main[:, OFF_QLAT:W_MAIN], _vec(vec_ref, VEC_KVA_G, MLA_KV_RANK)).astype(BF16)
    k_rope_t = rope_pair(k_rope).T.astype(BF16)
    k_nope_t = lax.dot_general(wukt_ref[...], kvn, (((1,), (1,)), ((), ())),
                               preferred_element_type=F32).astype(BF16)
    for hd in range(MLA_HEADS):
        mkt_ref[hd * MLA_QK_PAD:hd * MLA_QK_PAD + LANES, rows] = k_nope_t[hd * MLA_NOPE:(hd + 1) * MLA_NOPE, :]
        mkt_ref[hd * MLA_QK_PAD + LANES:(hd + 1) * MLA_QK_PAD, rows] = k_rope_t
    mv_ref[rows, :] = jnp.dot(kvn, wuv_ref[...], preferred_element_type=F32).astype(BF16)
    yield


def _inproj(x2, p, layer, tabs, seq):
    tm, sub, _ = _tiles(seq)
    tokens = x2.shape[0]
    nseq = seq // tm
    row = lambda i: (i, 0)
    pos = lambda i: (i % nseq, 0)
    return pl.pallas_call(
        functools.partial(_inproj_kernel, sub=sub),
        grid=(tokens // tm,),
        in_specs=[
            pl.BlockSpec((tm, D_MODEL), row),
            _layer_block(layer, (SUBLANES, D_MODEL)),
            _layer_block(layer, (D_MODEL, W_MAIN)),
            _layer_block(layer, (D_MODEL, MLA_Q_RANK + LANES)),
            _layer_block(layer, (D_MODEL, D_MIX)),
            _layer_block(layer, (MLA_Q_RANK, MLA_HEADS * (MLA_NOPE + MLA_ROPE))),
            _layer_block(layer, (MLA_HEADS * MLA_NOPE, MLA_KV_RANK)),
            _layer_block(layer, (MLA_KV_RANK, MLA_W)),
        ] + [pl.BlockSpec((tm, LANES), pos)] * len(tabs),
        out_specs=[
            pl.BlockSpec((tm, CONV_W), row),
            pl.BlockSpec((tm, 3 * RET_W), row),
            pl.BlockSpec((tm, MLA_HEADS * MLA_QK_PAD), row),
            pl.BlockSpec((MLA_HEADS * MLA_QK_PAD, tm), lambda i: (0, i)),
            pl.BlockSpec((tm, MLA_W), row),
            pl.BlockSpec((tm, D_MIX), row),
        ],
        out_shape=[
            jax.ShapeDtypeStruct((tokens, CONV_W), BF16),
            jax.ShapeDtypeStruct((tokens, 3 * RET_W), BF16),
            jax.ShapeDtypeStruct((tokens, MLA_HEADS * MLA_QK_PAD), BF16),
            jax.ShapeDtypeStruct((MLA_HEADS * MLA_QK_PAD, tokens), BF16),
            jax.ShapeDtypeStruct((tokens, MLA_W), BF16),
            jax.ShapeDtypeStruct((tokens, D_MIX), BF16),
        ],
        compiler_params=pltpu.CompilerParams(
            dimension_semantics=("arbitrary",), vmem_limit_bytes=VMEM_LIMIT),
        name="inproj",
    )(x2, p["vecs"], p["w_main"], p["w_lat"], p["w_gate"], p["w_uq"], p["w_ukt"], p["w_uv"], *tabs)


def _conv_tables():
    n, half = CONV_WIN, CONV_WIN // 2

    def trig_rows(pos):
        ang = 2.0 * np.pi * ((np.arange(half)[:, None] * pos[None, :]) % n) / n
        top, bot = np.cos(ang), np.sin(ang)
        bot[0] = np.cos(2.0 * np.pi * ((half * pos) % n) / n)
        return np.concatenate([top, bot], axis=0)

    def split(m):
        hi = m.astype(BF16)
        return hi, (m - hi.astype(np.float64)).astype(BF16)

    fwd_hi, fwd_lo = split(trig_rows(np.arange(n)))
    inv = trig_rows(np.arange(CONV_HOP)).T * (2.0 / n)
    inv[:, 0] *= 0.5
    inv[:, half] *= 0.5
    inv_hi, inv_lo = split(inv)
    inv = np.concatenate([inv_hi, inv_lo, inv_hi], axis=1)
    taps = np.zeros((n, CONV_K + 1), np.float32)
    taps[:, :CONV_K] = trig_rows(n - 1 - np.arange(CONV_K))
    return jnp.asarray(fwd_hi), jnp.asarray(fwd_lo), jnp.asarray(inv), jnp.asarray(taps)


def _conv_body(h_ref, gate_ref, w_ref, vec_ref, fwd_hi_ref, fwd_lo_ref, inv_ref, taps_ref, o_ref,
               pad_ref, g_ref, *, seq, seqs, first):
    n, half = CONV_WIN, CONV_WIN // 2
    zeros = jnp.zeros((CONV_HALO, CONV_W), BF16)
    for q in range(seqs):
        pad_ref[q, 0:CONV_HALO, :] = zeros
        pad_ref[q, CONV_HALO + seq:, :] = zeros
        pad_ref[q, CONV_HALO:CONV_HALO + seq, :] = h_ref[q * seq:(q + 1) * seq, :]

    @pl.when(first)
    def _():
        g = jnp.dot(taps_ref[...], w_ref[...], preferred_element_type=F32, precision=lax.Precision.HIGHEST)
        g_cos, g_mix = g[:half], g[half:]
        first = lax.broadcasted_iota(jnp.int32, g_cos.shape, 0) == 0
        g_sin = jnp.where(first, 0.0, g_mix)
        g_ref[0] = g_cos
        g_ref[1] = -g_sin
        g_ref[2] = g_sin
        g_ref[3] = jnp.where(first, g_mix, g_cos)

    def forward(q, i):
        win = pad_ref[q, i * CONV_HOP:i * CONV_HOP + n, :]
        return (jnp.dot(fwd_hi_ref[...], win, preferred_element_type=F32)
                + jnp.dot(fwd_lo_ref[...], win, preferred_element_type=F32))

    windows = [(q, i) for q in range(seqs) for i in range(seq // CONV_HOP)]
    xs_next = forward(*windows[0])
    for w, (q, i) in enumerate(windows):
        xs = xs_next
        if w + 1 < len(windows):
            xs_next = forward(*windows[w + 1])
        a, b = xs[:half], xs[half:]
        z = jnp.concatenate([a * g_ref[0] + b * g_ref[1], a * g_ref[2] + b * g_ref[3]], axis=0)
        z_hi = z.astype(BF16)
        z_lo = (z - z_hi.astype(F32)).astype(BF16)
        acc = jnp.dot(inv_ref[...], jnp.concatenate([z_hi, z_hi, z_lo], axis=0), preferred_element_type=F32)
        acc = acc + _vec(vec_ref, VEC_CONV_B, CONV_W)
        mu = jnp.mean(acc, axis=-1, keepdims=True)
        cen = acc - mu
        var = jnp.mean(cen * cen, axis=-1, keepdims=True)
        hn = cen * lax.rsqrt(var + EPS) * _vec(vec_ref, VEC_CONV_LN_G, CONV_W) + _vec(vec_ref, VEC_CONV_LN_B, CONV_W)
        out_rows = slice(q * seq + i * CONV_HOP, q * seq + (i + 1) * CONV_HOP)
        o_ref[out_rows, :] = (hn * jax.nn.sigmoid(hn) * gate_ref[out_rows, :].astype(F32)).astype(o_ref.dtype)
        yield


def _log_sigmoid(x):
    return jnp.minimum(x, 0.0) - jnp.log1p(jnp.exp(-jnp.abs(x)))


def _ret_body(dl_ref, q_ref, k_ref, v_ref, gate_ref, o_ref, decay_ref, tab_ref, cdec_ref, sd_ref, kv_ref, st_ref, *,
              seq, chunk, heads, first):
    C = chunk
    n_chunks = seq // C
    D = RET_HD
    ri = lax.broadcasted_iota(jnp.int32, (C, LANES), 0)
    idx = ri.astype(F32)
    trans_b = (((1,), (1,)), ((), ()))
    trans_a = (((0,), (0,)), ((), ()))

    def rows(n):
        return pl.ds(n * C, C)

    def lanes(j):
        return slice(j * D, (j + 1) * D)

    qdec_f, kdec_f, qdec_b, kdec_b = range(4)

    @pl.when(first)
    def _():
        for j in range(heads):
            lg_f = jnp.broadcast_to(_log_sigmoid(dl_ref[0, j])[0:1, :], (C, LANES))
            lg_b = jnp.broadcast_to(_log_sigmoid(dl_ref[1, j])[0:1, :], (C, LANES))
            for c in range(C // LANES):
                diff = (ri - (lax.broadcasted_iota(jnp.int32, (C, LANES), 1) + c * LANES)).astype(F32)
                decay_ref[j, :, c * LANES:(c + 1) * LANES] = jnp.where(
                    diff >= 0.0, jnp.exp(lg_f * jnp.maximum(diff, 0.0)), jnp.exp(lg_b * jnp.maximum(-diff, 0.0)))
            tab_ref[j, qdec_f] = jnp.exp(lg_f * (idx + 1.0))
            tab_ref[j, kdec_f] = jnp.exp(lg_f * (C - 1.0 - idx))
            tab_ref[j, qdec_b] = jnp.exp(lg_b * (C - idx))
            tab_ref[j, kdec_b] = jnp.exp(lg_b * idx)
            cdec_ref[j, 0] = jnp.exp(lg_f[:D] * float(C))
            cdec_ref[j, 1] = jnp.exp(lg_b[:D] * float(C))

    for n in range(n_chunks):
        for j in range(heads):
            qn, kn, vn = q_ref[rows(n), lanes(j)], k_ref[rows(n), lanes(j)], v_ref[rows(n), lanes(j)]
            s = lax.dot_general(qn, kn, trans_b, preferred_element_type=F32)
            sd_ref[j, n] = (s * decay_ref[j]).astype(BF16)
            knf = kn.astype(F32)
            kv_ref[j, 0, n] = lax.dot_general((knf * tab_ref[j, kdec_f]).astype(BF16), vn, trans_a, preferred_element_type=F32)
            kv_ref[j, 1, n] = lax.dot_general((knf * tab_ref[j, kdec_b]).astype(BF16), vn, trans_a, preferred_element_type=F32)
        yield

    for j in range(heads):
        state = jnp.zeros((D, D), F32)
        for n in range(n_chunks):
            st_ref[j, 0, n] = state.astype(BF16)
            state = cdec_ref[j, 0] * state + kv_ref[j, 0, n]
        state = jnp.zeros((D, D), F32)
        for n in reversed(range(n_chunks)):
            st_ref[j, 1, n] = state.astype(BF16)
            state = cdec_ref[j, 1] * state + kv_ref[j, 1, n]
    yield

    for n in range(n_chunks):
        for j in range(heads):
            qnf = q_ref[rows(n), lanes(j)].astype(F32)
            lhs = jnp.concatenate(
                [sd_ref[j, n], (qnf * tab_ref[j, qdec_f]).astype(BF16), (qnf * tab_ref[j, qdec_b]).astype(BF16)], axis=1)
            rhs = jnp.concatenate([v_ref[rows(n), lanes(j)], st_ref[j, 0, n], st_ref[j, 1, n]], axis=0)
            out = jnp.dot(lhs, rhs, preferred_element_type=F32)
            mu = jnp.mean(out, axis=-1, keepdims=True)
            cen = out - mu
            var = jnp.mean(cen * cen, axis=-1, keepdims=True)
            gated = cen * lax.rsqrt(var + EPS) * gate_ref[rows(n), lanes(j)].astype(F32)
            o_ref[rows(n), lanes(j)] = gated.astype(o_ref.dtype)
        yield


def _branches_kernel(hglu_ref, rqkv_ref, gate_ref, w_ref, vec_ref, dl_ref, fwd_hi_ref, fwd_lo_ref, inv_ref, taps_ref,
                     yc_ref, yr_ref, pad_ref, g_ref, decay_ref, tab_ref, cdec_ref, sd_ref, kv_ref, st_ref, *,
                     seq, chunk, heads):
    first = pl.program_id(0) == 0
    width = heads * RET_HD

    def cols(start, g):
        return pl.ds(start + g * width, width)

    bodies = [_conv_body(hglu_ref, gate_ref.at[:, pl.ds(0, CONV_W)], w_ref, vec_ref, fwd_hi_ref, fwd_lo_ref, inv_ref,
                         taps_ref, yc_ref, pad_ref, g_ref, seq=seq, seqs=1, first=first)]
    for g in range(RET_HEADS // heads):
        bodies.append(_ret_body(
            dl_ref.at[:, pl.ds(g * heads, heads)], rqkv_ref.at[:, cols(0, g)], rqkv_ref.at[:, cols(RET_W, g)],
            rqkv_ref.at[:, cols(2 * RET_W, g)], gate_ref.at[:, cols(CONV_W, g)], yr_ref.at[:, cols(0, g)],
            decay_ref.at[g], tab_ref.at[g], cdec_ref.at[g], sd_ref.at[g], kv_ref.at[g], st_ref.at[g],
            seq=seq, chunk=chunk, heads=heads, first=first))
    for _ in itertools.chain(*bodies):
        pass


def _branches(hglu, rqkv, gate, p, layer, batch, seq):
    assert seq % CONV_HOP == 0
    chunk = 256 if seq % 256 == 0 else RET_CHUNK
    n_chunks = seq // chunk
    heads = 2
    groups = RET_HEADS // heads
    tables = _conv_tables()
    whole = lambda a: pl.BlockSpec(a.shape, lambda i: (0,) * a.ndim, pipeline_mode=pl.Buffered(1))
    row = lambda width: pl.BlockSpec((seq, width), lambda i: (i, 0))
    return pl.pallas_call(
        functools.partial(_branches_kernel, seq=seq, chunk=chunk, heads=heads),
        grid=(batch,),
        in_specs=[
            row(CONV_W),
            row(3 * RET_W),
            row(CONV_W + RET_W),
            _layer_block(layer, (CONV_K + 1, CONV_W)),
            _layer_block(layer, (SUBLANES, D_MODEL)),
            _layer_block(layer, (2, RET_HEADS, SUBLANES, LANES)),
        ] + [whole(t) for t in tables],
        out_specs=[row(CONV_W), row(RET_W)],
        out_shape=[jax.ShapeDtypeStruct((batch * seq, CONV_W), BF16), jax.ShapeDtypeStruct((batch * seq, RET_W), BF16)],
        scratch_shapes=[
            pltpu.VMEM((1, seq + 2 * CONV_HALO, CONV_W), BF16),
            pltpu.VMEM((4, CONV_WIN // 2, CONV_W), F32),
            pltpu.VMEM((groups, heads, chunk, chunk), F32),
            pltpu.VMEM((groups, heads, 4, chunk, LANES), F32),
            pltpu.VMEM((groups, heads, 2, RET_HD, LANES), F32),
            pltpu.VMEM((groups, heads, n_chunks, chunk, chunk), BF16),
            pltpu.VMEM((groups, heads, 2, n_chunks, RET_HD, RET_HD), F32),
            pltpu.VMEM((groups, heads, 2, n_chunks, RET_HD, RET_HD), BF16),
        ],
        compiler_params=pltpu.CompilerParams(dimension_semantics=("arbitrary",), vmem_limit_bytes=VMEM_LIMIT),
        name="conv_retention",
    )(hglu, rqkv, gate, p["conv_w"], p["vecs"], p["ret_decay"], *tables)


def _mla_kernel(q_ref, kt_ref, v_ref, gate_ref, o_ref, *, sub, heads):
    ones = jnp.ones((v_ref.shape[0], MLA_V_HD), BF16)
    kts = [kt_ref[j * MLA_QK_PAD:(j + 1) * MLA_QK_PAD, :] for j in range(heads)]
    vs = [jnp.concatenate([v_ref[:, j * MLA_V_HD:(j + 1) * MLA_V_HD], ones], axis=1) for j in range(heads)]
    items = [(j, r) for j in range(heads) for r in range(q_ref.shape[0] // sub)]

    def scores(j, r):
        q = q_ref[r * sub:(r + 1) * sub, j * MLA_QK_PAD:(j + 1) * MLA_QK_PAD]
        return jnp.dot(q, kts[j], preferred_element_type=F32)

    s_next = scores(*items[0])
    for i, (j, r) in enumerate(items):
        rows, cols = slice(r * sub, (r + 1) * sub), slice(j * MLA_V_HD, (j + 1) * MLA_V_HD)
        s = s_next
        if i + 1 < len(items):
            s_next = scores(*items[i + 1])
        m = jnp.max(s, axis=-1, keepdims=True)
        p = jnp.exp2(s - m)
        o = jnp.dot(p.astype(BF16), vs[j], preferred_element_type=F32)
        gated = o[:, :MLA_V_HD] / o[:, MLA_V_HD:] * gate_ref[rows, cols].astype(F32)
        o_ref[rows, cols] = gated.astype(o_ref.dtype)


def _mla(mq, mkt, mv, gate, batch, seq):
    _, sub, tq = _tiles(seq)
    nq = seq // tq
    heads = 2
    qk_w, v_w = heads * MLA_QK_PAD, heads * MLA_V_HD
    return pl.pallas_call(
        functools.partial(_mla_kernel, sub=sub, heads=heads),
        grid=(batch, MLA_HEADS // heads, nq),
        in_specs=[
            pl.BlockSpec((tq, qk_w), lambda b, g, i: (b * nq + i, g)),
            pl.BlockSpec((qk_w, seq), lambda b, g, i: (g, b)),
            pl.BlockSpec((seq, v_w), lambda b, g, i: (b, g)),
            pl.BlockSpec((tq, v_w), lambda b, g, i: (b * nq + i, (CONV_W + RET_W) // v_w + g)),
        ],
        out_specs=pl.BlockSpec((tq, v_w), lambda b, g, i: (b * nq + i, g)),
        out_shape=jax.ShapeDtypeStruct((batch * seq, MLA_W), BF16),
        compiler_params=pltpu.CompilerParams(
            dimension_semantics=("arbitrary", "arbitrary", "arbitrary"), vmem_limit_bytes=VMEM_LIMIT),
        name="mla_attention",
    )(mq, mkt, mv, gate)


def _outproj_kernel(x_ref, yc_ref, yr_ref, ym_ref, w32_ref, fg_ref, o_ref, w_ref, *, final):
    @pl.when(pl.program_id(0) == 0)
    def _():
        w_ref[...] = w32_ref[...].astype(BF16)

    acc = x_ref[...]
    acc = acc + jnp.dot(yc_ref[...], w_ref[0:CONV_W, :], preferred_element_type=F32)
    acc = acc + jnp.dot(yr_ref[...], w_ref[CONV_W:CONV_W + RET_W, :], preferred_element_type=F32)
    acc = acc + jnp.dot(ym_ref[...], w_ref[CONV_W + RET_W:, :], preferred_element_type=F32)
    if final:
        acc = _rmsnorm(acc, fg_ref[...])
    o_ref[...] = acc


def _outproj(x2, yc, yr, ym, p, layer, seq, final):
    tm = min(2 * _tiles(seq)[0], seq)
    tokens = x2.shape[0]
    row = lambda i: (i, 0)
    return pl.pallas_call(
        functools.partial(_outproj_kernel, final=final),
        grid=(tokens // tm,),
        in_specs=[
            pl.BlockSpec((tm, D_MODEL), row),
            pl.BlockSpec((tm, CONV_W), row),
            pl.BlockSpec((tm, RET_W), row),
            pl.BlockSpec((tm, MLA_W), row),
            _layer_block(layer, (D_MIX, D_MODEL)),
            pl.BlockSpec((1, D_MODEL), lambda i: (0, 0)),
        ],
        out_specs=pl.BlockSpec((tm, D_MODEL), row),
        out_shape=jax.ShapeDtypeStruct(x2.shape, F32),
        scratch_shapes=[pltpu.VMEM((D_MIX, D_MODEL), BF16)],
        compiler_params=pltpu.CompilerParams(
            dimension_semantics=("arbitrary",), vmem_limit_bytes=VMEM_LIMIT),
        name="outproj",
    )(x2, yc, yr, ym, p["w_out"], p["final_g"])


def _pad_last(w, width):
    return jnp.pad(w, [(0, 0)] * (w.ndim - 1) + [(0, width - w.shape[-1])])


def _split_w_in_kernel(wt_ref, main_ref, lat_ref, gate_ref):
    wt = wt_ref[...]
    k_rope = wt[OFF_KROPE:OFF_GATE, :]
    main_ref[...] = jnp.concatenate([wt[:OFF_QLAT, :].T, wt[OFF_KVLAT:OFF_KROPE, :].T], axis=1).astype(BF16)
    lat_ref[...] = jnp.concatenate([wt[OFF_QLAT:OFF_KVLAT, :].T,
                                    jnp.concatenate([k_rope, k_rope], axis=0).T], axis=1).astype(BF16)
    gate_ref[...] = wt[OFF_GATE:, :].T.astype(BF16)


def _split_w_in(w_in):
    depth = w_in.shape[0]
    rows = 256
    widths = (W_MAIN, MLA_Q_RANK + LANES, D_MIX)
    return pl.pallas_call(
        _split_w_in_kernel,
        grid=(depth, D_MODEL // rows),
        in_specs=[pl.BlockSpec((None, N_IN, rows), lambda l, i: (l, 0, i))],
        out_specs=[pl.BlockSpec((None, rows, w), lambda l, i: (l, i, 0)) for w in widths],
        out_shape=[jax.ShapeDtypeStruct((depth, D_MODEL, w), BF16) for w in widths],
        compiler_params=pltpu.CompilerParams(dimension_semantics=("arbitrary", "arbitrary")),
        name="split_w_in",
    )(jnp.swapaxes(w_in, 1, 2))


def _prep_params(norm_g, w_in, conv_dw_w, conv_dw_b, conv_ln_g, conv_ln_b, ret_decay_logit,
                 mla_qa_g, mla_w_uq, mla_kva_g, mla_w_ukv, w_out, final_g):
    depth = norm_g.shape[0]
    uq = mla_w_uq.reshape(depth, MLA_Q_RANK, MLA_HEADS, MLA_NOPE + MLA_ROPE)
    uq = jnp.concatenate([uq[..., :MLA_NOPE].reshape(depth, MLA_Q_RANK, MLA_HEADS * MLA_NOPE),
                          uq[..., MLA_NOPE:].reshape(depth, MLA_Q_RANK, MLA_HEADS * MLA_ROPE)], axis=-1)
    ukv = mla_w_ukv.reshape(depth, MLA_KV_RANK, MLA_HEADS, MLA_NOPE + MLA_V_HD)
    ukt = ukv[..., :MLA_NOPE].reshape(depth, MLA_KV_RANK, MLA_HEADS * MLA_NOPE).transpose(0, 2, 1)
    uv = ukv[..., MLA_NOPE:].reshape(depth, MLA_KV_RANK, MLA_W)
    vecs = [norm_g, mla_qa_g, mla_kva_g, conv_dw_b, conv_ln_g, conv_ln_b]
    vecs = jnp.stack([_pad_last(v, D_MODEL) for v in vecs] + [jnp.zeros_like(norm_g)] * (SUBLANES - len(vecs)), axis=1)
    w_main, w_lat, w_gate = _split_w_in(w_in)
    return {
        "vecs": vecs,
        "w_main": w_main,
        "w_lat": w_lat,
        "w_gate": w_gate,
        "w_uq": uq.astype(BF16),
        "w_ukt": ukt.astype(BF16),
        "w_uv": uv.astype(BF16),
        "conv_w": jnp.pad(conv_dw_w, ((0, 0), (0, 1), (0, 0))),
        "ret_decay": jnp.broadcast_to(ret_decay_logit[:, :, :, None, None], (depth, 2, RET_HEADS, SUBLANES, LANES)),
        "w_out": w_out,
        "final_g": final_g[None, :],
    }


def _rope_inputs(seq):
    cos_r, sin_r = _rope_tables(seq, RET_HD)
    half = RET_HD // 2
    sin_r = np.concatenate([-sin_r[:, :half], sin_r[:, half:]], axis=1)
    cos_m, sin_m = _rope_tables(seq, MLA_ROPE)
    half = MLA_ROPE // 2
    zeros = np.zeros((seq, half), np.float32)
    cosm = np.concatenate([cos_m, cos_m], axis=1)
    sinma = np.concatenate([-sin_m[:, :half], zeros, -sin_m[:, :half], zeros], axis=1)
    sinmb = np.concatenate([zeros, sin_m[:, half:], zeros, sin_m[:, half:]], axis=1)
    return tuple(jnp.asarray(t) for t in (cos_r, sin_r, cosm, sinma, sinmb))


def kernel(x, norm_g, w_in, conv_dw_w, conv_dw_b, conv_ln_g, conv_ln_b, ret_decay_logit,
           mla_qa_g, mla_w_uq, mla_kva_g, mla_w_ukv, w_out, final_g):
    batch, seq, d_model = x.shape
    depth = norm_g.shape[0]
    assert d_model == D_MODEL and seq % RET_CHUNK == 0
    p = _prep_params(norm_g, w_in, conv_dw_w, conv_dw_b, conv_ln_g, conv_ln_b, ret_decay_logit,
                     mla_qa_g, mla_w_uq, mla_kva_g, mla_w_ukv, w_out, final_g)
    tabs = _rope_inputs(seq)
    x2 = x.reshape(batch * seq, d_model)
    for layer in range(depth):
        hglu, rqkv, mq, mkt, mv, gate = _inproj(x2, p, layer, tabs, seq)
        yc, yr = _branches(hglu, rqkv, gate, p, layer, batch, seq)
        ym = _mla(mq, mkt, mv, gate, batch, seq)
        x2 = _outproj(x2, yc, yr, ym, p, layer, seq, final=(layer == depth - 1))
    return x2.reshape(batch, seq, d_model)
```

```python
import functools
import math

import jax
import jax.numpy as jnp
import numpy as np
from jax import lax
from jax.experimental import pallas as pl
from jax.experimental.pallas import tpu as pltpu

D_MODEL = 1024
D_MIX = 2 * D_MODEL
CONV_W = 512
CONV_K = 31
RET_W = 512
RET_HEADS = 4
RET_HD = 128
RET_CHUNK = 128
MLA_W = 1024
MLA_HEADS = 8
MLA_V_HD = 128
MLA_NOPE = 128
MLA_ROPE = 64
MLA_Q_RANK = 384
MLA_KV_RANK = 256
ROPE_BASE = 10000.0
EPS = 1e-6

OFF_RET = 2 * CONV_W
OFF_QLAT = OFF_RET + 3 * RET_W
OFF_KVLAT = OFF_QLAT + MLA_Q_RANK
OFF_KROPE = OFF_KVLAT + MLA_KV_RANK
OFF_GATE = OFF_KROPE + MLA_ROPE
N_IN = OFF_GATE + D_MIX
W_MAIN = OFF_QLAT + MLA_KV_RANK

LANES = 128
SUBLANES = 8
MLA_QK_PAD = 2 * LANES
CONV_HALO = 16
CONV_HOP = 128
CONV_WIN = CONV_HOP + 2 * CONV_HALO
VMEM_LIMIT = 56 * 1024 * 1024

BF16 = jnp.bfloat16
F32 = jnp.float32


def _tiles(seq):
    return min(512, seq), 256, min(2048, seq)


def _rope_tables(seq, dim):
    f32 = np.float32
    inv = f32(1.0) / (f32(ROPE_BASE) ** (np.arange(0, dim, 2, dtype=f32) / f32(dim)))
    ang = np.arange(seq, dtype=f32)[:, None] * inv[None, :]
    ang = np.concatenate([ang, ang], axis=-1).astype(np.float64)
    return np.cos(ang).astype(f32), np.sin(ang).astype(f32)


def _rmsnorm(x, g):
    return x * lax.rsqrt(jnp.mean(x * x, axis=-1, keepdims=True) + EPS) * g


VEC_NORM_G, VEC_QA_G, VEC_KVA_G, VEC_CONV_B, VEC_CONV_LN_G, VEC_CONV_LN_B = range(6)


def _vec(vec_ref, row, width):
    return vec_ref[row:row + 1, :width]


def _layer_block(layer, shape):
    return pl.BlockSpec((None,) + shape, lambda *_: (layer,) + (0,) * len(shape), pipeline_mode=pl.Buffered(1))


def _inproj_kernel(x_ref, vec_ref, wmain_ref, wlat_ref, wgate_ref, wuq_ref, wukt_ref, wuv_ref,
                   cosr_ref, sinr_ref, cosm_ref, sinma_ref, sinmb_ref,
                   hglu_ref, rqkv_ref, mq_ref, mkt_ref, mv_ref, gate_ref, *, sub):
    blocks = [_inproj_rows(slice(r * sub, (r + 1) * sub), x_ref, vec_ref, wmain_ref, wlat_ref, wgate_ref,
                           wuq_ref, wukt_ref, wuv_ref,
                           cosr_ref, sinr_ref, cosm_ref, sinma_ref, sinmb_ref,
                           hglu_ref, rqkv_ref, mq_ref, mkt_ref, mv_ref, gate_ref)
              for r in range(x_ref.shape[0] // sub)]
    for _ in zip(*blocks):
        pass


def _inproj_rows(rows, x_ref, vec_ref, wmain_ref, wlat_ref, wgate_ref, wuq_ref, wukt_ref, wuv_ref,
                 cosr_ref, sinr_ref, cosm_ref, sinma_ref, sinmb_ref,
                 hglu_ref, rqkv_ref, mq_ref, mkt_ref, mv_ref, gate_ref):
    h = _rmsnorm(x_ref[rows, :], _vec(vec_ref, VEC_NORM_G, D_MODEL)).astype(BF16)

    lat = jnp.dot(h, wlat_ref[...], preferred_element_type=F32)
    q_lat, k_rope = lat[:, :MLA_Q_RANK], lat[:, MLA_Q_RANK:]
    yield

    gt = jnp.dot(h, wgate_ref[...], preferred_element_type=F32)
    gate_ref[rows, :] = (gt * jax.nn.sigmoid(gt)).astype(BF16)
    yield

    u_main = jnp.dot(h, wmain_ref[...], preferred_element_type=F32)

    hglu_ref[rows, :] = (u_main[:, :CONV_W] * jax.nn.sigmoid(u_main[:, CONV_W:2 * CONV_W])).astype(BF16)

    cosr = cosr_ref[rows, :]
    sinr = sinr_ref[rows, :]
    k_scale = RET_HD ** -0.5
    for part in range(3):
        for hd in range(RET_HEADS):
            col = part * RET_W + hd * RET_HD
            blk = u_main[:, OFF_RET + col:OFF_RET + col + RET_HD]
            if part < 2:
                blk = blk * cosr + pltpu.roll(blk, RET_HD // 2, 1) * sinr
            if part == 1:
                blk = blk * k_scale
            rqkv_ref[rows, col:col + RET_HD] = blk.astype(BF16)
    yield

    cosm = cosm_ref[rows, :]
    sinma = sinma_ref[rows, :]
    sinmb = sinmb_ref[rows, :]

    def rope_pair(blk):
        return (blk * cosm + pltpu.roll(blk, LANES - MLA_ROPE // 2, 1) * sinma
                + pltpu.roll(blk, MLA_ROPE // 2, 1) * sinmb)

    qn = _rmsnorm(q_lat, _vec(vec_ref, VEC_QA_G, MLA_Q_RANK)).astype(BF16)
    q_scale = (MLA_NOPE + MLA_ROPE) ** -0.5 * math.log2(math.e)
    q_all = jnp.dot(qn, wuq_ref[...], preferred_element_type=F32) * q_scale
    rope_off = MLA_HEADS * MLA_NOPE
    lane = lax.broadcasted_iota(jnp.int32, (q_all.shape[0], LANES), 1)
    for pair in range(MLA_HEADS // 2):
        roped = rope_pair(q_all[:, rope_off + pair * LANES:rope_off + (pair + 1) * LANES])
        for j in range(2):
            hd = 2 * pair + j
            own = (lane < MLA_ROPE) if j == 0 else (lane >= MLA_ROPE)
            mq_ref[rows, hd * MLA_QK_PAD:hd * MLA_QK_PAD + LANES] = q_all[:, hd * MLA_NOPE:(hd + 1) * MLA_NOPE].astype(BF16)
            mq_ref[rows, hd * MLA_QK_PAD + LANES:(hd + 1) * MLA_QK_PAD] = jnp.where(own, roped, 0.0).astype(BF16)

    yield
    kvn = _rmsnorm(u_main[:, OFF_QLAT:W_MAIN], _vec(vec_ref, VEC_KVA_G, MLA_KV_RANK)).astype(BF16)
    k_rope_t = rope_pair(k_rope).T.astype(BF16)
    k_nope_t = lax.dot_general(wukt_ref[...], kvn, (((1,), (1,)), ((), ())),
                               preferred_element_type=F32).astype(BF16)
    for hd in range(MLA_HEADS):
        mkt_ref[hd * MLA_QK_PAD:hd * MLA_QK_PAD + LANES, rows] = k_nope_t[hd * MLA_NOPE:(hd + 1) * MLA_NOPE, :]
        mkt_ref[hd * MLA_QK_PAD + LANES:(hd + 1) * MLA_QK_PAD, rows] = k_rope_t
    mv_ref[rows, :] = jnp.dot(kvn, wuv_ref[...], preferred_element_type=F32).astype(BF16)
    yield


def _inproj(x2, p, layer, tabs, seq):
    tm, sub, _ = _tiles(seq)
    tokens = x2.shape[0]
    nseq = seq // tm
    row = lambda i: (i, 0)
    pos = lambda i: (i % nseq, 0)
    return pl.pallas_call(
        functools.partial(_inproj_kernel, sub=sub),
        grid=(tokens // tm,),
        in_specs=[
            pl.BlockSpec((tm, D_MODEL), row),
            _layer_block(layer, (SUBLANES, D_MODEL)),
            _layer_block(layer, (D_MODEL, W_MAIN)),
            _layer_block(layer, (D_MODEL, MLA_Q_RANK + LANES)),
            _layer_block(layer, (D_MODEL, D_MIX)),
            _layer_block(layer, (MLA_Q_RANK, MLA_HEADS * (MLA_NOPE + MLA_ROPE))),
            _layer_block(layer, (MLA_HEADS * MLA_NOPE, MLA_KV_RANK)),
            _layer_block(layer, (MLA_KV_RANK, MLA_W)),
        ] + [pl.BlockSpec((tm, LANES), pos)] * len(tabs),
        out_specs=[
            pl.BlockSpec((tm, CONV_W), row),
            pl.BlockSpec((tm, 3 * RET_W), row),
            pl.BlockSpec((tm, MLA_HEADS * MLA_QK_PAD), row),
            pl.BlockSpec((MLA_HEADS * MLA_QK_PAD, tm), lambda i: (0, i)),
            pl.BlockSpec((tm, MLA_W), row),
            pl.BlockSpec((tm, D_MIX), row),
        ],
        out_shape=[
            jax.ShapeDtypeStruct((tokens, CONV_W), BF16),
            jax.ShapeDtypeStruct((tokens, 3 * RET_W), BF16),
            jax.ShapeDtypeStruct((tokens, MLA_HEADS * MLA_QK_PAD), BF16),
            jax.ShapeDtypeStruct((MLA_HEADS * MLA_QK_PAD, tokens), BF16),
            jax.ShapeDtypeStruct((tokens, MLA_W), BF16),
            jax.ShapeDtypeStruct((tokens, D_MIX), BF16),
        ],
        compiler_params=pltpu.CompilerParams(
            dimension_semantics=("arbitrary",), vmem_limit_bytes=VMEM_LIMIT),
        name="inproj",
    )(x2, p["vecs"], p["w_main"], p["w_lat"], p["w_gate"], p["w_uq"], p["w_ukt"], p["w_uv"], *tabs)


def _conv_tables():
    n, half = CONV_WIN, CONV_WIN // 2

    def trig_rows(pos):
        ang = 2.0 * np.pi * ((np.arange(half)[:, None] * pos[None, :]) % n) / n
        top, bot = np.cos(ang), np.sin(ang)
        bot[0] = np.cos(2.0 * np.pi * ((half * pos) % n) / n)
        return np.concatenate([top, bot], axis=0)

    def split(m):
        hi = m.astype(BF16)
        return hi, (m - hi.astype(np.float64)).astype(BF16)

    fwd_hi, fwd_lo = split(trig_rows(np.arange(n)))
    inv = trig_rows(np.arange(CONV_HOP)).T * (2.0 / n)
    inv[:, 0] *= 0.5
    inv[:, half] *= 0.5
    inv_hi, inv_lo = split(inv)
    inv = np.concatenate([inv_hi, inv_lo, inv_hi], axis=1)
    taps = np.zeros((n, CONV_K + 1), np.float32)
    taps[:, :CONV_K] = trig_rows(n - 1 - np.arange(CONV_K))
    return jnp.asarray(fwd_hi), jnp.asarray(fwd_lo), jnp.asarray(inv), jnp.asarray(taps)


def _conv_kernel(h_ref, gate_ref, w_ref, vec_ref, fwd_hi_ref, fwd_lo_ref, inv_ref, taps_ref, o_ref,
                 pad_ref, g_ref, *, seq, seqs):
    n, half = CONV_WIN, CONV_WIN // 2
    zeros = jnp.zeros((CONV_HALO, CONV_W), BF16)
    for q in range(seqs):
        pad_ref[q, 0:CONV_HALO, :] = zeros
        pad_ref[q, CONV_HALO + seq:, :] = zeros
        pad_ref[q, CONV_HALO:CONV_HALO + seq, :] = h_ref[q * seq:(q + 1) * seq, :]

    @pl.when(pl.program_id(0) == 0)
    def _():
        g = jnp.dot(taps_ref[...], w_ref[...], preferred_element_type=F32, precision=lax.Precision.HIGHEST)
        g_cos, g_mix = g[:half], g[half:]
        first = lax.broadcasted_iota(jnp.int32, g_cos.shape, 0) == 0
        g_sin = jnp.where(first, 0.0, g_mix)
        g_ref[0] = g_cos
        g_ref[1] = -g_sin
        g_ref[2] = g_sin
        g_ref[3] = jnp.where(first, g_mix, g_cos)

    def forward(q, i):
        win = pad_ref[q, i * CONV_HOP:i * CONV_HOP + n, :]
        return (jnp.dot(fwd_hi_ref[...], win, preferred_element_type=F32)
                + jnp.dot(fwd_lo_ref[...], win, preferred_element_type=F32))

    windows = [(q, i) for q in range(seqs) for i in range(seq // CONV_HOP)]
    xs_next = forward(*windows[0])
    for w, (q, i) in enumerate(windows):
        xs = xs_next
        if w + 1 < len(windows):
            xs_next = forward(*windows[w + 1])
        a, b = xs[:half], xs[half:]
        z = jnp.concatenate([a * g_ref[0] + b * g_ref[1], a * g_ref[2] + b * g_ref[3]], axis=0)
        z_hi = z.astype(BF16)
        z_lo = (z - z_hi.astype(F32)).astype(BF16)
        acc = jnp.dot(inv_ref[...], jnp.concatenate([z_hi, z_hi, z_lo], axis=0), preferred_element_type=F32)
        acc = acc + _vec(vec_ref, VEC_CONV_B, CONV_W)
        mu = jnp.mean(acc, axis=-1, keepdims=True)
        cen = acc - mu
        var = jnp.mean(cen * cen, axis=-1, keepdims=True)
        hn = cen * lax.rsqrt(var + EPS) * _vec(vec_ref, VEC_CONV_LN_G, CONV_W) + _vec(vec_ref, VEC_CONV_LN_B, CONV_W)
        out_rows = slice(q * seq + i * CONV_HOP, q * seq + (i + 1) * CONV_HOP)
        o_ref[out_rows, :] = (hn * jax.nn.sigmoid(hn) * gate_ref[out_rows, :].astype(F32)).astype(o_ref.dtype)


def _conv(hglu, gate, p, layer, batch, seq):
    assert seq % CONV_HOP == 0
    seqs = 2 if batch % 2 == 0 else 1
    rows = seqs * seq
    tables = _conv_tables()
    whole = lambda a: pl.BlockSpec(a.shape, lambda i: (0,) * a.ndim, pipeline_mode=pl.Buffered(1))
    return pl.pallas_call(
        functools.partial(_conv_kernel, seq=seq, seqs=seqs),
        grid=(batch // seqs,),
        in_specs=[
            pl.BlockSpec((rows, CONV_W), lambda i: (i, 0)),
            pl.BlockSpec((rows, CONV_W), lambda i: (i, 0)),
            _layer_block(layer, (CONV_K + 1, CONV_W)),
            _layer_block(layer, (SUBLANES, D_MODEL)),
        ] + [whole(t) for t in tables],
        out_specs=pl.BlockSpec((rows, CONV_W), lambda i: (i, 0)),
        out_shape=jax.ShapeDtypeStruct(hglu.shape, BF16),
        scratch_shapes=[
            pltpu.VMEM((seqs, seq + 2 * CONV_HALO, CONV_W), BF16),
            pltpu.VMEM((4, CONV_WIN // 2, CONV_W), F32),
        ],
        compiler_params=pltpu.CompilerParams(dimension_semantics=("arbitrary",)),
        name="conv",
    )(hglu, gate, p["conv_w"], p["vecs"], *tables)


def _log_sigmoid(x):
    return jnp.minimum(x, 0.0) - jnp.log1p(jnp.exp(-jnp.abs(x)))


def _ret_kernel(dl_ref, q_ref, k_ref, v_ref, gate_ref, o_ref, decay_ref, tab_ref, cdec_ref, sd_ref, kv_ref, st_ref, *,
                seq, chunk, heads):
    C = chunk
    n_chunks = seq // C
    D = RET_HD
    ri = lax.broadcasted_iota(jnp.int32, (C, LANES), 0)
    idx = ri.astype(F32)
    trans_b = (((1,), (1,)), ((), ()))
    trans_a = (((0,), (0,)), ((), ()))

    def rows(n):
        return pl.ds(n * C, C)

    def lanes(j):
        return slice(j * D, (j + 1) * D)

    qdec_f, kdec_f, qdec_b, kdec_b = range(4)

    @pl.when(pl.program_id(1) == 0)
    def _():
        for j in range(heads):
            lg_f = jnp.broadcast_to(_log_sigmoid(dl_ref[0, j])[0:1, :], (C, LANES))
            lg_b = jnp.broadcast_to(_log_sigmoid(dl_ref[1, j])[0:1, :], (C, LANES))
            for c in range(C // LANES):
                diff = (ri - (lax.broadcasted_iota(jnp.int32, (C, LANES), 1) + c * LANES)).astype(F32)
                decay_ref[j, :, c * LANES:(c + 1) * LANES] = jnp.where(
                    diff >= 0.0, jnp.exp(lg_f * jnp.maximum(diff, 0.0)), jnp.exp(lg_b * jnp.maximum(-diff, 0.0)))
            tab_ref[j, qdec_f] = jnp.exp(lg_f * (idx + 1.0))
            tab_ref[j, kdec_f] = jnp.exp(lg_f * (C - 1.0 - idx))
            tab_ref[j, qdec_b] = jnp.exp(lg_b * (C - idx))
            tab_ref[j, kdec_b] = jnp.exp(lg_b * idx)
            cdec_ref[j, 0] = jnp.exp(lg_f[:D] * float(C))
            cdec_ref[j, 1] = jnp.exp(lg_b[:D] * float(C))

    for n in range(n_chunks):
        for j in range(heads):
            qn, kn, vn = q_ref[rows(n), lanes(j)], k_ref[rows(n), lanes(j)], v_ref[rows(n), lanes(j)]
            s = lax.dot_general(qn, kn, trans_b, preferred_element_type=F32)
            sd_ref[j, n] = (s * decay_ref[j]).astype(BF16)
            knf = kn.astype(F32)
            kv_ref[j, 0, n] = lax.dot_general((knf * tab_ref[j, kdec_f]).astype(BF16), vn, trans_a, preferred_element_type=F32)
            kv_ref[j, 1, n] = lax.dot_general((knf * tab_ref[j, kdec_b]).astype(BF16), vn, trans_a, preferred_element_type=F32)

    for j in range(heads):
        state = jnp.zeros((D, D), F32)
        for n in range(n_chunks):
            st_ref[j, 0, n] = state.astype(BF16)
            state = cdec_ref[j, 0] * state + kv_ref[j, 0, n]
        state = jnp.zeros((D, D), F32)
        for n in reversed(range(n_chunks)):
            st_ref[j, 1, n] = state.astype(BF16)
            state = cdec_ref[j, 1] * state + kv_ref[j, 1, n]

    for n in range(n_chunks):
        for j in range(heads):
            qnf = q_ref[rows(n), lanes(j)].astype(F32)
            lhs = jnp.concatenate(
                [sd_ref[j, n], (qnf * tab_ref[j, qdec_f]).astype(BF16), (qnf * tab_ref[j, qdec_b]).astype(BF16)], axis=1)
            rhs = jnp.concatenate([v_ref[rows(n), lanes(j)], st_ref[j, 0, n], st_ref[j, 1, n]], axis=0)
            out = jnp.dot(lhs, rhs, preferred_element_type=F32)
            mu = jnp.mean(out, axis=-1, keepdims=True)
            cen = out - mu
            var = jnp.mean(cen * cen, axis=-1, keepdims=True)
            gated = cen * lax.rsqrt(var + EPS) * gate_ref[rows(n), lanes(j)].astype(F32)
            o_ref[rows(n), lanes(j)] = gated.astype(o_ref.dtype)


def _retention(rqkv, gate, p, layer, batch, seq):
    chunk = 256 if seq % 256 == 0 else RET_CHUNK
    n_chunks = seq // chunk
    heads = 4
    groups = RET_HEADS // heads
    width = heads * RET_HD
    blk = lambda first: pl.BlockSpec((seq, width), lambda g, b: (b, first + g))
    return pl.pallas_call(
        functools.partial(_ret_kernel, seq=seq, chunk=chunk, heads=heads),
        grid=(groups, batch),
        in_specs=[
            pl.BlockSpec((None, 2, heads, SUBLANES, LANES), lambda g, b: (layer, 0, g, 0, 0)),
            blk(0), blk(groups), blk(2 * groups),
            blk(CONV_W // width),
        ],
        out_specs=blk(0),
        out_shape=jax.ShapeDtypeStruct((batch * seq, RET_W), BF16),
        scratch_shapes=[
            pltpu.VMEM((heads, chunk, chunk), F32),
            pltpu.VMEM((heads, 4, chunk, LANES), F32),
            pltpu.VMEM((heads, 2, RET_HD, LANES), F32),
            pltpu.VMEM((heads, n_chunks, chunk, chunk), BF16),
            pltpu.VMEM((heads, 2, n_chunks, RET_HD, RET_HD), F32),
            pltpu.VMEM((heads, 2, n_chunks, RET_HD, RET_HD), BF16),
        ],
        compiler_params=pltpu.CompilerParams(dimension_semantics=("arbitrary", "arbitrary")),
        name="retention",
    )(p["ret_decay"], rqkv, rqkv, rqkv, gate)


def _mla_kernel(q_ref, kt_ref, v_ref, gate_ref, o_ref, *, sub, heads):
    ones = jnp.ones((v_ref.shape[0], MLA_V_HD), BF16)
    kts = [kt_ref[j * MLA_QK_PAD:(j + 1) * MLA_QK_PAD, :] for j in range(heads)]
    vs = [jnp.concatenate([v_ref[:, j * MLA_V_HD:(j + 1) * MLA_V_HD], ones], axis=1) for j in range(heads)]
    items = [(j, r) for j in range(heads) for r in range(q_ref.shape[0] // sub)]

    def scores(j, r):
        q = q_ref[r * sub:(r + 1) * sub, j * MLA_QK_PAD:(j + 1) * MLA_QK_PAD]
        return jnp.dot(q, kts[j], preferred_element_type=F32)

    s_next = scores(*items[0])
    for i, (j, r) in enumerate(items):
        rows, cols = slice(r * sub, (r + 1) * sub), slice(j * MLA_V_HD, (j + 1) * MLA_V_HD)
        s = s_next
        if i + 1 < len(items):
            s_next = scores(*items[i + 1])
        m = jnp.max(s, axis=-1, keepdims=True)
        p = jnp.exp2(s - m)
        o = jnp.dot(p.astype(BF16), vs[j], preferred_element_type=F32)
        gated = o[:, :MLA_V_HD] / o[:, MLA_V_HD:] * gate_ref[rows, cols].astype(F32)
        o_ref[rows, cols] = gated.astype(o_ref.dtype)


def _mla(mq, mkt, mv, gate, batch, seq):
    _, sub, tq = _tiles(seq)
    nq = seq // tq
    heads = 2
    qk_w, v_w = heads * MLA_QK_PAD, heads * MLA_V_HD
    return pl.pallas_call(
        functools.partial(_mla_kernel, sub=sub, heads=heads),
        grid=(batch, MLA_HEADS // heads, nq),
        in_specs=[
            pl.BlockSpec((tq, qk_w), lambda b, g, i: (b * nq + i, g)),
            pl.BlockSpec((qk_w, seq), lambda b, g, i: (g, b)),
            pl.BlockSpec((seq, v_w), lambda b, g, i: (b, g)),
            pl.BlockSpec((tq, v_w), lambda b, g, i: (b * nq + i, (CONV_W + RET_W) // v_w + g)),
        ],
        out_specs=pl.BlockSpec((tq, v_w), lambda b, g, i: (b * nq + i, g)),
        out_shape=jax.ShapeDtypeStruct((batch * seq, MLA_W), BF16),
        compiler_params=pltpu.CompilerParams(
            dimension_semantics=("arbitrary", "arbitrary", "arbitrary"), vmem_limit_bytes=VMEM_LIMIT),
        name="mla_attention",
    )(mq, mkt, mv, gate)


def _outproj_kernel(x_ref, yc_ref, yr_ref, ym_ref, w32_ref, fg_ref, o_ref, w_ref, *, final):
    @pl.when(pl.program_id(0) == 0)
    def _():
        w_ref[...] = w32_ref[...].astype(BF16)

    acc = x_ref[...]
    acc = acc + jnp.dot(yc_ref[...], w_ref[0:CONV_W, :], preferred_element_type=F32)
    acc = acc + jnp.dot(yr_ref[...], w_ref[CONV_W:CONV_W + RET_W, :], preferred_element_type=F32)
    acc = acc + jnp.dot(ym_ref[...], w_ref[CONV_W + RET_W:, :], preferred_element_type=F32)
    if final:
        acc = _rmsnorm(acc, fg_ref[...])
    o_ref[...] = acc


def _outproj(x2, yc, yr, ym, p, layer, seq, final):
    tm = min(2 * _tiles(seq)[0], seq)
    tokens = x2.shape[0]
    row = lambda i: (i, 0)
    return pl.pallas_call(
        functools.partial(_outproj_kernel, final=final),
        grid=(tokens // tm,),
        in_specs=[
            pl.BlockSpec((tm, D_MODEL), row),
            pl.BlockSpec((tm, CONV_W), row),
            pl.BlockSpec((tm, RET_W), row),
            pl.BlockSpec((tm, MLA_W), row),
            _layer_block(layer, (D_MIX, D_MODEL)),
            pl.BlockSpec((1, D_MODEL), lambda i: (0, 0)),
        ],
        out_specs=pl.BlockSpec((tm, D_MODEL), row),
        out_shape=jax.ShapeDtypeStruct(x2.shape, F32),
        scratch_shapes=[pltpu.VMEM((D_MIX, D_MODEL), BF16)],
        compiler_params=pltpu.CompilerParams(
            dimension_semantics=("arbitrary",), vmem_limit_bytes=VMEM_LIMIT),
        name="outproj",
    )(x2, yc, yr, ym, p["w_out"], p["final_g"])


def _pad_last(w, width):
    return jnp.pad(w, [(0, 0)] * (w.ndim - 1) + [(0, width - w.shape[-1])])


def _split_w_in_kernel(wt_ref, main_ref, lat_ref, gate_ref):
    wt = wt_ref[...]
    k_rope = wt[OFF_KROPE:OFF_GATE, :]
    main_ref[...] = jnp.concatenate([wt[:OFF_QLAT, :].T, wt[OFF_KVLAT:OFF_KROPE, :].T], axis=1).astype(BF16)
    lat_ref[...] = jnp.concatenate([wt[OFF_QLAT:OFF_KVLAT, :].T,
                                    jnp.concatenate([k_rope, k_rope], axis=0).T], axis=1).astype(BF16)
    gate_ref[...] = wt[OFF_GATE:, :].T.astype(BF16)


def _split_w_in(w_in):
    depth = w_in.shape[0]
    rows = 256
    widths = (W_MAIN, MLA_Q_RANK + LANES, D_MIX)
    return pl.pallas_call(
        _split_w_in_kernel,
        grid=(depth, D_MODEL // rows),
        in_specs=[pl.BlockSpec((None, N_IN, rows), lambda l, i: (l, 0, i))],
        out_specs=[pl.BlockSpec((None, rows, w), lambda l, i: (l, i, 0)) for w in widths],
        out_shape=[jax.ShapeDtypeStruct((depth, D_MODEL, w), BF16) for w in widths],
        compiler_params=pltpu.CompilerParams(dimension_semantics=("arbitrary", "arbitrary")),
        name="split_w_in",
    )(jnp.swapaxes(w_in, 1, 2))


def _prep_params(norm_g, w_in, conv_dw_w, conv_dw_b, conv_ln_g, conv_ln_b, ret_decay_logit,
                 mla_qa_g, mla_w_uq, mla_kva_g, mla_w_ukv, w_out, final_g):
    depth = norm_g.shape[0]
    uq = mla_w_uq.reshape(depth, MLA_Q_RANK, MLA_HEADS, MLA_NOPE + MLA_ROPE)
    uq = jnp.concatenate([uq[..., :MLA_NOPE].reshape(depth, MLA_Q_RANK, MLA_HEADS * MLA_NOPE),
                          uq[..., MLA_NOPE:].reshape(depth, MLA_Q_RANK, MLA_HEADS * MLA_ROPE)], axis=-1)
    ukv = mla_w_ukv.reshape(depth, MLA_KV_RANK, MLA_HEADS, MLA_NOPE + MLA_V_HD)
    ukt = ukv[..., :MLA_NOPE].reshape(depth, MLA_KV_RANK, MLA_HEADS * MLA_NOPE).transpose(0, 2, 1)
    uv = ukv[..., MLA_NOPE:].reshape(depth, MLA_KV_RANK, MLA_W)
    vecs = [norm_g, mla_qa_g, mla_kva_g, conv_dw_b, conv_ln_g, conv_ln_b]
    vecs = jnp.stack([_pad_last(v, D_MODEL) for v in vecs] + [jnp.zeros_like(norm_g)] * (SUBLANES - len(vecs)), axis=1)
    w_main, w_lat, w_gate = _split_w_in(w_in)
    return {
        "vecs": vecs,
        "w_main": w_main,
        "w_lat": w_lat,
        "w_gate": w_gate,
        "w_uq": uq.astype(BF16),
        "w_ukt": ukt.astype(BF16),
        "w_uv": uv.astype(BF16),
        "conv_w": jnp.pad(conv_dw_w, ((0, 0), (0, 1), (0, 0))),
        "ret_decay": jnp.broadcast_to(ret_decay_logit[:, :, :, None, None], (depth, 2, RET_HEADS, SUBLANES, LANES)),
        "w_out": w_out,
        "final_g": final_g[None, :],
    }


def _rope_inputs(seq):
    cos_r, sin_r = _rope_tables(seq, RET_HD)
    half = RET_HD // 2
    sin_r = np.concatenate([-sin_r[:, :half], sin_r[:, half:]], axis=1)
    cos_m, sin_m = _rope_tables(seq, MLA_ROPE)
    half = MLA_ROPE // 2
    zeros = np.zeros((seq, half), np.float32)
    cosm = np.concatenate([cos_m, cos_m], axis=1)
    sinma = np.concatenate([-sin_m[:, :half], zeros, -sin_m[:, :half], zeros], axis=1)
    sinmb = np.concatenate([zeros, sin_m[:, half:], zeros, sin_m[:, half:]], axis=1)
    return tuple(jnp.asarray(t) for t in (cos_r, sin_r, cosm, sinma, sinmb))


def kernel(x, norm_g, w_in, conv_dw_w, conv_dw_b, conv_ln_g, conv_ln_b, ret_decay_logit,
           mla_qa_g, mla_w_uq, mla_kva_g, mla_w_ukv, w_out, final_g):
    batch, seq, d_model = x.shape
    depth = norm_g.shape[0]
    assert d_model == D_MODEL and seq % RET_CHUNK == 0
    p = _prep_params(norm_g, w_in, conv_dw_w, conv_dw_b, conv_ln_g, conv_ln_b, ret_decay_logit,
                     mla_qa_g, mla_w_uq, mla_kva_g, mla_w_ukv, w_out, final_g)
    tabs = _rope_inputs(seq)
    x2 = x.reshape(batch * seq, d_model)
    for layer in range(depth):
        hglu, rqkv, mq, mkt, mv, gate = _inproj(x2, p, layer, tabs, seq)
        yc = _conv(hglu, gate, p, layer, batch, seq)
        yr = _retention(rqkv, gate, p, layer, batch, seq)
        ym = _mla(mq, mkt, mv, gate, batch, seq)
        x2 = _outproj(x2, yc, yr, ym, p, layer, seq, final=(layer == depth - 1))
    return x2.reshape(batch, seq, d_model)
```
